```python
import math
import jax, jax.numpy as jnp
from jax import lax
import numpy as np

D_MODEL = 1024
BATCH = 8
SEQ = 2048
DEPTH = 4

CTX_LEN = 256
GRID_W = 64
N_EVEN = (DEPTH + 1) // 2
N_ODD = DEPTH // 2

A_HEADS = 4
A_DQK = 64
A_DV = 2 * A_DQK
A_WIDTH = A_HEADS * A_DV
B_HEADS = 4
B_DK = 64
B_DV = 128
B_WIDTH = B_HEADS * B_DV
B_GATE_RANK = 16
B_GATE_NORM = 16.0
C_HEADS = 8
C_DH = D_MODEL // C_HEADS
C_WIDTH = C_HEADS * C_DH

CHUNK = 64
Q_BLOCK = 128
ROPE_BASE = 10000.0
EPS = 1e-6
D_FF = ((8 * D_MODEL + 3 * 256 - 1) // (3 * 256)) * 256

EVEN_SIZES = (2 * A_HEADS * A_DQK, 2 * A_HEADS * A_DQK, A_WIDTH,
              B_HEADS * B_DK, B_HEADS * B_DK, B_WIDTH, B_WIDTH, 2 * B_GATE_RANK)
EVEN_COLS = sum(EVEN_SIZES)
EVEN_SPLITS = tuple(int(s) for s in np.cumsum(EVEN_SIZES)[:-1])
ODD_COLS = 5 * C_WIDTH

kernel_name = "hybrid_diffattn_gla_hgrn2_prefix_dit"

F32 = jnp.float32


def rms_norm(x, gain):
    xf = x.astype(F32)
    y = xf * lax.rsqrt(jnp.mean(xf * xf, axis=-1, keepdims=True) + EPS)
    return (y * gain.astype(F32)).astype(x.dtype)


def axial_rope_tables(rows):
    row_ids = jnp.repeat(jnp.arange(rows), GRID_W).astype(F32)
    col_ids = jnp.tile(jnp.arange(GRID_W), rows).astype(F32)
    n_axis = A_DQK // 2
    inv = ROPE_BASE ** (-jnp.arange(0, n_axis, 2, dtype=F32) / n_axis)
    ang = jnp.concatenate([row_ids[:, None] * inv, col_ids[:, None] * inv], axis=-1)
    return jnp.cos(ang), jnp.sin(ang)


def apply_axial_rope(x, cos, sin):
    def rot(u, cs, sn):
        cs = cs[:, None, None, :]
        sn = sn[:, None, None, :]
        u1, u2 = jnp.split(u, 2, axis=-1)
        return jnp.concatenate([u1 * cs - u2 * sn, u2 * cs + u1 * sn], axis=-1)
    xr, xc = jnp.split(x, 2, axis=-1)
    cr, cc = jnp.split(cos, 2, axis=-1)
    sr, sc = jnp.split(sin, 2, axis=-1)
    return jnp.concatenate([rot(xr, cr, sr), rot(xc, cc, sc)], axis=-1).astype(x.dtype)


def diff_attend(q, k, v, lam):
    s = jnp.einsum('bqhcd,bkhcd->bhcqk', q, k).astype(F32) * (A_DQK ** -0.5)
    p = jax.nn.softmax(s, axis=-1)
    a = (p[:, :, 0] - lam * p[:, :, 1]).astype(v.dtype)
    return jnp.einsum('bhqk,bkhe->bqhe', a, v)


def chunked_gated_scan(q, k, v, logf, s0):
    Bn, L, H, _ = q.shape
    dv = v.shape[-1]
    n = L // CHUNK

    def to_chunks(a):
        return jnp.moveaxis(a.reshape(Bn, n, CHUNK, H, a.shape[-1]), 1, 0)

    mask = jnp.tril(jnp.ones((CHUNK, CHUNK), dtype=bool))[None, :, :, None, None]

    def step(S, inp):
        qc, kc, vc, gc = inp
        qf, kf, vf = qc.astype(F32), kc.astype(F32), vc.astype(F32)
        b = jnp.cumsum(gc.astype(F32), axis=1)
        o_inter = jnp.einsum('bthd,bhde->bthe', qf * jnp.exp(b), S)
        rel = b[:, :, None] - b[:, None, :]
        dec = jnp.exp(jnp.where(mask, rel, -jnp.inf))
        att = jnp.einsum('bthd,bshd,btshd->bhts', qf, kf, dec)
        o_intra = jnp.einsum('bhts,bshe->bthe', att, vf)
        b_last = b[:, -1]
        k_dec = kf * jnp.exp(b_last[:, None] - b)
        S_new = jnp.exp(b_last)[..., None] * S + jnp.einsum('bshd,bshe->bhde', k_dec, vf)
        return S_new, (o_inter + o_intra).astype(v.dtype)

    S, o = lax.scan(step, s0.astype(F32), (to_chunks(q), to_chunks(k), to_chunks(v), to_chunks(logf)))
    return jnp.moveaxis(o, 0, 1).reshape(Bn, L, H, dv), S


def bidir_scan(qc, kfc, gfc, kbc, gbc, vc, ql, kfl, gfl, kbl, gbl, vl, need_ctx):
    Bn, _, H, dk = qc.shape
    s0 = jnp.zeros((Bn, H, dk, vc.shape[-1]), F32)
    fl = lambda a: jnp.flip(a, axis=1)
    oc_f, sc_f = chunked_gated_scan(qc, kfc, vc, gfc, s0)
    ol_f, _ = chunked_gated_scan(ql, kfl, vl, gfl, sc_f)
    oc_b, sc_b = chunked_gated_scan(fl(qc), fl(kbc), fl(vc), fl(gbc), s0)
    ol_b, _ = chunked_gated_scan(fl(ql), fl(kbl), fl(vl), fl(gbl), sc_b)
    ol = ol_f + fl(ol_b)
    oc = oc_f + fl(oc_b) if need_ctx else None
    return oc, ol


def prep_even(p, qk_gain, w_gate_up, b_gate_up):
    Bn, L = p.shape[:2]
    aq, ak, av, bq, bk, bv, bg, blr = jnp.split(p, EVEN_SPLITS, axis=-1)
    aq = rms_norm(aq.reshape(Bn, L, A_HEADS, 2, A_DQK), qk_gain[0])
    ak = rms_norm(ak.reshape(Bn, L, A_HEADS, 2, A_DQK), qk_gain[1])
    av = av.reshape(Bn, L, A_HEADS, A_DV)
    bq = bq.reshape(Bn, L, B_HEADS, B_DK) * (B_DK ** -0.5)
    bk = bk.reshape(Bn, L, B_HEADS, B_DK)
    bv = bv.reshape(Bn, L, B_HEADS, B_DV)
    lr_f, lr_b = jnp.split(blr, 2, axis=-1)
    gk_f = (jax.nn.log_sigmoid((lr_f @ w_gate_up[0] + b_gate_up[0]).astype(F32)) / B_GATE_NORM).reshape(Bn, L, B_HEADS, B_DK)
    gk_b = (jax.nn.log_sigmoid((lr_b @ w_gate_up[1] + b_gate_up[1]).astype(F32)) / B_GATE_NORM).reshape(Bn, L, B_HEADS, B_DK)
    return aq, ak, av, (bq, bk, gk_f, bk, gk_b, bv), bg


def even_mixer(pc, pl, rope_cos, rope_sin, qk_gain, lam, sub_gain, lambda_init,
               w_gate_up, b_gate_up, gla_gain, need_ctx):
    qc, kc, vc, bc, bgc = prep_even(pc, qk_gain, w_gate_up, b_gate_up)
    ql, kl, vl, bl, bgl = prep_even(pl, qk_gain, w_gate_up, b_gate_up)
    ql = apply_axial_rope(ql, rope_cos, rope_sin)
    kl = apply_axial_rope(kl, rope_cos, rope_sin)
    k_all = jnp.concatenate([kc, kl], axis=1)
    v_all = jnp.concatenate([vc, vl], axis=1)
    Bn, T = ql.shape[:2]
    nb = T // Q_BLOCK
    q_blocks = jnp.moveaxis(ql.reshape(Bn, nb, Q_BLOCK, A_HEADS, 2, A_DQK), 1, 0)
    o_blocks = lax.map(lambda qb: diff_attend(qb, k_all, v_all, lam), q_blocks)
    oa_l = jnp.moveaxis(o_blocks, 0, 1).reshape(Bn, T, A_HEADS, A_DV)

    def post_a(o):
        return (rms_norm(o, sub_gain) * (1.0 - lambda_init)).reshape(o.shape[0], o.shape[1], A_WIDTH)

    def post_b(o, g):
        return (rms_norm(o, gla_gain) * jax.nn.silu(g).reshape(o.shape)).reshape(o.shape[0], o.shape[1], B_WIDTH)

    ob_c, ob_l = bidir_scan(*bc, *bl, need_ctx)
    ol = jnp.concatenate([post_a(oa_l), post_b(ob_l, bgl)], axis=-1)
    oc = None
    if need_ctx:
        oa_c = diff_attend(qc, kc, vc, lam)
        oc = jnp.concatenate([post_a(oa_c), post_b(ob_c, bgc)], axis=-1)
    return oc, ol


def prep_odd(p, lb_f, lb_b):
    Bn, L = p.shape[:2]
    hd = lambda a: a.reshape(Bn, L, C_HEADS, C_DH)
    q, ff, fb, i, g = jnp.split(p, 5, axis=-1)
    q = hd(jax.nn.silu(q)) * (C_DH ** -0.5)
    f_f = lb_f + (1.0 - lb_f) * jax.nn.sigmoid(ff.astype(F32))
    f_b = lb_b + (1.0 - lb_b) * jax.nn.sigmoid(fb.astype(F32))
    return (q, hd(1.0 - f_f), hd(jnp.log(f_f)), hd(1.0 - f_b), hd(jnp.log(f_b)), hd(i)), g


def odd_mixer(pc, pl, lb_f, lb_b, out_gain, need_ctx):
    sc, gc = prep_odd(pc, lb_f, lb_b)
    sl, gl = prep_odd(pl, lb_f, lb_b)
    oc, ol = bidir_scan(*sc, *sl, need_ctx)

    def post(o, g):
        return (rms_norm(o, out_gain) * jax.nn.silu(g).reshape(o.shape)).reshape(o.shape[0], o.shape[1], C_WIDTH)

    return (post(oc, gc) if need_ctx else None), post(ol, gl)


def swiglu(h, w_in, w_out):
    gate, up = jnp.split(h @ w_in, 2, axis=-1)
    return (jax.nn.silu(gate) * up) @ w_out


def setup_inputs(seed: int = 0) -> dict:
    key = jax.random.key(seed)
    ks = jax.random.split(key, 24)
    D = D_MODEL
    nrm = lambda k, shape, scale: jax.random.normal(k, shape, jnp.float32) * scale
    return {
        "x": nrm(ks[0], (BATCH, SEQ, D), 1.0),
        "c": nrm(ks[1], (BATCH, D), 1.0),
        "ctx": nrm(ks[2], (BATCH, CTX_LEN, D), 1.0),
        "c_ctx": nrm(ks[3], (D,), 1.0),
        "w_ada": nrm(ks[4], (DEPTH, D, 6 * D), 0.5 * D ** -0.5),
        "b_ada": nrm(ks[5], (DEPTH, 6 * D), 0.02),
        "norm1_gain": 1.0 + nrm(ks[6], (DEPTH, D), 0.02),
        "norm2_gain": 1.0 + nrm(ks[7], (DEPTH, D), 0.02),
        "w_in_even": nrm(ks[8], (N_EVEN, D, EVEN_COLS), D ** -0.5),
        "qk_gain_a": 1.0 + nrm(ks[9], (N_EVEN, 2, A_DQK), 0.02),
        "lambda_a": nrm(ks[10], (N_EVEN, 4, A_DQK), 0.1),
        "subln_gain_a": 1.0 + nrm(ks[11], (N_EVEN, A_DV), 0.02),
        "w_gate_up_b": nrm(ks[12], (N_EVEN, 2, B_GATE_RANK, B_HEADS * B_DK), B_GATE_RANK ** -0.5),
        "b_gate_up_b": nrm(ks[13], (N_EVEN, 2, B_HEADS * B_DK), 0.01),
        "onorm_gain_b": 1.0 + nrm(ks[14], (N_EVEN, B_DV), 0.02),
        "w_out_even": nrm(ks[15], (N_EVEN, A_WIDTH + B_WIDTH, D), (A_WIDTH + B_WIDTH) ** -0.5),
        "w_in_odd": nrm(ks[16], (N_ODD, D, ODD_COLS), D ** -0.5),
        "lb_raw_c": nrm(ks[17], (2, DEPTH, C_WIDTH), 0.5),
        "onorm_gain_c": 1.0 + nrm(ks[18], (N_ODD, C_DH), 0.02),
        "w_out_odd": nrm(ks[19], (N_ODD, C_WIDTH, D), C_WIDTH ** -0.5),
        "w_ffn_in": nrm(ks[20], (DEPTH, D, 2 * D_FF), D ** -0.5),
        "w_ffn_out": nrm(ks[21], (DEPTH, D_FF, D), D_FF ** -0.5),
    }


def reference(x, c, ctx, c_ctx, w_ada, b_ada, norm1_gain, norm2_gain, w_in_even, qk_gain_a,
              lambda_a, subln_gain_a, w_gate_up_b, b_gate_up_b, onorm_gain_b, w_out_even,
              w_in_odd, lb_raw_c, onorm_gain_c, w_out_odd, w_ffn_in, w_ffn_out):
    rows = x.shape[1] // GRID_W
    rope_cos, rope_sin = axial_rope_tables(rows)
    lb_p = jax.nn.softmax(lb_raw_c.astype(F32), axis=1)
    lower_bounds = jnp.cumsum(lb_p, axis=1) - lb_p[:, :1]
    z = ctx
    sc = jax.nn.silu(c)
    scc = jax.nn.silu(c_ctx)
    for l in range(DEPTH):
        need_ctx = l < DEPTH - 1
        mod_l = [m[..., None, :] for m in jnp.split(sc @ w_ada[l] + b_ada[l], 6, axis=-1)]
        mod_c = [m[..., None, :] for m in jnp.split(scc @ w_ada[l] + b_ada[l], 6, axis=-1)]
        hl = rms_norm(x, norm1_gain[l]) * (1.0 + mod_l[1]) + mod_l[0]
        hc = rms_norm(z, norm1_gain[l]) * (1.0 + mod_c[1]) + mod_c[0]
        if l % 2 == 0:
            j = l // 2
            lambda_init = 0.8 - 0.6 * math.exp(-0.3 * l)
            lq1, lk1, lq2, lk2 = lambda_a[j].astype(F32)
            lam = jnp.exp(jnp.sum(lq1 * lk1)) - jnp.exp(jnp.sum(lq2 * lk2)) + lambda_init
            oc, ol = even_mixer(hc @ w_in_even[j], hl @ w_in_even[j], rope_cos, rope_sin,
                                qk_gain_a[j], lam, subln_gain_a[j], lambda_init,
                                w_gate_up_b[j], b_gate_up_b[j], onorm_gain_b[j], need_ctx)
            w_out = w_out_even[j]
        else:
            j = l // 2
            oc, ol = odd_mixer(hc @ w_in_odd[j], hl @ w_in_odd[j], lower_bounds[0, l],
                               lower_bounds[1, l], onorm_gain_c[j], need_ctx)
            w_out = w_out_odd[j]
        x = x + mod_l[2] * (ol @ w_out)
        hl = rms_norm(x, norm2_gain[l]) * (1.0 + mod_l[4]) + mod_l[3]
        x = x + mod_l[5] * swiglu(hl, w_ffn_in[l], w_ffn_out[l])
        if need_ctx:
            z = z + mod_c[2] * (oc @ w_out)
            hc = rms_norm(z, norm2_gain[l]) * (1.0 + mod_c[4]) + mod_c[3]
            z = z + mod_c[5] * swiglu(hc, w_ffn_in[l], w_ffn_out[l])
    return x
```

```python
import functools
import math

import numpy as np
import jax
import jax.numpy as jnp
from jax import lax
from jax.experimental import pallas as pl
from jax.experimental.pallas import tpu as pltpu

F32 = jnp.float32
BF16 = jnp.bfloat16

D_MODEL = 1024
DEPTH = 4
CTX_LEN = 256
GRID_W = 64
A_HEADS = 4
A_DQK = 64
B_HEADS = 4
B_DK = 64
B_GATE_RANK = 16
B_GATE_NORM = 16.0
C_HEADS = 8
C_DH = 128
ROPE_BASE = 10000.0
EPS = 1e-6
D_FF = 2816
EVEN_COLS = 3104
EVEN_COLS_PAD = 3200
ODD_COLS = 5120

LANES = 128
ROW_TILE = 256
COL_CHUNK = 512
FF_CHUNK = 1408
Q_TILE = 256
K_CHUNK = 256
SCAN_CHUNK = 128
VMEM_LIMIT = 52 * 1024 * 1024


def _cparams(sem):
    return pltpu.CompilerParams(dimension_semantics=sem, vmem_limit_bytes=VMEM_LIMIT)


def _resident(shape):
    nd = len(shape)
    return pl.BlockSpec(shape, lambda *_: (0,) * nd, pipeline_mode=pl.Buffered(1))


def _silu(x):
    return x * jax.nn.sigmoid(x)


def _dot(a, b):
    return jnp.dot(a, b, preferred_element_type=F32)


def _dot_nt(a, b):
    return lax.dot_general(a, b, (((1,), (1,)), ((), ())), preferred_element_type=F32)


def _dot_tn(a, b):
    return lax.dot_general(a, b, (((0,), (0,)), ((), ())), preferred_element_type=F32)


def _ada_kernel(c_ref, w_ref, b_ref, o_ref):
    s = _silu(c_ref[...]).astype(BF16)
    o_ref[...] = _dot(s, w_ref[...].astype(BF16)) + b_ref[...]


def _ada(cc, w_ada, b_ada):
    depth, d, n = w_ada.shape
    rows = cc.shape[0]
    tn = 1536
    return pl.pallas_call(
        _ada_kernel,
        out_shape=jax.ShapeDtypeStruct((depth, rows, n), F32),
        grid=(depth, n // tn),
        in_specs=[
            pl.BlockSpec((rows, d), lambda l, j: (0, 0)),
            pl.BlockSpec((None, d, tn), lambda l, j: (l, 0, j)),
            pl.BlockSpec((None, 1, tn), lambda l, j: (l, 0, j)),
        ],
        out_specs=pl.BlockSpec((None, rows, tn), lambda l, j: (l, 0, j)),
        compiler_params=_cparams(("parallel", "parallel")),
        name="ada",
    )(cc, w_ada, b_ada.reshape(depth, 1, n))


def _norm_mod(x, gain, shift, scale):
    var = jnp.mean(x * x, axis=-1, keepdims=True)
    return (x * lax.rsqrt(var + EPS) * gain) * (1.0 + scale) + shift


def _inproj_kernel(x_ref, mod_ref, gain_ref, w_ref, o_ref):
    h = _norm_mod(x_ref[...], gain_ref[...], mod_ref[0:1, :], mod_ref[1:2, :]).astype(BF16)
    ncols = w_ref.shape[1]
    for c0 in range(0, ncols, COL_CHUNK):
        c1 = min(c0 + COL_CHUNK, ncols)
        o_ref[:, c0:c1] = _dot(h, w_ref[:, c0:c1])


def _tiles_per_sample(seq_all):
    return seq_all // ROW_TILE


def _mod_row(i, tiles, batch):
    return jnp.where(i % tiles == 0, batch, i // tiles)


def _inproj(xz, mod, gain, w, batch):
    rows, d = xz.shape
    ncols = w.shape[1]
    tiles = _tiles_per_sample(rows // batch)
    return pl.pallas_call(
        _inproj_kernel,
        out_shape=jax.ShapeDtypeStruct((rows, ncols), F32),
        grid=(rows // ROW_TILE,),
        in_specs=[
            pl.BlockSpec((ROW_TILE, d), lambda i: (i, 0)),
            pl.BlockSpec((None, 6, d), lambda i: (_mod_row(i, tiles, batch), 0, 0)),
            _resident((1, d)),
            _resident((d, ncols)),
        ],
        out_specs=pl.BlockSpec((ROW_TILE, ncols), lambda i: (i, 0)),
        compiler_params=_cparams(("parallel",)),
        name="inproj",
    )(xz, mod, gain.reshape(1, d), w)


def _post_kernel(*refs, n_parts):
    o_refs = refs[:n_parts]
    x_ref, mod_ref, gain_ref = refs[n_parts:n_parts + 3]
    wo_refs = refs[n_parts + 3:2 * n_parts + 3]
    wi_ref, wf_ref, out_ref = refs[2 * n_parts + 3:]
    y = _dot(o_refs[0][...], wo_refs[0][...])
    for p in range(1, n_parts):
        y = y + _dot(o_refs[p][...], wo_refs[p][...])
    x1 = x_ref[...] + mod_ref[2:3, :] * y
    h = _norm_mod(x1, gain_ref[...], mod_ref[3:4, :], mod_ref[4:5, :]).astype(BF16)
    acc = None
    for c0 in range(0, D_FF, FF_CHUNK):
        gate = _dot(h, wi_ref[:, c0:c0 + FF_CHUNK])
        up = _dot(h, wi_ref[:, D_FF + c0:D_FF + c0 + FF_CHUNK])
        a = (_silu(gate) * up).astype(BF16)
        part = _dot(a, wf_ref[c0:c0 + FF_CHUNK, :])
        acc = part if acc is None else acc + part
    out_ref[...] = x1 + mod_ref[5:6, :] * acc


def _post(o_parts, xz, mod, gain, wo_parts, wi, wf, batch, latent_only):
    rows, d = xz.shape
    tiles = _tiles_per_sample(rows // batch)
    if latent_only:
        out_tiles = tiles - 1
        src = lambda i: (i // out_tiles) * tiles + 1 + i % out_tiles
        mrow = lambda i: i // out_tiles
    else:
        out_tiles = tiles
        src = lambda i: i
        mrow = lambda i: _mod_row(i, tiles, batch)
    n_parts = len(o_parts)
    in_specs = [pl.BlockSpec((ROW_TILE, o.shape[1]), lambda i: (src(i), 0)) for o in o_parts]
    in_specs += [
        pl.BlockSpec((ROW_TILE, d), lambda i: (src(i), 0)),
        pl.BlockSpec((None, 6, d), lambda i: (mrow(i), 0, 0)),
        _resident((1, d)),
    ]
    in_specs += [_resident(w.shape) for w in wo_parts]
    in_specs += [_resident(wi.shape), _resident(wf.shape)]
    return pl.pallas_call(
        functools.partial(_post_kernel, n_parts=n_parts),
        out_shape=jax.ShapeDtypeStruct((batch * out_tiles * ROW_TILE, d), F32),
        grid=(batch * out_tiles,),
        in_specs=in_specs,
        out_specs=pl.BlockSpec((ROW_TILE, d), lambda i: (i, 0)),
        compiler_params=_cparams(("parallel",)),
        name="post",
    )(*o_parts, xz, mod, gain.reshape(1, d), *wo_parts, wi, wf)


def _attn_kernel(q_ref, k_ref, v_ref, cq_ref, saq_ref, sbq_ref, ck_ref, sak_ref, sbk_ref,
                 qg_ref, kg_ref, sg_ref, lam_ref, bd_ref, o_ref, kb_ref, vb_ref,
                 *, lambda_init, seq_all, ctx_first):
    qi = pl.program_id(2)

    def norm_rope(x, g, c, sa, sb):
        ms = _dot((x * x).astype(BF16), bd_ref[...])
        y = x * lax.rsqrt(ms + EPS) * g
        return y * c + pltpu.roll(y, LANES - 16, 1) * sa + pltpu.roll(y, 16, 1) * sb

    @pl.when(qi == 0)
    def _prep_keys():
        def body(r, carry):
            rows = pl.ds(pl.multiple_of(r * ROW_TILE, ROW_TILE), ROW_TILE)
            kb_ref[rows, :] = norm_rope(k_ref[rows, :], kg_ref[...], ck_ref[rows, :],
                                        sak_ref[rows, :], sbk_ref[rows, :]).astype(BF16)
            vb_ref[rows, :] = v_ref[rows, :].astype(BF16)
            return carry
        lax.fori_loop(0, seq_all // ROW_TILE, body, 0)

    qn = norm_rope(q_ref[...], qg_ref[...], cq_ref[...], saq_ref[...], sbq_ref[...]) * (A_DQK ** -0.5)
    lane = lax.broadcasted_iota(jnp.int32, qn.shape, 1)
    q0 = jnp.where(lane < A_DQK, qn, 0.0).astype(BF16)
    q1 = jnp.where(lane >= A_DQK, qn, 0.0).astype(BF16)

    def online(qc, kc, vc, m, l, acc):
        s = _dot_nt(qc, kc)
        mn = jnp.maximum(m, jnp.max(s, axis=-1, keepdims=True))
        p = jnp.exp(s - mn)
        alpha = jnp.exp(m - mn)
        l = alpha * l + jnp.sum(p, axis=-1, keepdims=True)
        acc = alpha * acc + _dot(p.astype(BF16), vc)
        return mn, l, acc

    def body(c, carry):
        m0, l0, a0, m1, l1, a1 = carry
        rows = pl.ds(pl.multiple_of(c * K_CHUNK, K_CHUNK), K_CHUNK)
        kc = kb_ref[rows, :]
        vc = vb_ref[rows, :]
        m0, l0, a0 = online(q0, kc, vc, m0, l0, a0)
        m1, l1, a1 = online(q1, kc, vc, m1, l1, a1)
        return m0, l0, a0, m1, l1, a1

    tq = qn.shape[0]
    neg = jnp.full((tq, 1), -jnp.inf, F32)
    zl = jnp.zeros((tq, 1), F32)
    za = jnp.zeros((tq, LANES), F32)
    if ctx_first:
        n_chunks = jnp.where(qi == 0, CTX_LEN // K_CHUNK, seq_all // K_CHUNK)
    else:
        n_chunks = seq_all // K_CHUNK
    m0, l0, a0, m1, l1, a1 = lax.fori_loop(0, n_chunks, body, (neg, zl, za, neg, zl, za))

    lam_v = lam_ref[...]
    e1 = jnp.exp(jnp.sum(lam_v[0:1, :] * lam_v[1:2, :], axis=-1, keepdims=True))
    e2 = jnp.exp(jnp.sum(lam_v[2:3, :] * lam_v[3:4, :], axis=-1, keepdims=True))
    lam = e1 - e2 + lambda_init
    o = a0 / l0 - lam * (a1 / l1)
    var = jnp.mean(o * o, axis=-1, keepdims=True)
    o_ref[...] = (o * lax.rsqrt(var + EPS) * sg_ref[...] * (1.0 - lambda_init)).astype(BF16)


def _attn(p3, tabs, qk_gain, lam, sub_gain, bd, lambda_init, need_ctx):
    batch, seq_all, _ = p3.shape
    nq = seq_all // Q_TILE
    q0 = 0 if need_ctx else 1
    cq, sa, sb = tabs
    qg = jnp.tile(qk_gain[0], 2).reshape(1, LANES)
    kg = jnp.tile(qk_gain[1], 2).reshape(1, LANES)
    qspec = lambda b, h, i: (b, i + q0, h)
    tspec = pl.BlockSpec((Q_TILE, LANES), lambda b, h, i: (i + q0, 0))
    full = _resident((seq_all, LANES))
    kern = functools.partial(_attn_kernel, lambda_init=lambda_init, seq_all=seq_all, ctx_first=need_ctx)
    return pl.pallas_call(
        kern,
        out_shape=jax.ShapeDtypeStruct((batch, seq_all, A_HEADS * LANES), BF16),
        grid=(batch, A_HEADS, nq - q0),
        in_specs=[
            pl.BlockSpec((None, Q_TILE, LANES), qspec),
            pl.BlockSpec((None, seq_all, LANES), lambda b, h, i: (b, 0, A_HEADS + h)),
            pl.BlockSpec((None, seq_all, LANES), lambda b, h, i: (b, 0, 2 * A_HEADS + h)),
            tspec, tspec, tspec, full, full, full,
            _resident((1, LANES)), _resident((1, LANES)), _resident((1, LANES)),
            _resident((4, A_DQK)), _resident((LANES, LANES)),
        ],
        out_specs=pl.BlockSpec((None, Q_TILE, LANES), qspec),
        scratch_shapes=[pltpu.VMEM((seq_all, LANES), BF16), pltpu.VMEM((seq_all, LANES), BF16)],
        compiler_params=_cparams(("parallel", "parallel", "arbitrary")),
        name="attn",
    )(p3, p3, p3, cq, sa, sb, cq, sa, sb, qg, kg, sub_gain.reshape(1, LANES), lam, bd)


def _scan_levels():
    return int(math.log2(SCAN_CHUNK))


def _scan_consts(n_heads):
    c = SCAN_CHUNK
    nl = _scan_levels()
    t = np.arange(c)[:, None]
    u = np.arange(c)[None, :]
    blocks_f = [u <= t, u > t]
    blocks_b = [u >= t, u < t]
    for lvl in range(1, nl + 1):
        h = 1 << (lvl - 1)
        m = (t // (2 * h)) * (2 * h) + h
        second = (t % (2 * h)) >= h
        blocks_f.append(np.where(second, (u >= m) & (u <= t), (u > t) & (u <= m - 1)))
        blocks_b.append(np.where(second, (u >= m) & (u < t), (u >= t) & (u <= m - 1)))
    a_f = np.concatenate(blocks_f, axis=0).astype(np.float32)
    a_b = np.concatenate(blocks_b, axis=0).astype(np.float32)
    x = t ^ u
    lv = np.where(x > 0, np.floor(np.log2(np.maximum(x, 1))).astype(np.int32) + 1, 0)
    lv_f = np.where(u <= t, lv, -1).astype(np.int32)
    lv_b = np.where(u >= t, lv, -1).astype(np.int32)
    lv_f = np.tile(lv_f, (n_heads, 1))
    lv_b = np.tile(lv_b, (n_heads, 1))
    return (jnp.asarray(a_f, BF16), jnp.asarray(a_b, BF16), jnp.asarray(lv_f), jnp.asarray(lv_b))


def _scan_chunk(q, k, g, vb, st, a_ref, lv_ref, *, n_heads, backward):
    c = SCAN_CHUNK
    nl = _scan_levels()
    dk = LANES // n_heads
    g_hi = g.astype(BF16)
    g_lo = (g - g_hi.astype(F32)).astype(BF16)
    sums = _dot(a_ref[...], jnp.concatenate([g_hi, g_lo], axis=1))
    e = jnp.exp(sums[:, :LANES] + sums[:, LANES:])
    e_in = e[0:c]
    e_out = e[c:2 * c]
    dec = e_in[0:1, :] if backward else e_in[c - 1:c, :]

    lane = lax.broadcasted_iota(jnp.int32, (c, LANES), 1)

    def by_head(xb):
        if n_heads == 1:
            return xb
        zero = jnp.zeros_like(xb)
        return jnp.concatenate(
            [jnp.where((lane >= hh * dk) & (lane < (hh + 1) * dk), xb, zero) for hh in range(n_heads)], axis=0)

    o = _dot_nt((q * e_in).astype(BF16), st.astype(BF16))

    lv = lv_ref[...]
    att = jnp.zeros((n_heads * c, c), F32)
    for lvl in range(nl + 1):
        if lvl == 0:
            qh, kh = q.astype(BF16), k.astype(BF16)
        else:
            el = e[(1 + lvl) * c:(2 + lvl) * c]
            qh, kh = (q * el).astype(BF16), (k * el).astype(BF16)
        att = jnp.where(lv == lvl, _dot_nt(by_head(qh), kh), att)
    attb = att.astype(BF16)
    intra = [_dot(attb[hh * c:(hh + 1) * c], vb[:, hh * LANES:(hh + 1) * LANES]) for hh in range(n_heads)]
    o = o + (intra[0] if n_heads == 1 else jnp.concatenate(intra, axis=1))

    upd = _dot_tn(vb, (k * e_out).astype(BF16))
    if n_heads > 1:
        r = lax.broadcasted_iota(jnp.int32, upd.shape, 0) // LANES
        cc = lax.broadcasted_iota(jnp.int32, upd.shape, 1) // dk
        upd = jnp.where(r == cc, upd, 0.0)
    return o, dec * st + upd


def _scan_loop(load_fwd, load_bwd, acc_ref, af_ref, ab_ref, lvf_ref, lvb_ref, *, n_heads, seq_all):
    c = SCAN_CHUNK
    n = seq_all // c
    n_ctx = CTX_LEN // c
    acc_ref[...] = jnp.zeros_like(acc_ref)
    st0 = jnp.zeros((n_heads * LANES, LANES), F32)

    def body(j, carry):
        st_f, st_b = carry
        rf = pl.ds(pl.multiple_of(j * c, c), c)
        jb = jnp.where(j < n_ctx, n_ctx - 1 - j, n - 1 - (j - n_ctx))
        rb = pl.ds(pl.multiple_of(jb * c, c), c)
        q, k, g, vb = load_fwd(rf)
        o_f, st_f = _scan_chunk(q, k, g, vb, st_f, af_ref, lvf_ref, n_heads=n_heads, backward=False)
        acc_ref[rf, :] += o_f
        q, k, g, vb = load_bwd(rb)
        o_b, st_b = _scan_chunk(q, k, g, vb, st_b, ab_ref, lvb_ref, n_heads=n_heads, backward=True)
        acc_ref[rb, :] += o_b
        return st_f, st_b

    lax.fori_loop(0, n, body, (st0, st0))


def _log_sigmoid(x):
    return jnp.minimum(x, 0.0) - jnp.log(1.0 + jnp.exp(-jnp.abs(x)))


def _gla_kernel(q_ref, k_ref, v_ref, g_ref, lr_ref, wgf_ref, wgb_ref, bf_ref, bb_ref, gain_ref,
                af_ref, ab_ref, lvf_ref, lvb_ref, o_ref, acc_ref, *, seq_all):
    def loader(wg_ref, b_ref):
        def load(rows):
            q = q_ref[rows, :] * (B_DK ** -0.5)
            k = k_ref[rows, :]
            pre = _dot(lr_ref[rows, :].astype(BF16), wg_ref[...]) + b_ref[...]
            g = _log_sigmoid(pre) * (1.0 / B_GATE_NORM)
            return q, k, g, v_ref[rows, :].astype(BF16)
        return load

    _scan_loop(loader(wgf_ref, bf_ref), loader(wgb_ref, bb_ref), acc_ref, af_ref, ab_ref, lvf_ref, lvb_ref,
               n_heads=2, seq_all=seq_all)

    def post(j, carry):
        rows = pl.ds(pl.multiple_of(j * ROW_TILE, ROW_TILE), ROW_TILE)
        for hh in range(2):
            cols = slice(hh * LANES, (hh + 1) * LANES)
            o = acc_ref[rows, cols]
            var = jnp.mean(o * o, axis=-1, keepdims=True)
            o_ref[rows, cols] = (o * lax.rsqrt(var + EPS) * gain_ref[...] * _silu(g_ref[rows, cols])).astype(BF16)
        return carry
    lax.fori_loop(0, seq_all // ROW_TILE, post, 0)


def _gla(p3, w_gate_up, b_gate_up, gain, consts):
    batch, seq_all, _ = p3.shape
    pairs = B_HEADS // 2
    wg = jnp.zeros((2, pairs, LANES, LANES), F32)
    for d in range(2):
        w = w_gate_up[d].reshape(B_GATE_RANK, pairs, LANES).transpose(1, 0, 2)
        wg = wg.at[d, :, d * B_GATE_RANK:(d + 1) * B_GATE_RANK, :].set(w)
    wg = wg.astype(BF16)
    bias = b_gate_up.reshape(2, pairs, 1, LANES)
    a_f, a_b, lv_f, lv_b = consts
    col = lambda off: (lambda b, p: (b, 0, off + p))
    q_off = 3 * A_HEADS * LANES // LANES
    k_off = q_off + B_HEADS * B_DK // LANES
    v_off = (k_off * LANES + B_HEADS * B_DK) // (2 * LANES)
    g_off = v_off + pairs
    lr_off = (g_off + pairs) * 2
    wspec = pl.BlockSpec((None, LANES, LANES), lambda b, p: (p, 0, 0))
    bspec = pl.BlockSpec((None, 1, LANES), lambda b, p: (p, 0, 0))
    return pl.pallas_call(
        functools.partial(_gla_kernel, seq_all=seq_all),
        out_shape=jax.ShapeDtypeStruct((batch, seq_all, B_HEADS * LANES), BF16),
        grid=(batch, pairs),
        in_specs=[
            pl.BlockSpec((None, seq_all, LANES), col(q_off)),
            pl.BlockSpec((None, seq_all, LANES), col(k_off)),
            pl.BlockSpec((None, seq_all, 2 * LANES), col(v_off)),
            pl.BlockSpec((None, seq_all, 2 * LANES), col(g_off)),
            pl.BlockSpec((None, seq_all, LANES), lambda b, p: (b, 0, lr_off)),
            wspec, wspec, bspec, bspec,
            _resident((1, LANES)),
            _resident(a_f.shape), _resident(a_b.shape), _resident(lv_f.shape), _resident(lv_b.shape),
        ],
        out_specs=pl.BlockSpec((None, seq_all, 2 * LANES), lambda b, p: (b, 0, p)),
        scratch_shapes=[pltpu.VMEM((seq_all, 2 * LANES), F32)],
        compiler_params=_cparams(("parallel", "parallel")),
        name="gla",
    )(p3, p3, p3, p3, p3, wg[0], wg[1], bias[0], bias[1], gain.reshape(1, LANES), a_f, a_b, lv_f, lv_b)


def _hgrn_kernel(q_ref, ff_ref, fb_ref, i_ref, g_ref, lb_ref, gain_ref,
                 af_ref, ab_ref, lvf_ref, lvb_ref, o_ref, acc_ref, *, seq_all, layer):
    def lower_bound(d):
        raw = lb_ref[d]
        ex = jnp.exp(raw - jnp.max(raw, axis=0, keepdims=True))
        p = ex / jnp.sum(ex, axis=0, keepdims=True)
        return jnp.sum(p[1:layer + 1], axis=0, keepdims=True)

    def loader(f_ref, lb):
        def load(rows):
            q = _silu(q_ref[rows, :]) * (C_DH ** -0.5)
            f = lb + (1.0 - lb) * jax.nn.sigmoid(f_ref[rows, :])
            return q, 1.0 - f, jnp.log(f), i_ref[rows, :].astype(BF16)
        return load

    _scan_loop(loader(ff_ref, lower_bound(0)), loader(fb_ref, lower_bound(1)), acc_ref,
               af_ref, ab_ref, lvf_ref, lvb_ref, n_heads=1, seq_all=seq_all)

    def post(j, carry):
        rows = pl.ds(pl.multiple_of(j * ROW_TILE, ROW_TILE), ROW_TILE)
        o = acc_ref[rows, :]
        var = jnp.mean(o * o, axis=-1, keepdims=True)
        o_ref[rows, :] = (o * lax.rsqrt(var + EPS) * gain_ref[...] * _silu(g_ref[rows, :])).astype(BF16)
        return carry
    lax.fori_loop(0, seq_all // ROW_TILE, post, 0)


def _hgrn(p3, lb_raw, gain, consts, layer):
    batch, seq_all, _ = p3.shape
    a_f, a_b, lv_f, lv_b = consts
    col = lambda off: (lambda b, h: (b, 0, off + h))
    return pl.pallas_call(
        functools.partial(_hgrn_kernel, seq_all=seq_all, layer=layer),
        out_shape=jax.ShapeDtypeStruct((batch, seq_all, C_HEADS * LANES), BF16),
        grid=(batch, C_HEADS),
        in_specs=[pl.BlockSpec((None, seq_all, LANES), col(s * C_HEADS)) for s in range(5)] + [
            pl.BlockSpec((2, DEPTH, LANES), lambda b, h: (0, 0, h)),
            _resident((1, LANES)),
            _resident(a_f.shape), _resident(a_b.shape), _resident(lv_f.shape), _resident(lv_b.shape),
        ],
        out_specs=pl.BlockSpec((None, seq_all, LANES), lambda b, h: (b, 0, h)),
        scratch_shapes=[pltpu.VMEM((seq_all, LANES), F32)],
        compiler_params=_cparams(("parallel", "parallel")),
        name="hgrn",
    )(p3, p3, p3, p3, p3, lb_raw, gain.reshape(1, LANES), a_f, a_b, lv_f, lv_b)


def _rope_tables(seq):
    pos = jnp.arange(seq)
    row_ids = (pos // GRID_W).astype(F32)
    col_ids = (pos % GRID_W).astype(F32)
    n_axis = A_DQK // 2
    inv = ROPE_BASE ** (-jnp.arange(0, n_axis, 2, dtype=F32) / n_axis)
    ang_r = row_ids[:, None] * inv
    ang_c = col_ids[:, None] * inv
    zeros = jnp.zeros_like(ang_r)
    cos64 = jnp.concatenate([jnp.cos(ang_r)] * 2 + [jnp.cos(ang_c)] * 2, axis=-1)
    sa64 = jnp.concatenate([-jnp.sin(ang_r), zeros, -jnp.sin(ang_c), zeros], axis=-1)
    sb64 = jnp.concatenate([zeros, jnp.sin(ang_r), zeros, jnp.sin(ang_c)], axis=-1)
    ident = lambda v: jnp.full((CTX_LEN, LANES), v, F32)
    full = lambda t, v: jnp.concatenate([ident(v), jnp.tile(t, (1, 2))], axis=0)
    return full(cos64, 1.0), full(sa64, 0.0), full(sb64, 0.0)


def kernel(x, c, ctx, c_ctx, w_ada, b_ada, norm1_gain, norm2_gain, w_in_even, qk_gain_a, lambda_a, subln_gain_a, w_gate_up_b, b_gate_up_b, onorm_gain_b, w_out_even, w_in_odd, lb_raw_c, onorm_gain_c, w_out_odd, w_ffn_in, w_ffn_out):
    batch, seq, d = x.shape
    seq_all = CTX_LEN + seq
    assert d == D_MODEL and ctx.shape[1] == CTX_LEN == ROW_TILE == Q_TILE and seq % ROW_TILE == 0

    xz = jnp.concatenate([ctx, x], axis=1).reshape(batch * seq_all, d)
    mod_rows = -(-(batch + 1) // 8) * 8
    cc = jnp.zeros((mod_rows, d), F32).at[:batch].set(c).at[batch].set(c_ctx)
    mods = _ada(cc, w_ada, b_ada).reshape(DEPTH, mod_rows, 6, d)

    tabs = _rope_tables(seq)
    half = np.arange(LANES) // A_DQK
    bd = jnp.asarray((half[:, None] == half[None, :]) / A_DQK, BF16)
    consts_gla = _scan_consts(2)
    consts_hgrn = _scan_consts(1)

    for l in range(DEPTH):
        j = l // 2
        last = l == DEPTH - 1
        wi = w_ffn_in[l].astype(BF16)
        wf = w_ffn_out[l].astype(BF16)
        if l % 2 == 0:
            w = jnp.pad(w_in_even[j], ((0, 0), (0, EVEN_COLS_PAD - EVEN_COLS))).astype(BF16)
            p3 = _inproj(xz, mods[l], norm1_gain[l], w, batch).reshape(batch, seq_all, EVEN_COLS_PAD)
            lambda_init = 0.8 - 0.6 * math.exp(-0.3 * l)
            oa = _attn(p3, tabs, qk_gain_a[j], lambda_a[j], subln_gain_a[j], bd, lambda_init, not last)
            ob = _gla(p3, w_gate_up_b[j], b_gate_up_b[j], onorm_gain_b[j], consts_gla)
            a_width = A_HEADS * LANES
            wo = w_out_even[j].astype(BF16)
            o_parts = [oa.reshape(batch * seq_all, -1), ob.reshape(batch * seq_all, -1)]
            wo_parts = [wo[:a_width], wo[a_width:]]
        else:
            p3 = _inproj(xz, mods[l], norm1_gain[l], w_in_odd[j].astype(BF16), batch).reshape(batch, seq_all, ODD_COLS)
            oc = _hgrn(p3, lb_raw_c, onorm_gain_c[j], consts_hgrn, l)
            o_parts = [oc.reshape(batch * seq_all, -1)]
            wo_parts = [w_out_odd[j].astype(BF16)]
        xz = _post(o_parts, xz, mods[l], norm2_gain[l], wo_parts, wi, wf, batch, last)
    return xz.reshape(batch, seq, d)
```

```python
import functools
import math

import numpy as np
import jax
import jax.numpy as jnp
from jax import lax
from jax.experimental import pallas as pl
from jax.experimental.pallas import tpu as pltpu

F32 = jnp.float32
BF16 = jnp.bfloat16

D_MODEL = 1024
DEPTH = 4
CTX_LEN = 256
GRID_W = 64
A_HEADS = 4
A_DQK = 64
B_HEADS = 4
B_DK = 64
B_GATE_RANK = 16
B_GATE_NORM = 16.0
C_HEADS = 8
C_DH = 128
ROPE_BASE = 10000.0
EPS = 1e-6
D_FF = 2816
EVEN_COLS = 3104
EVEN_COLS_PAD = 3200
ODD_COLS = 5120

LANES = 128
ROW_TILE = 256
COL_CHUNK = 512
FF_CHUNK = 1408
Q_TILE = 256
LOG2E = math.log2(math.e)
SCAN_CHUNK = 128
SCAN_UNROLL = 2
VMEM_LIMIT = 52 * 1024 * 1024


def _cparams(sem):
    return pltpu.CompilerParams(dimension_semantics=sem, vmem_limit_bytes=VMEM_LIMIT)


def _resident(shape):
    nd = len(shape)
    return pl.BlockSpec(shape, lambda *_: (0,) * nd, pipeline_mode=pl.Buffered(1))


def _silu(x):
    return x * jax.nn.sigmoid(x)


def _dot(a, b):
    return jnp.dot(a, b, preferred_element_type=F32)


def _dot_nt(a, b):
    return lax.dot_general(a, b, (((1,), (1,)), ((), ())), preferred_element_type=F32)


def _dot_tn(a, b):
    return lax.dot_general(a, b, (((0,), (0,)), ((), ())), preferred_element_type=F32)


def _ada_kernel(c_ref, w_ref, b_ref, o_ref):
    s = _silu(c_ref[...]).astype(BF16)
    o_ref[...] = _dot(s, w_ref[...].astype(BF16)) + b_ref[...]


def _ada(cc, w_ada, b_ada):
    depth, d, n = w_ada.shape
    rows = cc.shape[0]
    tn = 1536
    return pl.pallas_call(
        _ada_kernel,
        out_shape=jax.ShapeDtypeStruct((depth, rows, n), F32),
        grid=(depth, n // tn),
        in_specs=[
            pl.BlockSpec((rows, d), lambda l, j: (0, 0)),
            pl.BlockSpec((None, d, tn), lambda l, j: (l, 0, j)),
            pl.BlockSpec((None, 1, tn), lambda l, j: (l, 0, j)),
        ],
        out_specs=pl.BlockSpec((None, rows, tn), lambda l, j: (l, 0, j)),
        compiler_params=_cparams(("parallel", "parallel")),
        name="ada",
    )(cc, w_ada, b_ada.reshape(depth, 1, n))


def _norm_mod(x, gain, shift, scale):
    var = jnp.mean(x * x, axis=-1, keepdims=True)
    return (x * lax.rsqrt(var + EPS) * gain) * (1.0 + scale) + shift


def _inproj_kernel(x_ref, mod_ref, gain_ref, w_ref, o_ref):
    h = _norm_mod(x_ref[...], gain_ref[...], mod_ref[0:1, :], mod_ref[1:2, :]).astype(BF16)
    ncols = w_ref.shape[1]
    for c0 in range(0, ncols, COL_CHUNK):
        c1 = min(c0 + COL_CHUNK, ncols)
        o_ref[:, c0:c1] = _dot(h, w_ref[:, c0:c1])


def _tiles_per_sample(seq_all):
    return seq_all // ROW_TILE


def _mod_row(i, tiles, batch):
    return jnp.where(i % tiles == 0, batch, i // tiles)


def _inproj(xz, mod, gain, w, batch):
    rows, d = xz.shape
    ncols = w.shape[1]
    tiles = _tiles_per_sample(rows // batch)
    return pl.pallas_call(
        _inproj_kernel,
        out_shape=jax.ShapeDtypeStruct((rows, ncols), F32),
        grid=(rows // ROW_TILE,),
        in_specs=[
            pl.BlockSpec((ROW_TILE, d), lambda i: (i, 0)),
            pl.BlockSpec((None, 6, d), lambda i: (_mod_row(i, tiles, batch), 0, 0)),
            _resident((1, d)),
            _resident((d, ncols)),
        ],
        out_specs=pl.BlockSpec((ROW_TILE, ncols), lambda i: (i, 0)),
        compiler_params=_cparams(("parallel",)),
        name="inproj",
    )(xz, mod, gain.reshape(1, d), w)


def _post_kernel(*refs, n_parts):
    o_refs = refs[:n_parts]
    x_ref, mod_ref, gain_ref = refs[n_parts:n_parts + 3]
    wo_refs = refs[n_parts + 3:2 * n_parts + 3]
    wi_ref, wf_ref, out_ref = refs[2 * n_parts + 3:]
    y = _dot(o_refs[0][...], wo_refs[0][...])
    for p in range(1, n_parts):
        y = y + _dot(o_refs[p][...], wo_refs[p][...])
    x1 = x_ref[...] + mod_ref[2:3, :] * y
    h = _norm_mod(x1, gain_ref[...], mod_ref[3:4, :], mod_ref[4:5, :]).astype(BF16)
    acc = None
    for c0 in range(0, D_FF, FF_CHUNK):
        gate = _dot(h, wi_ref[:, c0:c0 + FF_CHUNK])
        up = _dot(h, wi_ref[:, D_FF + c0:D_FF + c0 + FF_CHUNK])
        a = (_silu(gate) * up).astype(BF16)
        part = _dot(a, wf_ref[c0:c0 + FF_CHUNK, :])
        acc = part if acc is None else acc + part
    out_ref[...] = x1 + mod_ref[5:6, :] * acc


def _post(o_parts, xz, mod, gain, wo_parts, wi, wf, batch, latent_only):
    rows, d = xz.shape
    tiles = _tiles_per_sample(rows // batch)
    if latent_only:
        out_tiles = tiles - 1
        src = lambda i: (i // out_tiles) * tiles + 1 + i % out_tiles
        mrow = lambda i: i // out_tiles
    else:
        out_tiles = tiles
        src = lambda i: i
        mrow = lambda i: _mod_row(i, tiles, batch)
    n_parts = len(o_parts)
    in_specs = [pl.BlockSpec((ROW_TILE, o.shape[1]), lambda i: (src(i), 0)) for o in o_parts]
    in_specs += [
        pl.BlockSpec((ROW_TILE, d), lambda i: (src(i), 0)),
        pl.BlockSpec((None, 6, d), lambda i: (mrow(i), 0, 0)),
        _resident((1, d)),
    ]
    in_specs += [_resident(w.shape) for w in wo_parts]
    in_specs += [_resident(wi.shape), _resident(wf.shape)]
    return pl.pallas_call(
        functools.partial(_post_kernel, n_parts=n_parts),
        out_shape=jax.ShapeDtypeStruct((batch * out_tiles * ROW_TILE, d), F32),
        grid=(batch * out_tiles,),
        in_specs=in_specs,
        out_specs=pl.BlockSpec((ROW_TILE, d), lambda i: (i, 0)),
        compiler_params=_cparams(("parallel",)),
        name="post",
    )(*o_parts, xz, mod, gain.reshape(1, d), *wo_parts, wi, wf)


def _attn_kernel(q_ref, k_ref, v_ref, cq_ref, saq_ref, sbq_ref, ck_ref, sak_ref, sbk_ref,
                 qg_ref, kg_ref, sg_ref, lam_ref, bd_ref, o_ref, kb_ref, vt_ref,
                 *, lambda_init, seq_all):
    qi = pl.program_id(2)

    def norm_rope(x, g, c, sa, sb):
        ms = _dot((x * x).astype(BF16), bd_ref[...])
        y = x * lax.rsqrt(ms + EPS) * g
        return y * c + pltpu.roll(y, LANES - 16, 1) * sa + pltpu.roll(y, 16, 1) * sb

    @pl.when(qi == 0)
    def _prep_keys():
        def body(r, carry):
            rows = pl.ds(pl.multiple_of(r * ROW_TILE, ROW_TILE), ROW_TILE)
            kb_ref[rows, :] = norm_rope(k_ref[rows, :], kg_ref[...], ck_ref[rows, :],
                                        sak_ref[rows, :], sbk_ref[rows, :]).astype(BF16)
            vt_ref[:, rows] = v_ref[rows, :].T.astype(BF16)
            return carry
        lax.fori_loop(0, seq_all // ROW_TILE, body, 0)

    qn = norm_rope(q_ref[...], qg_ref[...], cq_ref[...], saq_ref[...], sbq_ref[...]) * (A_DQK ** -0.5 * LOG2E)
    lane = lax.broadcasted_iota(jnp.int32, qn.shape, 1)
    q2 = jnp.concatenate([jnp.where(lane < A_DQK, qn, 0.0).astype(BF16),
                          jnp.where(lane >= A_DQK, qn, 0.0).astype(BF16)], axis=0)

    lam_v = lam_ref[...]
    e1 = jnp.exp(jnp.sum(lam_v[0:1, :] * lam_v[1:2, :], axis=-1, keepdims=True))
    e2 = jnp.exp(jnp.sum(lam_v[2:3, :] * lam_v[3:4, :], axis=-1, keepdims=True))
    lam = e1 - e2 + lambda_init

    def attend(n_keys):
        s = _dot_nt(kb_ref[0:n_keys, :], q2)
        m = jnp.max(s, axis=0, keepdims=True)
        p = jnp.exp2(s - m)
        l = jnp.sum(p, axis=0, keepdims=True)
        pv = _dot(vt_ref[:, 0:n_keys], p.astype(BF16)) / l
        o = pv[:, :Q_TILE] - lam * pv[:, Q_TILE:]
        var = jnp.mean(o * o, axis=0, keepdims=True)
        o = o * lax.rsqrt(var + EPS) * (sg_ref[...] * (1.0 - lambda_init))
        o_ref[...] = o.T.astype(BF16)

    pl.when(qi == 0)(lambda: attend(CTX_LEN))
    pl.when(qi != 0)(lambda: attend(seq_all))


def _attn(p3, tabs, qk_gain, lam, sub_gain, bd, lambda_init):
    batch, seq_all, _ = p3.shape
    nq = seq_all // Q_TILE
    cq, sa, sb = tabs
    qg = jnp.tile(qk_gain[0], 2).reshape(1, LANES)
    kg = jnp.tile(qk_gain[1], 2).reshape(1, LANES)
    qspec = lambda b, h, i: (b, i, h)
    tspec = pl.BlockSpec((Q_TILE, LANES), lambda b, h, i: (i, 0))
    full = _resident((seq_all, LANES))
    kern = functools.partial(_attn_kernel, lambda_init=lambda_init, seq_all=seq_all)
    return pl.pallas_call(
        kern,
        out_shape=jax.ShapeDtypeStruct((batch, seq_all, A_HEADS * LANES), BF16),
        grid=(batch, A_HEADS, nq),
        in_specs=[
            pl.BlockSpec((None, Q_TILE, LANES), qspec),
            pl.BlockSpec((None, seq_all, LANES), lambda b, h, i: (b, 0, A_HEADS + h)),
            pl.BlockSpec((None, seq_all, LANES), lambda b, h, i: (b, 0, 2 * A_HEADS + h)),
            tspec, tspec, tspec, full, full, full,
            _resident((1, LANES)), _resident((1, LANES)), _resident((LANES, 1)),
            _resident((4, A_DQK)), _resident((LANES, LANES)),
        ],
        out_specs=pl.BlockSpec((None, Q_TILE, LANES), qspec),
        scratch_shapes=[pltpu.VMEM((seq_all, LANES), BF16), pltpu.VMEM((LANES, seq_all), BF16)],
        compiler_params=_cparams(("parallel", "parallel", "arbitrary")),
        name="attn",
    )(p3, p3, p3, cq, sa, sb, cq, sa, sb, qg, kg, sub_gain.reshape(LANES, 1), lam, bd)


MATMUL_LEVELS = 2

def _scan_levels():
    return int(math.log2(SCAN_CHUNK))


def _scan_consts(n_heads):
    c = SCAN_CHUNK
    nl = _scan_levels()
    t = np.arange(c)[:, None]
    u = np.arange(c)[None, :]
    blocks_f = [u <= t]
    blocks_b = [u >= t]
    for lvl in range(1, MATMUL_LEVELS + 1):
        h = 1 << (lvl - 1)
        m = (t // (2 * h)) * (2 * h) + h
        second = (t % (2 * h)) >= h
        blocks_f.append(np.where(second, (u >= m) & (u <= t), (u > t) & (u <= m - 1)))
        blocks_b.append(np.where(second, (u >= m) & (u < t), (u >= t) & (u <= m - 1)))
    a_f = np.tile(np.concatenate(blocks_f, axis=0).astype(np.float32), (1, 2))
    a_b = np.tile(np.concatenate(blocks_b, axis=0).astype(np.float32), (1, 2))
    x = t ^ u
    lv = np.where(x > 0, np.floor(np.log2(np.maximum(x, 1))).astype(np.int32) + 1, 0)
    lv_f = np.where(u <= t, lv, -1).astype(np.int32)
    lv_b = np.where(u >= t, lv, -1).astype(np.int32)
    lv_f = np.tile(lv_f, (n_heads, 1))
    lv_b = np.tile(lv_b, (n_heads, 1))
    return (jnp.asarray(a_f, BF16), jnp.asarray(a_b, BF16), jnp.asarray(lv_f), jnp.asarray(lv_b))


def _scan_chunk(q, k, g, vb, st, a_ref, lv_ref, *, n_heads, backward):
    c = SCAN_CHUNK
    nl = _scan_levels()
    dk = LANES // n_heads
    g2 = g * LOG2E
    g_hi = g2.astype(BF16)
    g_lo = (g2 - g_hi.astype(F32)).astype(BF16)
    sums = _dot(a_ref[...], jnp.concatenate([g_hi, g_lo], axis=0))
    b = sums[0:c]
    edge = b[0:1, :] if backward else b[c - 1:c, :]
    e_in = jnp.exp2(b).astype(BF16)
    e_out = jnp.exp2(edge - b).astype(BF16)

    def level_decay(lvl):
        if lvl <= MATMUL_LEVELS:
            return jnp.exp2(sums[lvl * c:(lvl + 1) * c]).astype(BF16)
        h = 1 << (lvl - 1)
        r0 = h if backward else h - 1
        b3 = b.reshape(c // (2 * h), 2 * h, LANES)
        ref = b3[:, r0:r0 + 1, :]
        if h % 8 == 0:
            first, second = b3[:, :h, :], b3[:, h:, :]
            x = jnp.concatenate([first - ref, ref - second] if backward else [ref - first, second - ref], axis=1)
        else:
            x = -jnp.abs(b3 - ref)
        return jnp.exp2(x.reshape(c, LANES)).astype(BF16)

    qb = q.astype(BF16)
    kb = k.astype(BF16)
    lane = lax.broadcasted_iota(jnp.int32, (c, LANES), 1)
    head_masks = [jnp.where((lane >= hh * dk) & (lane < (hh + 1) * dk), 1.0, 0.0).astype(BF16)
                  for hh in range(n_heads)]

    def by_head(xb):
        if n_heads == 1:
            return xb
        return jnp.concatenate([xb * hm for hm in head_masks], axis=0)

    o = _dot_nt(qb * e_in, st.astype(BF16))

    lv = lv_ref[...]
    att = jnp.zeros((n_heads * c, c), F32)
    for lvl in range(nl + 1):
        if lvl == 0:
            qh, kh = qb, kb
        else:
            el = level_decay(lvl)
            qh, kh = qb * el, kb * el
        att = jnp.where(lv == lvl, _dot_nt(by_head(qh), kh), att)
    attb = att.astype(BF16)
    intra = [_dot(attb[hh * c:(hh + 1) * c], vb[:, hh * LANES:(hh + 1) * LANES]) for hh in range(n_heads)]
    o = o + (intra[0] if n_heads == 1 else jnp.concatenate(intra, axis=1))

    dec = jnp.exp2(edge)
    upd = _dot_tn(vb, kb * e_out)
    if n_heads > 1:
        r = lax.broadcasted_iota(jnp.int32, upd.shape, 0) // LANES
        cc = lax.broadcasted_iota(jnp.int32, upd.shape, 1) // dk
        upd = jnp.where(r == cc, upd, 0.0)
    return o, dec * st + upd


def _scan_loop(load_fwd, load_bwd, of_ref, ob_ref, af_ref, ab_ref, lvf_ref, lvb_ref, *, n_heads, seq_all):
    c = SCAN_CHUNK
    n = seq_all // c
    n_ctx = CTX_LEN // c
    st0 = jnp.zeros((n_heads * LANES, LANES), F32)

    def body(j, carry):
        st_f, st_b = carry
        rf = pl.ds(pl.multiple_of(j * c, c), c)
        jb = jnp.where(j < n_ctx, n_ctx - 1 - j, n - 1 - (j - n_ctx))
        rb = pl.ds(pl.multiple_of(jb * c, c), c)
        q, k, g, vb = load_fwd(rf)
        of_ref[rf, :], st_f = _scan_chunk(q, k, g, vb, st_f, af_ref, lvf_ref, n_heads=n_heads, backward=False)
        q, k, g, vb = load_bwd(rb)
        ob_ref[rb, :], st_b = _scan_chunk(q, k, g, vb, st_b, ab_ref, lvb_ref, n_heads=n_heads, backward=True)
        return st_f, st_b

    lax.fori_loop(0, n, body, (st0, st0), unroll=SCAN_UNROLL)


def _log_sigmoid(x):
    return jnp.minimum(x, 0.0) - jnp.log(1.0 + jnp.exp(-jnp.abs(x)))


def _gla_kernel(q_ref, k_ref, v_ref, g_ref, lr_ref, wgf_ref, wgb_ref, bf_ref, bb_ref, gain_ref,
                af_ref, ab_ref, lvf_ref, lvb_ref, o_ref, of_ref, ob_ref, *, seq_all):
    def loader(wg_ref, b_ref):
        def load(rows):
            q = q_ref[rows, :] * (B_DK ** -0.5)
            k = k_ref[rows, :]
            pre = _dot(lr_ref[rows, :].astype(BF16), wg_ref[...]) + b_ref[...]
            g = _log_sigmoid(pre) * (1.0 / B_GATE_NORM)
            return q, k, g, v_ref[rows, :].astype(BF16)
        return load

    _scan_loop(loader(wgf_ref, bf_ref), loader(wgb_ref, bb_ref), of_ref, ob_ref, af_ref, ab_ref, lvf_ref, lvb_ref,
               n_heads=2, seq_all=seq_all)

    def post(j, carry):
        rows = pl.ds(pl.multiple_of(j * ROW_TILE, ROW_TILE), ROW_TILE)
        for hh in range(2):
            cols = slice(hh * LANES, (hh + 1) * LANES)
            o = of_ref[rows, cols] + ob_ref[rows, cols]
            var = jnp.mean(o * o, axis=-1, keepdims=True)
            o_ref[rows, cols] = (o * lax.rsqrt(var + EPS) * gain_ref[...] * _silu(g_ref[rows, cols])).astype(BF16)
        return carry
    lax.fori_loop(0, seq_all // ROW_TILE, post, 0)


def _gla(p3, w_gate_up, b_gate_up, gain, consts):
    batch, seq_all, _ = p3.shape
    pairs = B_HEADS // 2
    wg = jnp.zeros((2, pairs, LANES, LANES), F32)
    for d in range(2):
        w = w_gate_up[d].reshape(B_GATE_RANK, pairs, LANES).transpose(1, 0, 2)
        wg = wg.at[d, :, d * B_GATE_RANK:(d + 1) * B_GATE_RANK, :].set(w)
    wg = wg.astype(BF16)
    bias = b_gate_up.reshape(2, pairs, 1, LANES)
    a_f, a_b, lv_f, lv_b = consts
    col = lambda off: (lambda b, p: (b, 0, off + p))
    q_off = 3 * A_HEADS * LANES // LANES
    k_off = q_off + B_HEADS * B_DK // LANES
    v_off = (k_off * LANES + B_HEADS * B_DK) // (2 * LANES)
    g_off = v_off + pairs
    lr_off = (g_off + pairs) * 2
    wspec = pl.BlockSpec((None, LANES, LANES), lambda b, p: (p, 0, 0))
    bspec = pl.BlockSpec((None, 1, LANES), lambda b, p: (p, 0, 0))
    return pl.pallas_call(
        functools.partial(_gla_kernel, seq_all=seq_all),
        out_shape=jax.ShapeDtypeStruct((batch, seq_all, B_HEADS * LANES), BF16),
        grid=(batch, pairs),
        in_specs=[
            pl.BlockSpec((None, seq_all, LANES), col(q_off)),
            pl.BlockSpec((None, seq_all, LANES), col(k_off)),
            pl.BlockSpec((None, seq_all, 2 * LANES), col(v_off)),
            pl.BlockSpec((None, seq_all, 2 * LANES), col(g_off)),
            pl.BlockSpec((None, seq_all, LANES), lambda b, p: (b, 0, lr_off)),
            wspec, wspec, bspec, bspec,
            _resident((1, LANES)),
            _resident(a_f.shape), _resident(a_b.shape), _resident(lv_f.shape), _resident(lv_b.shape),
        ],
        out_specs=pl.BlockSpec((None, seq_all, 2 * LANES), lambda b, p: (b, 0, p)),
        scratch_shapes=[pltpu.VMEM((seq_all, 2 * LANES), F32), pltpu.VMEM((seq_all, 2 * LANES), F32)],
        compiler_params=_cparams(("parallel", "parallel")),
        name="gla",
    )(p3, p3, p3, p3, p3, wg[0], wg[1], bias[0], bias[1], gain.reshape(1, LANES), a_f, a_b, lv_f, lv_b)


def _hgrn_kernel(q_ref, ff_ref, fb_ref, i_ref, g_ref, lb_ref, gain_ref,
                 af_ref, ab_ref, lvf_ref, lvb_ref, o_ref, of_ref, ob_ref, *, seq_all, layer):
    def lower_bound(d):
        raw = lb_ref[d]
        ex = jnp.exp(raw - jnp.max(raw, axis=0, keepdims=True))
        p = ex / jnp.sum(ex, axis=0, keepdims=True)
        return jnp.sum(p[1:layer + 1], axis=0, keepdims=True)

    def loader(f_ref, lb):
        def load(rows):
            q = _silu(q_ref[rows, :]) * (C_DH ** -0.5)
            f = lb + (1.0 - lb) * jax.nn.sigmoid(f_ref[rows, :])
            return q, 1.0 - f, jnp.log(f), i_ref[rows, :].astype(BF16)
        return load

    _scan_loop(loader(ff_ref, lower_bound(0)), loader(fb_ref, lower_bound(1)), of_ref, ob_ref,
               af_ref, ab_ref, lvf_ref, lvb_ref, n_heads=1, seq_all=seq_all)

    def post(j, carry):
        rows = pl.ds(pl.multiple_of(j * ROW_TILE, ROW_TILE), ROW_TILE)
        o = of_ref[rows, :] + ob_ref[rows, :]
        var = jnp.mean(o * o, axis=-1, keepdims=True)
        o_ref[rows, :] = (o * lax.rsqrt(var + EPS) * gain_ref[...] * _silu(g_ref[rows, :])).astype(BF16)
        return carry
    lax.fori_loop(0, seq_all // ROW_TILE, post, 0)


def _hgrn(p3, lb_raw, gain, consts, layer):
    batch, seq_all, _ = p3.shape
    a_f, a_b, lv_f, lv_b = consts
    col = lambda off: (lambda b, h: (b, 0, off + h))
    return pl.pallas_call(
        functools.partial(_hgrn_kernel, seq_all=seq_all, layer=layer),
        out_shape=jax.ShapeDtypeStruct((batch, seq_all, C_HEADS * LANES), BF16),
        grid=(batch, C_HEADS),
        in_specs=[pl.BlockSpec((None, seq_all, LANES), col(s * C_HEADS)) for s in range(5)] + [
            pl.BlockSpec((2, DEPTH, LANES), lambda b, h: (0, 0, h)),
            _resident((1, LANES)),
            _resident(a_f.shape), _resident(a_b.shape), _resident(lv_f.shape), _resident(lv_b.shape),
        ],
        out_specs=pl.BlockSpec((None, seq_all, LANES), lambda b, h: (b, 0, h)),
        scratch_shapes=[pltpu.VMEM((seq_all, LANES), F32), pltpu.VMEM((seq_all, LANES), F32)],
        compiler_params=_cparams(("parallel", "parallel")),
        name="hgrn",
    )(p3, p3, p3, p3, p3, lb_raw, gain.reshape(1, LANES), a_f, a_b, lv_f, lv_b)


def _rope_tables(seq):
    pos = jnp.arange(seq)
    row_ids = (pos // GRID_W).astype(F32)
    col_ids = (pos % GRID_W).astype(F32)
    n_axis = A_DQK // 2
    inv = ROPE_BASE ** (-jnp.arange(0, n_axis, 2, dtype=F32) / n_axis)
    ang_r = row_ids[:, None] * inv
    ang_c = col_ids[:, None] * inv
    zeros = jnp.zeros_like(ang_r)
    cos64 = jnp.concatenate([jnp.cos(ang_r)] * 2 + [jnp.cos(ang_c)] * 2, axis=-1)
    sa64 = jnp.concatenate([-jnp.sin(ang_r), zeros, -jnp.sin(ang_c), zeros], axis=-1)
    sb64 = jnp.concatenate([zeros, jnp.sin(ang_r), zeros, jnp.sin(ang_c)], axis=-1)
    ident = lambda v: jnp.full((CTX_LEN, LANES), v, F32)
    full = lambda t, v: jnp.concatenate([ident(v), jnp.tile(t, (1, 2))], axis=0)
    return full(cos64, 1.0), full(sa64, 0.0), full(sb64, 0.0)


def kernel(x, c, ctx, c_ctx, w_ada, b_ada, norm1_gain, norm2_gain, w_in_even, qk_gain_a, lambda_a, subln_gain_a, w_gate_up_b, b_gate_up_b, onorm_gain_b, w_out_even, w_in_odd, lb_raw_c, onorm_gain_c, w_out_odd, w_ffn_in, w_ffn_out):
    batch, seq, d = x.shape
    seq_all = CTX_LEN + seq
    assert d == D_MODEL and ctx.shape[1] == CTX_LEN == ROW_TILE == Q_TILE and seq % ROW_TILE == 0

    xz = jnp.concatenate([ctx, x], axis=1).reshape(batch * seq_all, d)
    mod_rows = -(-(batch + 1) // 8) * 8
    cc = jnp.zeros((mod_rows, d), F32).at[:batch].set(c).at[batch].set(c_ctx)
    mods = _ada(cc, w_ada, b_ada).reshape(DEPTH, mod_rows, 6, d)

    tabs = _rope_tables(seq)
    half = np.arange(LANES) // A_DQK
    bd = jnp.asarray((half[:, None] == half[None, :]) / A_DQK, BF16)
    consts_gla = _scan_consts(2)
    consts_hgrn = _scan_consts(1)

    for l in range(DEPTH):
        j = l // 2
        last = l == DEPTH - 1
        wi = w_ffn_in[l].astype(BF16)
        wf = w_ffn_out[l].astype(BF16)
        if l % 2 == 0:
            w = jnp.pad(w_in_even[j], ((0, 0), (0, EVEN_COLS_PAD - EVEN_COLS))).astype(BF16)
            p3 = _inproj(xz, mods[l], norm1_gain[l], w, batch).reshape(batch, seq_all, EVEN_COLS_PAD)
            lambda_init = 0.8 - 0.6 * math.exp(-0.3 * l)
            oa = _attn(p3, tabs, qk_gain_a[j], lambda_a[j], subln_gain_a[j], bd, lambda_init)
            ob = _gla(p3, w_gate_up_b[j], b_gate_up_b[j], onorm_gain_b[j], consts_gla)
            a_width = A_HEADS * LANES
            wo = w_out_even[j].astype(BF16)
            o_parts = [oa.reshape(batch * seq_all, -1), ob.reshape(batch * seq_all, -1)]
            wo_parts = [wo[:a_width], wo[a_width:]]
        else:
            p3 = _inproj(xz, mods[l], norm1_gain[l], w_in_odd[j].astype(BF16), batch).reshape(batch, seq_all, ODD_COLS)
            oc = _hgrn(p3, lb_raw_c, onorm_gain_c[j], consts_hgrn, l)
            o_parts = [oc.reshape(batch * seq_all, -1)]
            wo_parts = [w_out_odd[j].astype(BF16)]
        xz = _post(o_parts, xz, mods[l], norm2_gain[l], wo_parts, wi, wf, batch, last)
    return xz.reshape(batch, seq, d)
```

```python
import functools
import math

import numpy as np
import jax
import jax.numpy as jnp
from jax import lax
from jax.experimental import pallas as pl
from jax.experimental.pallas import tpu as pltpu

F32 = jnp.float32
BF16 = jnp.bfloat16

D_MODEL = 1024
DEPTH = 4
CTX_LEN = 256
GRID_W = 64
A_HEADS = 4
A_DQK = 64
B_HEADS = 4
B_DK = 64
B_GATE_RANK = 16
B_GATE_NORM = 16.0
C_HEADS = 8
C_DH = 128
ROPE_BASE = 10000.0
EPS = 1e-6
D_FF = 2816
EVEN_COLS = 3104
EVEN_COLS_PAD = 3200
ODD_COLS = 5120

LANES = 128
ROW_TILE = 256
COL_CHUNK = 512
FF_CHUNK = 1408
Q_TILE = 256
LOG2E = math.log2(math.e)
SCAN_CHUNK = 128
SCAN_UNROLL = 6
STAGE_UNROLL = 6
VMEM_LIMIT = 52 * 1024 * 1024


def _cparams(sem):
    return pltpu.CompilerParams(dimension_semantics=sem, vmem_limit_bytes=VMEM_LIMIT)


def _resident(shape):
    nd = len(shape)
    return pl.BlockSpec(shape, lambda *_: (0,) * nd, pipeline_mode=pl.Buffered(1))


def _silu(x):
    return x * jax.nn.sigmoid(x)


def _dot(a, b):
    return jnp.dot(a, b, preferred_element_type=F32)


def _dot_nt(a, b):
    return lax.dot_general(a, b, (((1,), (1,)), ((), ())), preferred_element_type=F32)


def _dot_tn(a, b):
    return lax.dot_general(a, b, (((0,), (0,)), ((), ())), preferred_element_type=F32)


def _ada_kernel(c_ref, w_ref, b_ref, o_ref):
    s = _silu(c_ref[...]).astype(BF16)
    o_ref[...] = _dot(s, w_ref[...].astype(BF16)) + b_ref[...]


def _ada(cc, w_ada, b_ada):
    depth, d, n = w_ada.shape
    rows = cc.shape[0]
    tn = 1536
    return pl.pallas_call(
        _ada_kernel,
        out_shape=jax.ShapeDtypeStruct((depth, rows, n), F32),
        grid=(depth, n // tn),
        in_specs=[
            pl.BlockSpec((rows, d), lambda l, j: (0, 0)),
            pl.BlockSpec((None, d, tn), lambda l, j: (l, 0, j)),
            pl.BlockSpec((None, 1, tn), lambda l, j: (l, 0, j)),
        ],
        out_specs=pl.BlockSpec((None, rows, tn), lambda l, j: (l, 0, j)),
        compiler_params=_cparams(("parallel", "parallel")),
        name="ada",
    )(cc, w_ada, b_ada.reshape(depth, 1, n))


def _norm_mod(x, gain, shift, scale):
    var = jnp.mean(x * x, axis=-1, keepdims=True)
    return (x * lax.rsqrt(var + EPS) * gain) * (1.0 + scale) + shift


def _inproj_kernel(x_ref, mod_ref, gain_ref, w_ref, o_ref):
    h = _norm_mod(x_ref[...], gain_ref[...], mod_ref[0:1, :], mod_ref[1:2, :]).astype(BF16)
    ncols = w_ref.shape[1]
    for c0 in range(0, ncols, COL_CHUNK):
        c1 = min(c0 + COL_CHUNK, ncols)
        o_ref[:, c0:c1] = _dot(h, w_ref[:, c0:c1])


def _tiles_per_sample(seq_all):
    return seq_all // ROW_TILE


def _mod_row(i, tiles, batch):
    return jnp.where(i % tiles == 0, batch, i // tiles)


def _inproj(xz, mod, gain, w, batch):
    rows, d = xz.shape
    ncols = w.shape[1]
    tiles = _tiles_per_sample(rows // batch)
    return pl.pallas_call(
        _inproj_kernel,
        out_shape=jax.ShapeDtypeStruct((rows, ncols), F32),
        grid=(rows // ROW_TILE,),
        in_specs=[
            pl.BlockSpec((ROW_TILE, d), lambda i: (i, 0)),
            pl.BlockSpec((None, 6, d), lambda i: (_mod_row(i, tiles, batch), 0, 0)),
            _resident((1, d)),
            _resident((d, ncols)),
        ],
        out_specs=pl.BlockSpec((ROW_TILE, ncols), lambda i: (i, 0)),
        compiler_params=_cparams(("parallel",)),
        name="inproj",
    )(xz, mod, gain.reshape(1, d), w)


def _post_kernel(*refs, n_parts):
    o_refs = refs[:n_parts]
    x_ref, mod_ref, gain_ref = refs[n_parts:n_parts + 3]
    wo_refs = refs[n_parts + 3:2 * n_parts + 3]
    wi_ref, wf_ref, out_ref = refs[2 * n_parts + 3:]
    y = _dot(o_refs[0][...], wo_refs[0][...])
    for p in range(1, n_parts):
        y = y + _dot(o_refs[p][...], wo_refs[p][...])
    x1 = x_ref[...] + mod_ref[2:3, :] * y
    h = _norm_mod(x1, gain_ref[...], mod_ref[3:4, :], mod_ref[4:5, :]).astype(BF16)
    acc = None
    for c0 in range(0, D_FF, FF_CHUNK):
        gate = _dot(h, wi_ref[:, c0:c0 + FF_CHUNK])
        up = _dot(h, wi_ref[:, D_FF + c0:D_FF + c0 + FF_CHUNK])
        a = (_silu(gate) * up).astype(BF16)
        part = _dot(a, wf_ref[c0:c0 + FF_CHUNK, :])
        acc = part if acc is None else acc + part
    out_ref[...] = x1 + mod_ref[5:6, :] * acc


def _post(o_parts, xz, mod, gain, wo_parts, wi, wf, batch, latent_only):
    rows, d = xz.shape
    tiles = _tiles_per_sample(rows // batch)
    if latent_only:
        out_tiles = tiles - 1
        src = lambda i: (i // out_tiles) * tiles + 1 + i % out_tiles
        mrow = lambda i: i // out_tiles
    else:
        out_tiles = tiles
        src = lambda i: i
        mrow = lambda i: _mod_row(i, tiles, batch)
    n_parts = len(o_parts)
    in_specs = [pl.BlockSpec((ROW_TILE, o.shape[1]), lambda i: (src(i), 0)) for o in o_parts]
    in_specs += [
        pl.BlockSpec((ROW_TILE, d), lambda i: (src(i), 0)),
        pl.BlockSpec((None, 6, d), lambda i: (mrow(i), 0, 0)),
        _resident((1, d)),
    ]
    in_specs += [_resident(w.shape) for w in wo_parts]
    in_specs += [_resident(wi.shape), _resident(wf.shape)]
    return pl.pallas_call(
        functools.partial(_post_kernel, n_parts=n_parts),
        out_shape=jax.ShapeDtypeStruct((batch * out_tiles * ROW_TILE, d), F32),
        grid=(batch * out_tiles,),
        in_specs=in_specs,
        out_specs=pl.BlockSpec((ROW_TILE, d), lambda i: (i, 0)),
        compiler_params=_cparams(("parallel",)),
        name="post",
    )(*o_parts, xz, mod, gain.reshape(1, d), *wo_parts, wi, wf)


def _attn_kernel(q_ref, k_ref, v_ref, cq_ref, saq_ref, sbq_ref, ck_ref, sak_ref, sbk_ref,
                 qg_ref, kg_ref, sg_ref, lam_ref, bd_ref, o_ref, kb_ref, vt_ref,
                 *, lambda_init, seq_all):
    qi = pl.program_id(2)

    def norm_rope(x, g, c, sa, sb):
        ms = _dot((x * x).astype(BF16), bd_ref[...])
        y = x * lax.rsqrt(ms + EPS) * g
        return y * c + pltpu.roll(y, LANES - 16, 1) * sa + pltpu.roll(y, 16, 1) * sb

    @pl.when(qi == 0)
    def _prep_keys():
        def body(r, carry):
            rows = pl.ds(pl.multiple_of(r * ROW_TILE, ROW_TILE), ROW_TILE)
            kb_ref[rows, :] = norm_rope(k_ref[rows, :], kg_ref[...], ck_ref[rows, :],
                                        sak_ref[rows, :], sbk_ref[rows, :]).astype(BF16)
            vt_ref[:, rows] = v_ref[rows, :].T.astype(BF16)
            return carry
        lax.fori_loop(0, seq_all // ROW_TILE, body, 0)

    qn = norm_rope(q_ref[...], qg_ref[...], cq_ref[...], saq_ref[...], sbq_ref[...]) * (A_DQK ** -0.5 * LOG2E)
    lane = lax.broadcasted_iota(jnp.int32, qn.shape, 1)
    q2 = jnp.concatenate([jnp.where(lane < A_DQK, qn, 0.0).astype(BF16),
                          jnp.where(lane >= A_DQK, qn, 0.0).astype(BF16)], axis=0)

    lam_v = lam_ref[...]
    e1 = jnp.exp(jnp.sum(lam_v[0:1, :] * lam_v[1:2, :], axis=-1, keepdims=True))
    e2 = jnp.exp(jnp.sum(lam_v[2:3, :] * lam_v[3:4, :], axis=-1, keepdims=True))
    lam = e1 - e2 + lambda_init

    def attend(n_keys):
        s = _dot_nt(kb_ref[0:n_keys, :], q2)
        m = jnp.max(s, axis=0, keepdims=True)
        p = jnp.exp2(s - m)
        l = jnp.sum(p, axis=0, keepdims=True)
        pv = _dot(vt_ref[:, 0:n_keys], p.astype(BF16)) / l
        o = pv[:, :Q_TILE] - lam * pv[:, Q_TILE:]
        var = jnp.mean(o * o, axis=0, keepdims=True)
        o = o * lax.rsqrt(var + EPS) * (sg_ref[...] * (1.0 - lambda_init))
        o_ref[...] = o.T.astype(BF16)

    pl.when(qi == 0)(lambda: attend(CTX_LEN))
    pl.when(qi != 0)(lambda: attend(seq_all))


def _attn(p3, tabs, qk_gain, lam, sub_gain, bd, lambda_init):
    batch, seq_all, _ = p3.shape
    nq = seq_all // Q_TILE
    cq, sa, sb = tabs
    qg = jnp.tile(qk_gain[0], 2).reshape(1, LANES)
    kg = jnp.tile(qk_gain[1], 2).reshape(1, LANES)
    qspec = lambda b, h, i: (b, i, h)
    tspec = pl.BlockSpec((Q_TILE, LANES), lambda b, h, i: (i, 0))
    full = _resident((seq_all, LANES))
    kern = functools.partial(_attn_kernel, lambda_init=lambda_init, seq_all=seq_all)
    return pl.pallas_call(
        kern,
        out_shape=jax.ShapeDtypeStruct((batch, seq_all, A_HEADS * LANES), BF16),
        grid=(batch, A_HEADS, nq),
        in_specs=[
            pl.BlockSpec((None, Q_TILE, LANES), qspec),
            pl.BlockSpec((None, seq_all, LANES), lambda b, h, i: (b, 0, A_HEADS + h)),
            pl.BlockSpec((None, seq_all, LANES), lambda b, h, i: (b, 0, 2 * A_HEADS + h)),
            tspec, tspec, tspec, full, full, full,
            _resident((1, LANES)), _resident((1, LANES)), _resident((LANES, 1)),
            _resident((4, A_DQK)), _resident((LANES, LANES)),
        ],
        out_specs=pl.BlockSpec((None, Q_TILE, LANES), qspec),
        scratch_shapes=[pltpu.VMEM((seq_all, LANES), BF16), pltpu.VMEM((LANES, seq_all), BF16)],
        compiler_params=_cparams(("parallel", "parallel", "arbitrary")),
        name="attn",
    )(p3, p3, p3, cq, sa, sb, cq, sa, sb, qg, kg, sub_gain.reshape(LANES, 1), lam, bd)


MATMUL_LEVELS = 2
LOCAL_BLOCK = 128
SAFE_LOG2 = 100.0


def _scan_levels():
    return int(math.log2(SCAN_CHUNK))


def _scan_consts(n_heads):
    c = SCAN_CHUNK
    t = np.arange(c)[:, None]
    u = np.arange(c)[None, :]
    blocks_f = [u <= t]
    blocks_b = [u >= t]
    for lvl in range(1, MATMUL_LEVELS + 1):
        h = 1 << (lvl - 1)
        m = (t // (2 * h)) * (2 * h) + h
        second = (t % (2 * h)) >= h
        blocks_f.append(np.where(second, (u >= m) & (u <= t), (u > t) & (u <= m - 1)))
        blocks_b.append(np.where(second, (u >= m) & (u < t), (u >= t) & (u <= m - 1)))
    a_f = np.tile(np.concatenate(blocks_f, axis=0).astype(np.float32), (1, 2))
    a_b = np.tile(np.concatenate(blocks_b, axis=0).astype(np.float32), (1, 2))
    x = t ^ u
    lv = np.where(x > 0, np.floor(np.log2(np.maximum(x, 1))).astype(np.int32) + 1, 0)
    lv_f = np.where(u <= t, lv, -1).astype(np.int32)
    lv_b = np.where(u >= t, lv, -1).astype(np.int32)
    lv_f = np.tile(lv_f, (n_heads, 1))
    lv_b = np.tile(lv_b, (n_heads, 1))
    return (jnp.asarray(a_f, BF16), jnp.asarray(a_b, BF16), jnp.asarray(lv_f), jnp.asarray(lv_b))


def _scan_sums(g, a):
    g2 = g * LOG2E
    g_hi = g2.astype(BF16)
    g_lo = (g2 - g_hi.astype(F32)).astype(BF16)
    return _dot(a, jnp.concatenate([g_hi, g_lo], axis=0))


def _local_exponent(b, backward):
    c = SCAN_CHUNK
    b3 = b.reshape(c // LOCAL_BLOCK, LOCAL_BLOCK, LANES)
    r0 = LOCAL_BLOCK // 2 if backward else LOCAL_BLOCK // 2 - 1
    return (b3 - b3[:, r0:r0 + 1, :]).reshape(c, LANES)


def _by_head(xb, n_heads):
    if n_heads == 1:
        return xb
    dk = LANES // n_heads
    lane = lax.broadcasted_iota(jnp.int32, xb.shape, 1)
    masks = [jnp.where((lane >= hh * dk) & (lane < (hh + 1) * dk), 1.0, 0.0).astype(BF16) for hh in range(n_heads)]
    return jnp.concatenate([xb * hm for hm in masks], axis=0)


def _head_block_mask(shape, row_block, col_block):
    r = lax.broadcasted_iota(jnp.int32, shape, 0) // row_block
    cc = lax.broadcasted_iota(jnp.int32, shape, 1) // col_block
    return r == cc


def _level_chunk(qb, kb, b, fine, vb, st, lv_ref, *, n_heads, backward):
    c = SCAN_CHUNK
    nl = _scan_levels()
    edge = b[0:1, :] if backward else b[c - 1:c, :]
    e_in = jnp.exp2(b).astype(BF16)
    e_out = jnp.exp2(edge - b).astype(BF16)

    def level_decay(lvl):
        if lvl <= MATMUL_LEVELS:
            return jnp.exp2(fine[(lvl - 1) * c:lvl * c]).astype(BF16)
        h = 1 << (lvl - 1)
        r0 = h if backward else h - 1
        b3 = b.reshape(c // (2 * h), 2 * h, LANES)
        ref = b3[:, r0:r0 + 1, :]
        if h % 8 == 0:
            first, second = b3[:, :h, :], b3[:, h:, :]
            x = jnp.concatenate([first - ref, ref - second] if backward else [ref - first, second - ref], axis=1)
        else:
            x = -jnp.abs(b3 - ref)
        return jnp.exp2(x.reshape(c, LANES)).astype(BF16)

    o = _dot_nt(qb * e_in, st.astype(BF16))

    lv = lv_ref[...]
    att = jnp.zeros((n_heads * c, c), F32)
    for lvl in range(nl + 1):
        if lvl == 0:
            qh, kh = qb, kb
        else:
            el = level_decay(lvl)
            qh, kh = qb * el, kb * el
        att = jnp.where(lv == lvl, _dot_nt(_by_head(qh, n_heads), kh), att)
    attb = att.astype(BF16)
    intra = [_dot(attb[hh * c:(hh + 1) * c], vb[:, hh * LANES:(hh + 1) * LANES]) for hh in range(n_heads)]
    o = o + (intra[0] if n_heads == 1 else jnp.concatenate(intra, axis=1))

    dec = jnp.exp2(edge)
    upd = _dot_tn(vb, kb * e_out)
    if n_heads > 1:
        upd = jnp.where(_head_block_mask(upd.shape, LANES, LANES // n_heads), upd, 0.0)
    return o, dec * st + upd


def _shortcut_pairs(qp, kp, kd, vb, lv_ref, *, n_heads):
    att = jnp.where(lv_ref[...] >= 0, _dot_nt(_by_head(qp, n_heads), kp), 0.0).astype(BF16)
    upd = _dot_tn(kd, vb)
    if n_heads > 1:
        upd = jnp.where(_head_block_mask(upd.shape, LANES // n_heads, LANES), upd, 0.0)
    return att, upd


def _shortcut_output(att, upd, qe, vb, dec, s, *, n_heads):
    c = SCAN_CHUNK
    sb = s.astype(BF16)
    outs = []
    for hh in range(n_heads):
        cols = slice(hh * LANES, (hh + 1) * LANES)
        lhs = jnp.concatenate([att[hh * c:(hh + 1) * c], qe], axis=1)
        rhs = jnp.concatenate([vb[:, cols], sb[:, cols]], axis=0)
        outs.append(_dot(lhs, rhs))
    o = outs[0] if n_heads == 1 else jnp.concatenate(outs, axis=1)
    if n_heads > 1:
        dec = jnp.concatenate([dec] * n_heads, axis=1)
    return o, dec * s + upd


def _scan_loop(load_common, load_dir, out_refs, a_refs, lv_refs, v_s, stage_refs, *, n_heads, seq_all):
    c = SCAN_CHUNK
    n = seq_all // c
    n_ctx = CTX_LEN // c

    def chunk_rows(j):
        return pl.ds(pl.multiple_of(j * c, c), c)

    def scan(prepare, finish, state0, unroll):
        def body(i, sts):
            sts = list(sts)
            todo = []
            for u in range(unroll):
                j = i * unroll + u
                jb = jnp.where(j < n_ctx, n_ctx - 1 - j, n - 1 - (j - n_ctx))
                for d, jj in ((0, j), (1, jb)):
                    rows = chunk_rows(jj)
                    todo.append((d, rows, prepare(d, rows)))
            for d, rows, values in todo:
                out_refs[d][rows, :], sts[d] = finish(d, rows, values, sts[d])
            return tuple(sts)
        lax.fori_loop(0, n // unroll, body, (state0, state0))

    def stage(j, worst):
        gated = []
        for u in range(STAGE_UNROLL):
            rows = chunk_rows(j * STAGE_UNROLL + u)
            q, vb = load_common(rows)
            v_s[rows, :] = vb
            gated.append((rows, q, [load_dir[d](rows) for d in (0, 1)]))
        summed = [(rows, q, [(k, _scan_sums(g, a_refs[d][0:c, :])) for d, (k, g) in enumerate(kg)])
                  for rows, q, kg in gated]
        for rows, q, kb in summed:
            for d, (k, b) in enumerate(kb):
                x = _local_exponent(b, bool(d))
                edge = b[0:1, :] if d else b[c - 1:c, :]
                qp_s, kp_s, qe_s, kd_s, dec_s = stage_refs[d]
                qp_s[rows, :] = (q * jnp.exp2(x)).astype(BF16)
                kp_s[rows, :] = (k * jnp.exp2(-x)).astype(BF16)
                qe_s[rows, :] = (q * jnp.exp2(b)).astype(BF16)
                kd_s[rows, :] = (k * jnp.exp2(edge - b)).astype(BF16)
                dec_s[rows, :] = jnp.broadcast_to(jnp.exp2(edge), (LANES, LANES)).T
                worst = jnp.maximum(worst, jnp.max(jnp.abs(x).reshape(c // 8, 8, LANES), axis=0))
        return worst
    worst = lax.fori_loop(0, n // STAGE_UNROLL, stage, jnp.zeros((8, LANES), F32))
    safe = jnp.max(worst) < SAFE_LOG2

    def shortcut_pairs(d, rows):
        qp_s, kp_s, _, kd_s, _ = stage_refs[d]
        return _shortcut_pairs(qp_s[rows, :], kp_s[rows, :], kd_s[rows, :], v_s[rows, :], lv_refs[d], n_heads=n_heads)

    def shortcut_output(d, rows, values, s):
        _, _, qe_s, _, dec_s = stage_refs[d]
        return _shortcut_output(*values, qe_s[rows, :], v_s[rows, :], dec_s[rows, :], s, n_heads=n_heads)

    def levels(d, rows, values, st):
        q, vb = load_common(rows)
        k, g = load_dir[d](rows)
        sums = _scan_sums(g, a_refs[d][...])
        return _level_chunk(q.astype(BF16), k.astype(BF16), sums[0:c], sums[c:], vb, st, lv_refs[d],
                            n_heads=n_heads, backward=bool(d))

    pl.when(safe)(lambda: scan(shortcut_pairs, shortcut_output, jnp.zeros((LANES, n_heads * LANES), F32), SCAN_UNROLL))
    pl.when(jnp.logical_not(safe))(lambda: scan(lambda d, rows: None, levels, jnp.zeros((n_heads * LANES, LANES), F32), 1))


def _scan_scratch(seq_all, n_heads):
    assert SCAN_CHUNK == LANES == LOCAL_BLOCK
    width = n_heads * LANES
    out = [pltpu.VMEM((seq_all, width), F32)] * 2
    stage = [pltpu.VMEM((seq_all, LANES), BF16)] * 4 + [pltpu.VMEM((seq_all, LANES), F32)]
    return out + [pltpu.VMEM((seq_all, width), BF16)] + stage * 2


def _log_sigmoid(x):
    return jnp.minimum(x, 0.0) - jnp.log(1.0 + jnp.exp(-jnp.abs(x)))


def _gla_kernel(q_ref, k_ref, v_ref, g_ref, lr_ref, wgf_ref, wgb_ref, bf_ref, bb_ref, gain_ref,
                af_ref, ab_ref, lvf_ref, lvb_ref, o_ref, of_ref, ob_ref, v_s, *stage_refs, seq_all):
    def load_common(rows):
        return q_ref[rows, :] * (B_DK ** -0.5), v_ref[rows, :].astype(BF16)

    def load_dir(wg_ref, b_ref):
        def load(rows):
            pre = _dot(lr_ref[rows, :].astype(BF16), wg_ref[...]) + b_ref[...]
            return k_ref[rows, :], _log_sigmoid(pre) * (1.0 / B_GATE_NORM)
        return load

    _scan_loop(load_common, (load_dir(wgf_ref, bf_ref), load_dir(wgb_ref, bb_ref)), (of_ref, ob_ref),
               (af_ref, ab_ref), (lvf_ref, lvb_ref), v_s, (stage_refs[:5], stage_refs[5:]), n_heads=2, seq_all=seq_all)

    def post(j, carry):
        rows = pl.ds(pl.multiple_of(j * ROW_TILE, ROW_TILE), ROW_TILE)
        for hh in range(2):
            cols = slice(hh * LANES, (hh + 1) * LANES)
            o = of_ref[rows, cols] + ob_ref[rows, cols]
            var = jnp.mean(o * o, axis=-1, keepdims=True)
            o_ref[rows, cols] = (o * lax.rsqrt(var + EPS) * gain_ref[...] * _silu(g_ref[rows, cols])).astype(BF16)
        return carry
    lax.fori_loop(0, seq_all // ROW_TILE, post, 0)


def _gla(p3, w_gate_up, b_gate_up, gain, consts):
    batch, seq_all, _ = p3.shape
    pairs = B_HEADS // 2
    wg = jnp.zeros((2, pairs, LANES, LANES), F32)
    for d in range(2):
        w = w_gate_up[d].reshape(B_GATE_RANK, pairs, LANES).transpose(1, 0, 2)
        wg = wg.at[d, :, d * B_GATE_RANK:(d + 1) * B_GATE_RANK, :].set(w)
    wg = wg.astype(BF16)
    bias = b_gate_up.reshape(2, pairs, 1, LANES)
    a_f, a_b, lv_f, lv_b = consts
    col = lambda off: (lambda b, p: (b, 0, off + p))
    q_off = 3 * A_HEADS * LANES // LANES
    k_off = q_off + B_HEADS * B_DK // LANES
    v_off = (k_off * LANES + B_HEADS * B_DK) // (2 * LANES)
    g_off = v_off + pairs
    lr_off = (g_off + pairs) * 2
    wspec = pl.BlockSpec((None, LANES, LANES), lambda b, p: (p, 0, 0))
    bspec = pl.BlockSpec((None, 1, LANES), lambda b, p: (p, 0, 0))
    return pl.pallas_call(
        functools.partial(_gla_kernel, seq_all=seq_all),
        out_shape=jax.ShapeDtypeStruct((batch, seq_all, B_HEADS * LANES), BF16),
        grid=(batch, pairs),
        in_specs=[
            pl.BlockSpec((None, seq_all, LANES), col(q_off)),
            pl.BlockSpec((None, seq_all, LANES), col(k_off)),
            pl.BlockSpec((None, seq_all, 2 * LANES), col(v_off)),
            pl.BlockSpec((None, seq_all, 2 * LANES), col(g_off)),
            pl.BlockSpec((None, seq_all, LANES), lambda b, p: (b, 0, lr_off)),
            wspec, wspec, bspec, bspec,
            _resident((1, LANES)),
            _resident(a_f.shape), _resident(a_b.shape), _resident(lv_f.shape), _resident(lv_b.shape),
        ],
        out_specs=pl.BlockSpec((None, seq_all, 2 * LANES), lambda b, p: (b, 0, p)),
        scratch_shapes=_scan_scratch(seq_all, 2),
        compiler_params=_cparams(("parallel", "parallel")),
        name="gla",
    )(p3, p3, p3, p3, p3, wg[0], wg[1], bias[0], bias[1], gain.reshape(1, LANES), a_f, a_b, lv_f, lv_b)


def _hgrn_kernel(q_ref, ff_ref, fb_ref, i_ref, g_ref, lb_ref, gain_ref,
                 af_ref, ab_ref, lvf_ref, lvb_ref, o_ref, of_ref, ob_ref, v_s, *stage_refs, seq_all, layer):
    def lower_bound(d):
        raw = lb_ref[d]
        ex = jnp.exp(raw - jnp.max(raw, axis=0, keepdims=True))
        p = ex / jnp.sum(ex, axis=0, keepdims=True)
        return jnp.sum(p[1:layer + 1], axis=0, keepdims=True)

    def load_common(rows):
        return _silu(q_ref[rows, :]) * (C_DH ** -0.5), i_ref[rows, :].astype(BF16)

    def load_dir(f_ref, lb):
        def load(rows):
            f = lb + (1.0 - lb) * jax.nn.sigmoid(f_ref[rows, :])
            return 1.0 - f, jnp.log(f)
        return load

    _scan_loop(load_common, (load_dir(ff_ref, lower_bound(0)), load_dir(fb_ref, lower_bound(1))), (of_ref, ob_ref),
               (af_ref, ab_ref), (lvf_ref, lvb_ref), v_s, (stage_refs[:5], stage_refs[5:]), n_heads=1, seq_all=seq_all)

    def post(j, carry):
        rows = pl.ds(pl.multiple_of(j * ROW_TILE, ROW_TILE), ROW_TILE)
        o = of_ref[rows, :] + ob_ref[rows, :]
        var = jnp.mean(o * o, axis=-1, keepdims=True)
        o_ref[rows, :] = (o * lax.rsqrt(var + EPS) * gain_ref[...] * _silu(g_ref[rows, :])).astype(BF16)
        return carry
    lax.fori_loop(0, seq_all // ROW_TILE, post, 0)


def _hgrn(p3, lb_raw, gain, consts, layer):
    batch, seq_all, _ = p3.shape
    a_f, a_b, lv_f, lv_b = consts
    col = lambda off: (lambda b, h: (b, 0, off + h))
    return pl.pallas_call(
        functools.partial(_hgrn_kernel, seq_all=seq_all, layer=layer),
        out_shape=jax.ShapeDtypeStruct((batch, seq_all, C_HEADS * LANES), BF16),
        grid=(batch, C_HEADS),
        in_specs=[pl.BlockSpec((None, seq_all, LANES), col(s * C_HEADS)) for s in range(5)] + [
            pl.BlockSpec((2, DEPTH, LANES), lambda b, h: (0, 0, h)),
            _resident((1, LANES)),
            _resident(a_f.shape), _resident(a_b.shape), _resident(lv_f.shape), _resident(lv_b.shape),
        ],
        out_specs=pl.BlockSpec((None, seq_all, LANES), lambda b, h: (b, 0, h)),
        scratch_shapes=_scan_scratch(seq_all, 1),
        compiler_params=_cparams(("parallel", "parallel")),
        name="hgrn",
    )(p3, p3, p3, p3, p3, lb_raw, gain.reshape(1, LANES), a_f, a_b, lv_f, lv_b)


def _rope_tables(seq):
    pos = jnp.arange(seq)
    row_ids = (pos // GRID_W).astype(F32)
    col_ids = (pos % GRID_W).astype(F32)
    n_axis = A_DQK // 2
    inv = ROPE_BASE ** (-jnp.arange(0, n_axis, 2, dtype=F32) / n_axis)
    ang_r = row_ids[:, None] * inv
    ang_c = col_ids[:, None] * inv
    zeros = jnp.zeros_like(ang_r)
    cos64 = jnp.concatenate([jnp.cos(ang_r)] * 2 + [jnp.cos(ang_c)] * 2, axis=-1)
    sa64 = jnp.concatenate([-jnp.sin(ang_r), zeros, -jnp.sin(ang_c), zeros], axis=-1)
    sb64 = jnp.concatenate([zeros, jnp.sin(ang_r), zeros, jnp.sin(ang_c)], axis=-1)
    ident = lambda v: jnp.full((CTX_LEN, LANES), v, F32)
    full = lambda t, v: jnp.concatenate([ident(v), jnp.tile(t, (1, 2))], axis=0)
    return full(cos64, 1.0), full(sa64, 0.0), full(sb64, 0.0)


def kernel(x, c, ctx, c_ctx, w_ada, b_ada, norm1_gain, norm2_gain, w_in_even, qk_gain_a, lambda_a, subln_gain_a, w_gate_up_b, b_gate_up_b, onorm_gain_b, w_out_even, w_in_odd, lb_raw_c, onorm_gain_c, w_out_odd, w_ffn_in, w_ffn_out):
    batch, seq, d = x.shape
    seq_all = CTX_LEN + seq
    assert d == D_MODEL and ctx.shape[1] == CTX_LEN == ROW_TILE == Q_TILE and seq % ROW_TILE == 0

    xz = jnp.concatenate([ctx, x], axis=1).reshape(batch * seq_all, d)
    mod_rows = -(-(batch + 1) // 8) * 8
    cc = jnp.zeros((mod_rows, d), F32).at[:batch].set(c).at[batch].set(c_ctx)
    mods = _ada(cc, w_ada, b_ada).reshape(DEPTH, mod_rows, 6, d)

    tabs = _rope_tables(seq)
    half = np.arange(LANES) // A_DQK
    bd = jnp.asarray((half[:, None] == half[None, :]) / A_DQK, BF16)
    consts_gla = _scan_consts(2)
    consts_hgrn = _scan_consts(1)

    for l in range(DEPTH):
        j = l // 2
        last = l == DEPTH - 1
        wi = w_ffn_in[l].astype(BF16)
        wf = w_ffn_out[l].astype(BF16)
        if l % 2 == 0:
            w = jnp.pad(w_in_even[j], ((0, 0), (0, EVEN_COLS_PAD - EVEN_COLS))).astype(BF16)
            p3 = _inproj(xz, mods[l], norm1_gain[l], w, batch).reshape(batch, seq_all, EVEN_COLS_PAD)
            lambda_init = 0.8 - 0.6 * math.exp(-0.3 * l)
            oa = _attn(p3, tabs, qk_gain_a[j], lambda_a[j], subln_gain_a[j], bd, lambda_init)
            ob = _gla(p3, w_gate_up_b[j], b_gate_up_b[j], onorm_gain_b[j], consts_gla)
            a_width = A_HEADS * LANES
            wo = w_out_even[j].astype(BF16)
            o_parts = [oa.reshape(batch * seq_all, -1), ob.reshape(batch * seq_all, -1)]
            wo_parts = [wo[:a_width], wo[a_width:]]
        else:
            p3 = _inproj(xz, mods[l], norm1_gain[l], w_in_odd[j].astype(BF16), batch).reshape(batch, seq_all, ODD_COLS)
            oc = _hgrn(p3, lb_raw_c, onorm_gain_c[j], consts_hgrn, l)
            o_parts = [oc.reshape(batch * seq_all, -1)]
            wo_parts = [w_out_odd[j].astype(BF16)]
        xz = _post(o_parts, xz, mods[l], norm2_gain[l], wo_parts, wi, wf, batch, last)
    return xz.reshape(batch, seq, d)
```

```python
import functools
import math

import numpy as np
import jax
import jax.numpy as jnp
from jax import lax
from jax.experimental import pallas as pl
from jax.experimental.pallas import tpu as pltpu

F32 = jnp.float32
BF16 = jnp.bfloat16

D_MODEL = 1024
DEPTH = 4
CTX_LEN = 256
GRID_W = 64
A_HEADS = 4
A_DQK = 64
B_HEADS = 4
B_DK = 64
B_GATE_RANK = 16
B_GATE_NORM = 16.0
C_HEADS = 8
C_DH = 128
ROPE_BASE = 10000.0
EPS = 1e-6
D_FF = 2816
EVEN_COLS = 3104
EVEN_COLS_PAD = 3200
ODD_COLS = 5120

LANES = 128
ROW_TILE = 256
COL_CHUNK = 512
FF_CHUNK = 1408
Q_TILE = 256
SUM_ROWS = 16
LOG2E = math.log2(math.e)
SCAN_CHUNK = 128
SCAN_UNROLL = 6
STAGE_UNROLL = 6
VMEM_LIMIT = 52 * 1024 * 1024


def _cparams(sem):
    return pltpu.CompilerParams(dimension_semantics=sem, vmem_limit_bytes=VMEM_LIMIT)


def _resident(shape):
    nd = len(shape)
    return pl.BlockSpec(shape, lambda *_: (0,) * nd, pipeline_mode=pl.Buffered(1))


def _silu(x):
    return x * jax.nn.sigmoid(x)


def _dot(a, b):
    return jnp.dot(a, b, preferred_element_type=F32)


def _dot_nt(a, b):
    return lax.dot_general(a, b, (((1,), (1,)), ((), ())), preferred_element_type=F32)


def _dot_tn(a, b):
    return lax.dot_general(a, b, (((0,), (0,)), ((), ())), preferred_element_type=F32)


def _ada_kernel(c_ref, w_ref, b_ref, o_ref):
    s = _silu(c_ref[...]).astype(BF16)
    o_ref[...] = _dot(s, w_ref[...].astype(BF16)) + b_ref[...]


def _ada(cc, w_ada, b_ada):
    depth, d, n = w_ada.shape
    rows = cc.shape[0]
    tn = 1536
    return pl.pallas_call(
        _ada_kernel,
        out_shape=jax.ShapeDtypeStruct((depth, rows, n), F32),
        grid=(depth, n // tn),
        in_specs=[
            pl.BlockSpec((rows, d), lambda l, j: (0, 0)),
            pl.BlockSpec((None, d, tn), lambda l, j: (l, 0, j)),
            pl.BlockSpec((None, 1, tn), lambda l, j: (l, 0, j)),
        ],
        out_specs=pl.BlockSpec((None, rows, tn), lambda l, j: (l, 0, j)),
        compiler_params=_cparams(("parallel", "parallel")),
        name="ada",
    )(cc, w_ada, b_ada.reshape(depth, 1, n))


def _norm_mod(x, gain, shift, scale):
    var = jnp.mean(x * x, axis=-1, keepdims=True)
    return (x * lax.rsqrt(var + EPS) * gain) * (1.0 + scale) + shift


def _inproj_kernel(x_ref, mod_ref, gain_ref, w_ref, o_ref):
    h = _norm_mod(x_ref[...], gain_ref[...], mod_ref[0:1, :], mod_ref[1:2, :]).astype(BF16)
    ncols = w_ref.shape[1]
    for c0 in range(0, ncols, COL_CHUNK):
        c1 = min(c0 + COL_CHUNK, ncols)
        o_ref[:, c0:c1] = _dot(h, w_ref[:, c0:c1])


def _tiles_per_sample(seq_all):
    return seq_all // ROW_TILE


def _mod_row(i, tiles, batch):
    return jnp.where(i % tiles == 0, batch, i // tiles)


def _inproj(xz, mod, gain, w, batch):
    rows, d = xz.shape
    ncols = w.shape[1]
    tiles = _tiles_per_sample(rows // batch)
    return pl.pallas_call(
        _inproj_kernel,
        out_shape=jax.ShapeDtypeStruct((rows, ncols), F32),
        grid=(rows // ROW_TILE,),
        in_specs=[
            pl.BlockSpec((ROW_TILE, d), lambda i: (i, 0)),
            pl.BlockSpec((None, 6, d), lambda i: (_mod_row(i, tiles, batch), 0, 0)),
            _resident((1, d)),
            _resident((d, ncols)),
        ],
        out_specs=pl.BlockSpec((ROW_TILE, ncols), lambda i: (i, 0)),
        compiler_params=_cparams(("parallel",)),
        name="inproj",
    )(xz, mod, gain.reshape(1, d), w)


def _post_kernel(*refs, n_parts):
    o_refs = refs[:n_parts]
    x_ref, mod_ref, gain_ref = refs[n_parts:n_parts + 3]
    wo_refs = refs[n_parts + 3:2 * n_parts + 3]
    wi_ref, wf_ref, out_ref = refs[2 * n_parts + 3:]
    y = _dot(o_refs[0][...], wo_refs[0][...])
    for p in range(1, n_parts):
        y = y + _dot(o_refs[p][...], wo_refs[p][...])
    x1 = x_ref[...] + mod_ref[2:3, :] * y
    h = _norm_mod(x1, gain_ref[...], mod_ref[3:4, :], mod_ref[4:5, :]).astype(BF16)
    acc = None
    for c0 in range(0, D_FF, FF_CHUNK):
        gate = _dot(h, wi_ref[:, c0:c0 + FF_CHUNK])
        up = _dot(h, wi_ref[:, D_FF + c0:D_FF + c0 + FF_CHUNK])
        a = (_silu(gate) * up).astype(BF16)
        part = _dot(a, wf_ref[c0:c0 + FF_CHUNK, :])
        acc = part if acc is None else acc + part
    out_ref[...] = x1 + mod_ref[5:6, :] * acc


def _post(o_parts, xz, mod, gain, wo_parts, wi, wf, batch, latent_only):
    rows, d = xz.shape
    tiles = _tiles_per_sample(rows // batch)
    if latent_only:
        out_tiles = tiles - 1
        src = lambda i: (i // out_tiles) * tiles + 1 + i % out_tiles
        mrow = lambda i: i // out_tiles
    else:
        out_tiles = tiles
        src = lambda i: i
        mrow = lambda i: _mod_row(i, tiles, batch)
    n_parts = len(o_parts)
    in_specs = [pl.BlockSpec((ROW_TILE, o.shape[1]), lambda i: (src(i), 0)) for o in o_parts]
    in_specs += [
        pl.BlockSpec((ROW_TILE, d), lambda i: (src(i), 0)),
        pl.BlockSpec((None, 6, d), lambda i: (mrow(i), 0, 0)),
        _resident((1, d)),
    ]
    in_specs += [_resident(w.shape) for w in wo_parts]
    in_specs += [_resident(wi.shape), _resident(wf.shape)]
    return pl.pallas_call(
        functools.partial(_post_kernel, n_parts=n_parts),
        out_shape=jax.ShapeDtypeStruct((batch * out_tiles * ROW_TILE, d), F32),
        grid=(batch * out_tiles,),
        in_specs=in_specs,
        out_specs=pl.BlockSpec((ROW_TILE, d), lambda i: (i, 0)),
        compiler_params=_cparams(("parallel",)),
        name="post",
    )(*o_parts, xz, mod, gain.reshape(1, d), *wo_parts, wi, wf)


def _attn_kernel(qc_ref, qa_ref, qb_ref, k_ref, v_ref, c_ref, sa_ref, sb_ref,
                 qg_ref, kg_ref, sg_ref, lam_ref, bd_ref, octx_ref, olat_ref, kb_ref, vt_ref,
                 *, lambda_init, seq_all):
    step = pl.program_id(2)

    def norm_rope(x, g, rows):
        ms = _dot((x * x).astype(BF16), bd_ref[...])
        y = x * lax.rsqrt(ms + EPS) * g
        return y * c_ref[rows, :] + pltpu.roll(y, LANES - 16, 1) * sa_ref[rows, :] + pltpu.roll(y, 16, 1) * sb_ref[rows, :]

    def tile_rows(t):
        return pl.ds(pl.multiple_of(t * Q_TILE, Q_TILE), Q_TILE)

    @pl.when(step == 0)
    def _prep_keys():
        def body(r, carry):
            rows = tile_rows(r)
            kb_ref[rows, :] = norm_rope(k_ref[rows, :], kg_ref[...], rows).astype(BF16)
            vt_ref[0:LANES, rows] = v_ref[rows, :].T.astype(BF16)
            return carry
        lax.fori_loop(0, seq_all // Q_TILE, body, 0)
        vt_ref[LANES:, :] = jnp.ones((SUM_ROWS, seq_all), BF16)

    lam_v = lam_ref[...]
    e1 = jnp.exp(jnp.sum(lam_v[0:1, :] * lam_v[1:2, :], axis=-1, keepdims=True))
    e2 = jnp.exp(jnp.sum(lam_v[2:3, :] * lam_v[3:4, :], axis=-1, keepdims=True))
    lam = e1 - e2 + lambda_init

    def scores(q_ref, tile, n_keys):
        qn = norm_rope(q_ref[...], qg_ref[...], tile_rows(tile)) * (A_DQK ** -0.5 * LOG2E)
        lane = lax.broadcasted_iota(jnp.int32, qn.shape, 1)
        q2 = jnp.concatenate([jnp.where(lane < A_DQK, qn, 0.0).astype(BF16),
                              jnp.where(lane >= A_DQK, qn, 0.0).astype(BF16)], axis=0)
        return _dot_nt(kb_ref[0:n_keys, :], q2)

    def outputs(s, n_keys):
        m = jnp.max(s, axis=0, keepdims=True)
        p = jnp.exp2(s - m).astype(BF16)
        pv = _dot(vt_ref[:, 0:n_keys], p)
        pv = pv[0:LANES] / pv[LANES:LANES + 1]
        o = pv[:, :Q_TILE] - lam * pv[:, Q_TILE:]
        var = jnp.mean(o * o, axis=0, keepdims=True)
        o = o * lax.rsqrt(var + EPS) * (sg_ref[...] * (1.0 - lambda_init))
        return o.T.astype(BF16)

    tile_a = 1 + 2 * step
    s_a = scores(qa_ref, tile_a, seq_all)
    s_b = scores(qb_ref, tile_a + 1, seq_all)
    olat_ref[0:Q_TILE, :] = outputs(s_a, seq_all)
    olat_ref[Q_TILE:, :] = outputs(s_b, seq_all)

    @pl.when(step == 0)
    def _context_tile():
        octx_ref[...] = outputs(scores(qc_ref, 0, CTX_LEN), CTX_LEN)


def _attn(p3, tabs, qk_gain, lam, sub_gain, bd, lambda_init):
    batch, seq_all, _ = p3.shape
    seq = seq_all - CTX_LEN
    assert seq % (2 * Q_TILE) == 0
    qg = jnp.tile(qk_gain[0], 2).reshape(1, LANES)
    kg = jnp.tile(qk_gain[1], 2).reshape(1, LANES)
    full = _resident((seq_all, LANES))
    kern = functools.partial(_attn_kernel, lambda_init=lambda_init, seq_all=seq_all)
    o_ctx, o_lat = pl.pallas_call(
        kern,
        out_shape=(jax.ShapeDtypeStruct((batch, CTX_LEN, A_HEADS * LANES), BF16),
                   jax.ShapeDtypeStruct((batch, seq, A_HEADS * LANES), BF16)),
        grid=(batch, A_HEADS, seq // (2 * Q_TILE)),
        in_specs=[
            pl.BlockSpec((None, Q_TILE, LANES), lambda b, h, i: (b, 0, h)),
            pl.BlockSpec((None, Q_TILE, LANES), lambda b, h, i: (b, 1 + 2 * i, h)),
            pl.BlockSpec((None, Q_TILE, LANES), lambda b, h, i: (b, 2 + 2 * i, h)),
            pl.BlockSpec((None, seq_all, LANES), lambda b, h, i: (b, 0, A_HEADS + h)),
            pl.BlockSpec((None, seq_all, LANES), lambda b, h, i: (b, 0, 2 * A_HEADS + h)),
            full, full, full,
            _resident((1, LANES)), _resident((1, LANES)), _resident((LANES, 1)),
            _resident((4, A_DQK)), _resident((LANES, LANES)),
        ],
        out_specs=(pl.BlockSpec((None, CTX_LEN, LANES), lambda b, h, i: (b, 0, h)),
                   pl.BlockSpec((None, 2 * Q_TILE, LANES), lambda b, h, i: (b, i, h))),
        scratch_shapes=[pltpu.VMEM((seq_all, LANES), BF16), pltpu.VMEM((LANES + SUM_ROWS, seq_all), BF16)],
        compiler_params=_cparams(("parallel", "parallel", "arbitrary")),
        name="attn",
    )(p3, p3, p3, p3, p3, *tabs, qg, kg, sub_gain.reshape(LANES, 1), lam, bd)
    return jnp.concatenate([o_ctx, o_lat], axis=1)


MATMUL_LEVELS = 2
LOCAL_BLOCK = 64
N_STAGED = 6
SAFE_LOG2 = 100.0


def _scan_levels():
    return int(math.log2(SCAN_CHUNK))


def _scan_consts(n_heads):
    c = SCAN_CHUNK
    t = np.arange(c)[:, None]
    u = np.arange(c)[None, :]
    blocks_f = [u <= t]
    blocks_b = [u >= t]
    for lvl in range(1, MATMUL_LEVELS + 1):
        h = 1 << (lvl - 1)
        m = (t // (2 * h)) * (2 * h) + h
        second = (t % (2 * h)) >= h
        blocks_f.append(np.where(second, (u >= m) & (u <= t), (u > t) & (u <= m - 1)))
        blocks_b.append(np.where(second, (u >= m) & (u < t), (u >= t) & (u <= m - 1)))
    a_f = np.tile(np.concatenate(blocks_f, axis=0).astype(np.float32), (1, 2))
    a_b = np.tile(np.concatenate(blocks_b, axis=0).astype(np.float32), (1, 2))
    x = t ^ u
    lv = np.where(x > 0, np.floor(np.log2(np.maximum(x, 1))).astype(np.int32) + 1, 0)
    lv_f = np.where(u <= t, lv, -1).astype(np.int32)
    lv_b = np.where(u >= t, lv, -1).astype(np.int32)
    lv_f = np.tile(lv_f, (n_heads, 1))
    lv_b = np.tile(lv_b, (n_heads, 1))
    return (jnp.asarray(a_f, BF16), jnp.asarray(a_b, BF16), jnp.asarray(lv_f), jnp.asarray(lv_b))


def _scan_sums(g, a):
    g2 = g * LOG2E
    g_hi = g2.astype(BF16)
    g_lo = (g2 - g_hi.astype(F32)).astype(BF16)
    return _dot(a, jnp.concatenate([g_hi, g_lo], axis=0))


def _local_exponent(b, backward):
    c = SCAN_CHUNK
    b3 = b.reshape(c // LOCAL_BLOCK, LOCAL_BLOCK, LANES)
    r0 = LOCAL_BLOCK // 2 if backward else LOCAL_BLOCK // 2 - 1
    return (b3 - b3[:, r0:r0 + 1, :]).reshape(c, LANES)


def _level_exponent(b, lvl, backward):
    c = SCAN_CHUNK
    h = 1 << (lvl - 1)
    r0 = h if backward else h - 1
    b3 = b.reshape(c // (2 * h), 2 * h, LANES)
    ref = b3[:, r0:r0 + 1, :]
    if h % 8 == 0:
        first, second = b3[:, :h, :], b3[:, h:, :]
        x = jnp.concatenate([first - ref, ref - second] if backward else [ref - first, second - ref], axis=1)
    else:
        x = -jnp.abs(b3 - ref)
    return x.reshape(c, LANES)


def _by_head(xb, n_heads):
    if n_heads == 1:
        return xb
    dk = LANES // n_heads
    lane = lax.broadcasted_iota(jnp.int32, xb.shape, 1)
    masks = [jnp.where((lane >= hh * dk) & (lane < (hh + 1) * dk), 1.0, 0.0).astype(BF16) for hh in range(n_heads)]
    return jnp.concatenate([xb * hm for hm in masks], axis=0)


def _head_block_mask(shape, row_block, col_block):
    r = lax.broadcasted_iota(jnp.int32, shape, 0) // row_block
    cc = lax.broadcasted_iota(jnp.int32, shape, 1) // col_block
    return r == cc


def _level_chunk(qb, kb, b, fine, vb, st, lv_ref, *, n_heads, backward):
    c = SCAN_CHUNK
    nl = _scan_levels()
    edge = b[0:1, :] if backward else b[c - 1:c, :]
    e_in = jnp.exp2(b).astype(BF16)
    e_out = jnp.exp2(edge - b).astype(BF16)

    def level_decay(lvl):
        if lvl <= MATMUL_LEVELS:
            return jnp.exp2(fine[(lvl - 1) * c:lvl * c]).astype(BF16)
        return jnp.exp2(_level_exponent(b, lvl, backward)).astype(BF16)

    o = _dot_nt(qb * e_in, st.astype(BF16))

    lv = lv_ref[...]
    att = jnp.zeros((n_heads * c, c), F32)
    for lvl in range(nl + 1):
        if lvl == 0:
            qh, kh = qb, kb
        else:
            el = level_decay(lvl)
            qh, kh = qb * el, kb * el
        att = jnp.where(lv == lvl, _dot_nt(_by_head(qh, n_heads), kh), att)
    attb = att.astype(BF16)
    intra = [_dot(attb[hh * c:(hh + 1) * c], vb[:, hh * LANES:(hh + 1) * LANES]) for hh in range(n_heads)]
    o = o + (intra[0] if n_heads == 1 else jnp.concatenate(intra, axis=1))

    dec = jnp.exp2(edge)
    upd = _dot_tn(vb, kb * e_out)
    if n_heads > 1:
        upd = jnp.where(_head_block_mask(upd.shape, LANES, LANES // n_heads), upd, 0.0)
    return o, dec * st + upd


def _shortcut_pairs(qp, kp, qt, kt, kd, vb, lv_ref, *, n_heads):
    lv = lv_ref[...]
    local = _dot_nt(_by_head(qp, n_heads), kp)
    top = _dot_nt(_by_head(qt, n_heads), kt)
    att = jnp.where(lv == _scan_levels(), top, jnp.where(lv >= 0, local, 0.0)).astype(BF16)
    upd = _dot_tn(kd, vb)
    if n_heads > 1:
        upd = jnp.where(_head_block_mask(upd.shape, LANES // n_heads, LANES), upd, 0.0)
    return att, upd


def _shortcut_output(att, upd, qe, vb, dec, s, *, n_heads):
    c = SCAN_CHUNK
    sb = s.astype(BF16)
    outs = []
    for hh in range(n_heads):
        cols = slice(hh * LANES, (hh + 1) * LANES)
        lhs = jnp.concatenate([att[hh * c:(hh + 1) * c], qe], axis=1)
        rhs = jnp.concatenate([vb[:, cols], sb[:, cols]], axis=0)
        outs.append(_dot(lhs, rhs))
    o = outs[0] if n_heads == 1 else jnp.concatenate(outs, axis=1)
    if n_heads > 1:
        dec = jnp.concatenate([dec] * n_heads, axis=1)
    return o, dec * s + upd


def _scan_loop(load_common, load_dir, out_refs, a_refs, lv_refs, v_s, stage_refs, *, n_heads, seq_all):
    c = SCAN_CHUNK
    n = seq_all // c
    n_ctx = CTX_LEN // c

    def chunk_rows(j):
        return pl.ds(pl.multiple_of(j * c, c), c)

    def scan(prepare, finish, state0, unroll):
        def body(i, sts):
            sts = list(sts)
            todo = []
            for u in range(unroll):
                j = i * unroll + u
                jb = jnp.where(j < n_ctx, n_ctx - 1 - j, n - 1 - (j - n_ctx))
                for d, jj in ((0, j), (1, jb)):
                    rows = chunk_rows(jj)
                    todo.append((d, rows, prepare(d, rows)))
            for d, rows, values in todo:
                out_refs[d][rows, :], sts[d] = finish(d, rows, values, sts[d])
            return tuple(sts)
        lax.fori_loop(0, n // unroll, body, (state0, state0))

    def stage(j, worst):
        gated = []
        for u in range(STAGE_UNROLL):
            rows = chunk_rows(j * STAGE_UNROLL + u)
            q, vb = load_common(rows)
            v_s[rows, :] = vb
            gated.append((rows, q, [load_dir[d](rows) for d in (0, 1)]))
        summed = [(rows, q, [(k, _scan_sums(g, a_refs[d][0:c, :])) for d, (k, g) in enumerate(kg)])
                  for rows, q, kg in gated]
        for rows, q, kb in summed:
            for d, (k, b) in enumerate(kb):
                x = _local_exponent(b, bool(d))
                e_top = jnp.exp2(_level_exponent(b, _scan_levels(), bool(d)))
                edge = b[0:1, :] if d else b[c - 1:c, :]
                qp_s, kp_s, qt_s, kt_s, qe_s, kd_s, dec_s = stage_refs[d]
                qp_s[rows, :] = (q * jnp.exp2(x)).astype(BF16)
                kp_s[rows, :] = (k * jnp.exp2(-x)).astype(BF16)
                qt_s[rows, :] = (q * e_top).astype(BF16)
                kt_s[rows, :] = (k * e_top).astype(BF16)
                qe_s[rows, :] = (q * jnp.exp2(b)).astype(BF16)
                kd_s[rows, :] = (k * jnp.exp2(edge - b)).astype(BF16)
                dec_s[rows, :] = jnp.broadcast_to(jnp.exp2(edge), (LANES, LANES)).T
                worst = jnp.maximum(worst, jnp.max(jnp.abs(x).reshape(c // 8, 8, LANES), axis=0))
        return worst
    worst = lax.fori_loop(0, n // STAGE_UNROLL, stage, jnp.zeros((8, LANES), F32))
    safe = jnp.max(worst) < SAFE_LOG2

    def shortcut_pairs(d, rows):
        qp_s, kp_s, qt_s, kt_s, _, kd_s, _ = stage_refs[d]
        return _shortcut_pairs(qp_s[rows, :], kp_s[rows, :], qt_s[rows, :], kt_s[rows, :], kd_s[rows, :],
                               v_s[rows, :], lv_refs[d], n_heads=n_heads)

    def shortcut_output(d, rows, values, s):
        _, _, _, _, qe_s, _, dec_s = stage_refs[d]
        return _shortcut_output(*values, qe_s[rows, :], v_s[rows, :], dec_s[rows, :], s, n_heads=n_heads)

    def levels(d, rows, values, st):
        q, vb = load_common(rows)
        k, g = load_dir[d](rows)
        sums = _scan_sums(g, a_refs[d][...])
        return _level_chunk(q.astype(BF16), k.astype(BF16), sums[0:c], sums[c:], vb, st, lv_refs[d],
                            n_heads=n_heads, backward=bool(d))

    pl.when(safe)(lambda: scan(shortcut_pairs, shortcut_output, jnp.zeros((LANES, n_heads * LANES), F32), SCAN_UNROLL))
    pl.when(jnp.logical_not(safe))(lambda: scan(lambda d, rows: None, levels, jnp.zeros((n_heads * LANES, LANES), F32), 1))


def _scan_scratch(seq_all, n_heads):
    assert SCAN_CHUNK == LANES == 2 * LOCAL_BLOCK
    width = n_heads * LANES
    out = [pltpu.VMEM((seq_all, width), F32)] * 2
    stage = [pltpu.VMEM((seq_all, LANES), BF16)] * N_STAGED + [pltpu.VMEM((seq_all, LANES), F32)]
    return out + [pltpu.VMEM((seq_all, width), BF16)] + stage * 2


def _log_sigmoid(x):
    return jnp.minimum(x, 0.0) - jnp.log(1.0 + jnp.exp(-jnp.abs(x)))


def _gla_kernel(q_ref, k_ref, v_ref, g_ref, lr_ref, wgf_ref, wgb_ref, bf_ref, bb_ref, gain_ref,
                af_ref, ab_ref, lvf_ref, lvb_ref, o_ref, of_ref, ob_ref, v_s, *stage_refs, seq_all):
    def load_common(rows):
        return q_ref[rows, :] * (B_DK ** -0.5), v_ref[rows, :].astype(BF16)

    def load_dir(wg_ref, b_ref):
        def load(rows):
            pre = _dot(lr_ref[rows, :].astype(BF16), wg_ref[...]) + b_ref[...]
            return k_ref[rows, :], _log_sigmoid(pre) * (1.0 / B_GATE_NORM)
        return load

    _scan_loop(load_common, (load_dir(wgf_ref, bf_ref), load_dir(wgb_ref, bb_ref)), (of_ref, ob_ref),
               (af_ref, ab_ref), (lvf_ref, lvb_ref), v_s, (stage_refs[:N_STAGED + 1], stage_refs[N_STAGED + 1:]), n_heads=2, seq_all=seq_all)

    def post(j, carry):
        rows = pl.ds(pl.multiple_of(j * ROW_TILE, ROW_TILE), ROW_TILE)
        for hh in range(2):
            cols = slice(hh * LANES, (hh + 1) * LANES)
            o = of_ref[rows, cols] + ob_ref[rows, cols]
            var = jnp.mean(o * o, axis=-1, keepdims=True)
            o_ref[rows, cols] = (o * lax.rsqrt(var + EPS) * gain_ref[...] * _silu(g_ref[rows, cols])).astype(BF16)
        return carry
    lax.fori_loop(0, seq_all // ROW_TILE, post, 0, unroll=3)


def _gla(p3, w_gate_up, b_gate_up, gain, consts):
    batch, seq_all, _ = p3.shape
    pairs = B_HEADS // 2
    wg = jnp.zeros((2, pairs, LANES, LANES), F32)
    for d in range(2):
        w = w_gate_up[d].reshape(B_GATE_RANK, pairs, LANES).transpose(1, 0, 2)
        wg = wg.at[d, :, d * B_GATE_RANK:(d + 1) * B_GATE_RANK, :].set(w)
    wg = wg.astype(BF16)
    bias = b_gate_up.reshape(2, pairs, 1, LANES)
    a_f, a_b, lv_f, lv_b = consts
    col = lambda off: (lambda b, p: (b, 0, off + p))
    q_off = 3 * A_HEADS * LANES // LANES
    k_off = q_off + B_HEADS * B_DK // LANES
    v_off = (k_off * LANES + B_HEADS * B_DK) // (2 * LANES)
    g_off = v_off + pairs
    lr_off = (g_off + pairs) * 2
    wspec = pl.BlockSpec((None, LANES, LANES), lambda b, p: (p, 0, 0))
    bspec = pl.BlockSpec((None, 1, LANES), lambda b, p: (p, 0, 0))
    return pl.pallas_call(
        functools.partial(_gla_kernel, seq_all=seq_all),
        out_shape=jax.ShapeDtypeStruct((batch, seq_all, B_HEADS * LANES), BF16),
        grid=(batch, pairs),
        in_specs=[
            pl.BlockSpec((None, seq_all, LANES), col(q_off)),
            pl.BlockSpec((None, seq_all, LANES), col(k_off)),
            pl.BlockSpec((None, seq_all, 2 * LANES), col(v_off)),
            pl.BlockSpec((None, seq_all, 2 * LANES), col(g_off)),
            pl.BlockSpec((None, seq_all, LANES), lambda b, p: (b, 0, lr_off)),
            wspec, wspec, bspec, bspec,
            _resident((1, LANES)),
            _resident(a_f.shape), _resident(a_b.shape), _resident(lv_f.shape), _resident(lv_b.shape),
        ],
        out_specs=pl.BlockSpec((None, seq_all, 2 * LANES), lambda b, p: (b, 0, p)),
        scratch_shapes=_scan_scratch(seq_all, 2),
        compiler_params=_cparams(("parallel", "parallel")),
        name="gla",
    )(p3, p3, p3, p3, p3, wg[0], wg[1], bias[0], bias[1], gain.reshape(1, LANES), a_f, a_b, lv_f, lv_b)


def _hgrn_kernel(q_ref, ff_ref, fb_ref, i_ref, g_ref, lb_ref, gain_ref,
                 af_ref, ab_ref, lvf_ref, lvb_ref, o_ref, of_ref, ob_ref, v_s, *stage_refs, seq_all, layer):
    def lower_bound(d):
        raw = lb_ref[d]
        ex = jnp.exp(raw - jnp.max(raw, axis=0, keepdims=True))
        p = ex / jnp.sum(ex, axis=0, keepdims=True)
        return jnp.sum(p[1:layer + 1], axis=0, keepdims=True)

    def load_common(rows):
        return _silu(q_ref[rows, :]) * (C_DH ** -0.5), i_ref[rows, :].astype(BF16)

    def load_dir(f_ref, lb):
        def load(rows):
            f = lb + (1.0 - lb) * jax.nn.sigmoid(f_ref[rows, :])
            return 1.0 - f, jnp.log(f)
        return load

    _scan_loop(load_common, (load_dir(ff_ref, lower_bound(0)), load_dir(fb_ref, lower_bound(1))), (of_ref, ob_ref),
               (af_ref, ab_ref), (lvf_ref, lvb_ref), v_s, (stage_refs[:N_STAGED + 1], stage_refs[N_STAGED + 1:]), n_heads=1, seq_all=seq_all)

    def post(j, carry):
        rows = pl.ds(pl.multiple_of(j * ROW_TILE, ROW_TILE), ROW_TILE)
        o = of_ref[rows, :] + ob_ref[rows, :]
        var = jnp.mean(o * o, axis=-1, keepdims=True)
        o_ref[rows, :] = (o * lax.rsqrt(var + EPS) * gain_ref[...] * _silu(g_ref[rows, :])).astype(BF16)
        return carry
    lax.fori_loop(0, seq_all // ROW_TILE, post, 0, unroll=3)


def _hgrn(p3, lb_raw, gain, consts, layer):
    batch, seq_all, _ = p3.shape
    a_f, a_b, lv_f, lv_b = consts
    col = lambda off: (lambda b, h: (b, 0, off + h))
    return pl.pallas_call(
        functools.partial(_hgrn_kernel, seq_all=seq_all, layer=layer),
        out_shape=jax.ShapeDtypeStruct((batch, seq_all, C_HEADS * LANES), BF16),
        grid=(batch, C_HEADS),
        in_specs=[pl.BlockSpec((None, seq_all, LANES), col(s * C_HEADS)) for s in range(5)] + [
            pl.BlockSpec((2, DEPTH, LANES), lambda b, h: (0, 0, h)),
            _resident((1, LANES)),
            _resident(a_f.shape), _resident(a_b.shape), _resident(lv_f.shape), _resident(lv_b.shape),
        ],
        out_specs=pl.BlockSpec((None, seq_all, LANES), lambda b, h: (b, 0, h)),
        scratch_shapes=_scan_scratch(seq_all, 1),
        compiler_params=_cparams(("parallel", "parallel")),
        name="hgrn",
    )(p3, p3, p3, p3, p3, lb_raw, gain.reshape(1, LANES), a_f, a_b, lv_f, lv_b)


def _rope_tables(seq):
    pos = jnp.arange(seq)
    row_ids = (pos // GRID_W).astype(F32)
    col_ids = (pos % GRID_W).astype(F32)
    n_axis = A_DQK // 2
    inv = ROPE_BASE ** (-jnp.arange(0, n_axis, 2, dtype=F32) / n_axis)
    ang_r = row_ids[:, None] * inv
    ang_c = col_ids[:, None] * inv
    zeros = jnp.zeros_like(ang_r)
    cos64 = jnp.concatenate([jnp.cos(ang_r)] * 2 + [jnp.cos(ang_c)] * 2, axis=-1)
    sa64 = jnp.concatenate([-jnp.sin(ang_r), zeros, -jnp.sin(ang_c), zeros], axis=-1)
    sb64 = jnp.concatenate([zeros, jnp.sin(ang_r), zeros, jnp.sin(ang_c)], axis=-1)
    ident = lambda v: jnp.full((CTX_LEN, LANES), v, F32)
    full = lambda t, v: jnp.concatenate([ident(v), jnp.tile(t, (1, 2))], axis=0)
    return full(cos64, 1.0), full(sa64, 0.0), full(sb64, 0.0)


def kernel(x, c, ctx, c_ctx, w_ada, b_ada, norm1_gain, norm2_gain, w_in_even, qk_gain_a, lambda_a, subln_gain_a, w_gate_up_b, b_gate_up_b, onorm_gain_b, w_out_even, w_in_odd, lb_raw_c, onorm_gain_c, w_out_odd, w_ffn_in, w_ffn_out):
    batch, seq, d = x.shape
    seq_all = CTX_LEN + seq
    assert d == D_MODEL and ctx.shape[1] == CTX_LEN == ROW_TILE == Q_TILE and seq % ROW_TILE == 0

    xz = jnp.concatenate([ctx, x], axis=1).reshape(batch * seq_all, d)
    mod_rows = -(-(batch + 1) // 8) * 8
    cc = jnp.zeros((mod_rows, d), F32).at[:batch].set(c).at[batch].set(c_ctx)
    mods = _ada(cc, w_ada, b_ada).reshape(DEPTH, mod_rows, 6, d)

    tabs = _rope_tables(seq)
    half = np.arange(LANES) // A_DQK
    bd = jnp.asarray((half[:, None] == half[None, :]) / A_DQK, BF16)
    consts_gla = _scan_consts(2)
    consts_hgrn = _scan_consts(1)

    for l in range(DEPTH):
        j = l // 2
        last = l == DEPTH - 1
        wi = w_ffn_in[l].astype(BF16)
        wf = w_ffn_out[l].astype(BF16)
        if l % 2 == 0:
            w = jnp.pad(w_in_even[j], ((0, 0), (0, EVEN_COLS_PAD - EVEN_COLS))).astype(BF16)
            p3 = _inproj(xz, mods[l], norm1_gain[l], w, batch).reshape(batch, seq_all, EVEN_COLS_PAD)
            lambda_init = 0.8 - 0.6 * math.exp(-0.3 * l)
            oa = _attn(p3, tabs, qk_gain_a[j], lambda_a[j], subln_gain_a[j], bd, lambda_init)
            ob = _gla(p3, w_gate_up_b[j], b_gate_up_b[j], onorm_gain_b[j], consts_gla)
            a_width = A_HEADS * LANES
            wo = w_out_even[j].astype(BF16)
            o_parts = [oa.reshape(batch * seq_all, -1), ob.reshape(batch * seq_all, -1)]
            wo_parts = [wo[:a_width], wo[a_width:]]
        else:
            p3 = _inproj(xz, mods[l], norm1_gain[l], w_in_odd[j].astype(BF16), batch).reshape(batch, seq_all, ODD_COLS)
            oc = _hgrn(p3, lb_raw_c, onorm_gain_c[j], consts_hgrn, l)
            o_parts = [oc.reshape(batch * seq_all, -1)]
            wo_parts = [w_out_odd[j].astype(BF16)]
        xz = _post(o_parts, xz, mods[l], norm2_gain[l], wo_parts, wi, wf, batch, last)
    return xz.reshape(batch, seq, d)
```

```python
import functools
import math

import numpy as np
import jax
import jax.numpy as jnp
from jax import lax
from jax.experimental import pallas as pl
from jax.experimental.pallas import tpu as pltpu

F32 = jnp.float32
BF16 = jnp.bfloat16

D_MODEL = 1024
DEPTH = 4
CTX_LEN = 256
GRID_W = 64
A_HEADS = 4
A_DQK = 64
B_HEADS = 4
B_DK = 64
B_GATE_RANK = 16
B_GATE_NORM = 16.0
C_HEADS = 8
C_DH = 128
ROPE_BASE = 10000.0
EPS = 1e-6
D_FF = 2816
EVEN_COLS = 3104
EVEN_COLS_PAD = 3200
ODD_COLS = 5120

LANES = 128
ROW_TILE = 256
COL_CHUNK = 512
FF_CHUNK = 1408
Q_TILE = 256
SUM_ROWS = 16
SAFE_SCORE_LOG2 = 60.0
LOG2E = math.log2(math.e)
SCAN_CHUNK = 128
SCAN_UNROLL = 6
STAGE_UNROLL = 6
VMEM_LIMIT = 52 * 1024 * 1024


def _cparams(sem):
    return pltpu.CompilerParams(dimension_semantics=sem, vmem_limit_bytes=VMEM_LIMIT)


def _resident(shape):
    nd = len(shape)
    return pl.BlockSpec(shape, lambda *_: (0,) * nd, pipeline_mode=pl.Buffered(1))


def _silu(x):
    return x * jax.nn.sigmoid(x)


def _dot(a, b):
    return jnp.dot(a, b, preferred_element_type=F32)


def _dot_nt(a, b):
    return lax.dot_general(a, b, (((1,), (1,)), ((), ())), preferred_element_type=F32)


def _dot_tn(a, b):
    return lax.dot_general(a, b, (((0,), (0,)), ((), ())), preferred_element_type=F32)


def _ada_kernel(c_ref, w_ref, b_ref, o_ref):
    s = _silu(c_ref[...]).astype(BF16)
    o_ref[...] = _dot(s, w_ref[...].astype(BF16)) + b_ref[...]


def _ada(cc, w_ada, b_ada):
    depth, d, n = w_ada.shape
    rows = cc.shape[0]
    tn = 1536
    return pl.pallas_call(
        _ada_kernel,
        out_shape=jax.ShapeDtypeStruct((depth, rows, n), F32),
        grid=(depth, n // tn),
        in_specs=[
            pl.BlockSpec((rows, d), lambda l, j: (0, 0)),
            pl.BlockSpec((None, d, tn), lambda l, j: (l, 0, j)),
            pl.BlockSpec((None, 1, tn), lambda l, j: (l, 0, j)),
        ],
        out_specs=pl.BlockSpec((None, rows, tn), lambda l, j: (l, 0, j)),
        compiler_params=_cparams(("parallel", "parallel")),
        name="ada",
    )(cc, w_ada, b_ada.reshape(depth, 1, n))


def _norm_mod(x, gain, shift, scale):
    var = jnp.mean(x * x, axis=-1, keepdims=True)
    return (x * lax.rsqrt(var + EPS) * gain) * (1.0 + scale) + shift


def _inproj_kernel(x_ref, mod_a_ref, mod_b_ref, gain_ref, w_ref, o_ref):
    halves = [_norm_mod(x_ref[t * ROW_TILE:(t + 1) * ROW_TILE, :], gain_ref[...], m[0:1, :], m[1:2, :]).astype(BF16)
              for t, m in enumerate((mod_a_ref, mod_b_ref))]
    h = jnp.concatenate(halves, axis=0)
    ncols = w_ref.shape[1]
    for c0 in range(0, ncols, COL_CHUNK):
        c1 = min(c0 + COL_CHUNK, ncols)
        o_ref[:, c0:c1] = _dot(h, w_ref[:, c0:c1])


def _tiles_per_sample(seq_all):
    return seq_all // ROW_TILE


def _mod_row(i, tiles, batch):
    return jnp.where(i % tiles == 0, batch, i // tiles)


def _inproj(xz, mod, gain, w, batch):
    rows, d = xz.shape
    ncols = w.shape[1]
    tiles = _tiles_per_sample(rows // batch)
    mod_spec = lambda t: pl.BlockSpec((None, 6, d), lambda i: (_mod_row(2 * i + t, tiles, batch), 0, 0))
    return pl.pallas_call(
        _inproj_kernel,
        out_shape=jax.ShapeDtypeStruct((rows, ncols), F32),
        grid=(rows // (2 * ROW_TILE),),
        in_specs=[
            pl.BlockSpec((2 * ROW_TILE, d), lambda i: (i, 0)),
            mod_spec(0), mod_spec(1),
            _resident((1, d)),
            _resident((d, ncols)),
        ],
        out_specs=pl.BlockSpec((2 * ROW_TILE, ncols), lambda i: (i, 0)),
        compiler_params=_cparams(("parallel",)),
        name="inproj",
    )(xz, mod, mod, gain.reshape(1, d), w)


def _post_kernel(*refs, n_parts):
    o_refs = [refs[2 * p:2 * p + 2] for p in range(n_parts)]
    rest = refs[2 * n_parts:]
    x_refs, mod_refs, gain_ref = rest[0:2], rest[2:4], rest[4]
    wo_refs = rest[5:5 + n_parts]
    wi_ref, wf_ref, out_ref = rest[5 + n_parts:]
    y = None
    for (oa_ref, ob_ref), wo_ref in zip(o_refs, wo_refs):
        part = _dot(jnp.concatenate([oa_ref[...], ob_ref[...]], axis=0), wo_ref[...])
        y = part if y is None else y + part
    x1 = [x_refs[t][...] + mod_refs[t][2:3, :] * y[t * ROW_TILE:(t + 1) * ROW_TILE] for t in range(2)]
    h = jnp.concatenate([_norm_mod(x1[t], gain_ref[...], mod_refs[t][3:4, :], mod_refs[t][4:5, :]).astype(BF16)
                         for t in range(2)], axis=0)
    acc = None
    for c0 in range(0, D_FF, FF_CHUNK):
        gate = _dot(h, wi_ref[:, c0:c0 + FF_CHUNK])
        up = _dot(h, wi_ref[:, D_FF + c0:D_FF + c0 + FF_CHUNK])
        a = (_silu(gate) * up).astype(BF16)
        part = _dot(a, wf_ref[c0:c0 + FF_CHUNK, :])
        acc = part if acc is None else acc + part
    for t in range(2):
        rows = slice(t * ROW_TILE, (t + 1) * ROW_TILE)
        out_ref[rows, :] = x1[t] + mod_refs[t][5:6, :] * acc[rows]


def _post(o_parts, xz, mod, gain, wo_parts, wi, wf, batch, latent_only):
    rows, d = xz.shape
    tiles = _tiles_per_sample(rows // batch)
    if latent_only:
        out_tiles = tiles - 1
        src = lambda i: (i // out_tiles) * tiles + 1 + i % out_tiles
        mrow = lambda i: i // out_tiles
    else:
        out_tiles = tiles
        src = lambda i: i
        mrow = lambda i: _mod_row(i, tiles, batch)
    assert (batch * out_tiles) % 2 == 0
    n_parts = len(o_parts)
    row_spec = lambda width, t: pl.BlockSpec((ROW_TILE, width), lambda i: (src(2 * i + t), 0))
    in_specs = [row_spec(o.shape[1], t) for o in o_parts for t in range(2)]
    in_specs += [row_spec(d, 0), row_spec(d, 1)]
    in_specs += [pl.BlockSpec((None, 6, d), (lambda t: lambda i: (mrow(2 * i + t), 0, 0))(t)) for t in range(2)]
    in_specs += [_resident((1, d))]
    in_specs += [_resident(w.shape) for w in wo_parts]
    in_specs += [_resident(wi.shape), _resident(wf.shape)]
    o_args = [o for o in o_parts for _ in range(2)]
    return pl.pallas_call(
        functools.partial(_post_kernel, n_parts=n_parts),
        out_shape=jax.ShapeDtypeStruct((batch * out_tiles * ROW_TILE, d), F32),
        grid=(batch * out_tiles // 2,),
        in_specs=in_specs,
        out_specs=pl.BlockSpec((2 * ROW_TILE, d), lambda i: (i, 0)),
        compiler_params=_cparams(("parallel",)),
        name="post",
    )(*o_args, xz, xz, mod, mod, gain.reshape(1, d), *wo_parts, wi, wf)


def _attn_kernel(qc_ref, qa_ref, qb_ref, k_ref, v_ref, c_ref, sa_ref, sb_ref,
                 qg_ref, kg_ref, sg_ref, lam_ref, bd_ref, octx_ref, olat_ref, kb_ref, vt_ref,
                 *, lambda_init, seq_all):
    step = pl.program_id(2)

    def norm_rope(x, g, rows):
        ms = _dot((x * x).astype(BF16), bd_ref[...])
        y = x * lax.rsqrt(ms + EPS) * g
        return y * c_ref[rows, :] + pltpu.roll(y, LANES - 16, 1) * sa_ref[rows, :] + pltpu.roll(y, 16, 1) * sb_ref[rows, :]

    def tile_rows(t):
        return pl.ds(pl.multiple_of(t * Q_TILE, Q_TILE), Q_TILE)

    @pl.when(step == 0)
    def _prep_keys():
        def body(r, carry):
            rows = tile_rows(r)
            kb_ref[rows, :] = norm_rope(k_ref[rows, :], kg_ref[...], rows).astype(BF16)
            vt_ref[0:LANES, rows] = v_ref[rows, :].T.astype(BF16)
            return carry
        lax.fori_loop(0, seq_all // Q_TILE, body, 0)
        vt_ref[LANES:, :] = jnp.ones((SUM_ROWS, seq_all), BF16)

    lam_v = lam_ref[...]
    e1 = jnp.exp(jnp.sum(lam_v[0:1, :] * lam_v[1:2, :], axis=-1, keepdims=True))
    e2 = jnp.exp(jnp.sum(lam_v[2:3, :] * lam_v[3:4, :], axis=-1, keepdims=True))
    lam = e1 - e2 + lambda_init

    def scores(q_ref, tile, n_keys):
        qn = norm_rope(q_ref[...], qg_ref[...], tile_rows(tile)) * (A_DQK ** -0.5 * LOG2E)
        lane = lax.broadcasted_iota(jnp.int32, qn.shape, 1)
        q2 = jnp.concatenate([jnp.where(lane < A_DQK, qn, 0.0).astype(BF16),
                              jnp.where(lane >= A_DQK, qn, 0.0).astype(BF16)], axis=0)
        return _dot_nt(kb_ref[0:n_keys, :], q2)

    def outputs(s, n_keys, shift):
        if shift is None:
            shift = jnp.max(s, axis=0, keepdims=True)
        p = jnp.exp2(s - shift).astype(BF16)
        pv = _dot(vt_ref[:, 0:n_keys], p)
        pv = pv[0:LANES] / pv[LANES:LANES + 1]
        o = pv[:, :Q_TILE] - lam * pv[:, Q_TILE:]
        var = jnp.mean(o * o, axis=0, keepdims=True)
        o = o * lax.rsqrt(var + EPS) * (sg_ref[...] * (1.0 - lambda_init))
        return o.T.astype(BF16)

    def run(shift):
        tile_a = 1 + 2 * step
        s_a = scores(qa_ref, tile_a, seq_all)
        s_b = scores(qb_ref, tile_a + 1, seq_all)
        olat_ref[0:Q_TILE, :] = outputs(s_a, seq_all, shift)
        olat_ref[Q_TILE:, :] = outputs(s_b, seq_all, shift)

        @pl.when(step == 0)
        def _context_tile():
            octx_ref[...] = outputs(scores(qc_ref, 0, CTX_LEN), CTX_LEN, shift)

    bound = (A_DQK ** 0.5 * LOG2E) * (jnp.max(jnp.abs(qg_ref[...]), axis=-1, keepdims=True)
                                      * jnp.max(jnp.abs(kg_ref[...]), axis=-1, keepdims=True)) + 1.0
    small = jnp.max(bound) < SAFE_SCORE_LOG2
    pl.when(small)(lambda: run(bound))
    pl.when(jnp.logical_not(small))(lambda: run(None))


def _attn(p3, tabs, qk_gain, lam, sub_gain, bd, lambda_init):
    batch, seq_all, _ = p3.shape
    seq = seq_all - CTX_LEN
    assert seq % (2 * Q_TILE) == 0
    qg = jnp.tile(qk_gain[0], 2).reshape(1, LANES)
    kg = jnp.tile(qk_gain[1], 2).reshape(1, LANES)
    full = _resident((seq_all, LANES))
    kern = functools.partial(_attn_kernel, lambda_init=lambda_init, seq_all=seq_all)
    o_ctx, o_lat = pl.pallas_call(
        kern,
        out_shape=(jax.ShapeDtypeStruct((batch, CTX_LEN, A_HEADS * LANES), BF16),
                   jax.ShapeDtypeStruct((batch, seq, A_HEADS * LANES), BF16)),
        grid=(batch, A_HEADS, seq // (2 * Q_TILE)),
        in_specs=[
            pl.BlockSpec((None, Q_TILE, LANES), lambda b, h, i: (b, 0, h)),
            pl.BlockSpec((None, Q_TILE, LANES), lambda b, h, i: (b, 1 + 2 * i, h)),
            pl.BlockSpec((None, Q_TILE, LANES), lambda b, h, i: (b, 2 + 2 * i, h)),
            pl.BlockSpec((None, seq_all, LANES), lambda b, h, i: (b, 0, A_HEADS + h)),
            pl.BlockSpec((None, seq_all, LANES), lambda b, h, i: (b, 0, 2 * A_HEADS + h)),
            full, full, full,
            _resident((1, LANES)), _resident((1, LANES)), _resident((LANES, 1)),
            _resident((4, A_DQK)), _resident((LANES, LANES)),
        ],
        out_specs=(pl.BlockSpec((None, CTX_LEN, LANES), lambda b, h, i: (b, 0, h)),
                   pl.BlockSpec((None, 2 * Q_TILE, LANES), lambda b, h, i: (b, i, h))),
        scratch_shapes=[pltpu.VMEM((seq_all, LANES), BF16), pltpu.VMEM((LANES + SUM_ROWS, seq_all), BF16)],
        compiler_params=_cparams(("parallel", "parallel", "arbitrary")),
        name="attn",
    )(p3, p3, p3, p3, p3, *tabs, qg, kg, sub_gain.reshape(LANES, 1), lam, bd)
    return jnp.concatenate([o_ctx, o_lat], axis=1)


MATMUL_LEVELS = 2
LOCAL_BLOCK = 64
N_STAGED = 6
SAFE_LOG2 = 100.0


def _scan_levels():
    return int(math.log2(SCAN_CHUNK))


def _scan_consts(n_heads):
    c = SCAN_CHUNK
    t = np.arange(c)[:, None]
    u = np.arange(c)[None, :]
    blocks_f = [u <= t]
    blocks_b = [u >= t]
    for lvl in range(1, MATMUL_LEVELS + 1):
        h = 1 << (lvl - 1)
        m = (t // (2 * h)) * (2 * h) + h
        second = (t % (2 * h)) >= h
        blocks_f.append(np.where(second, (u >= m) & (u <= t), (u > t) & (u <= m - 1)))
        blocks_b.append(np.where(second, (u >= m) & (u < t), (u >= t) & (u <= m - 1)))
    a_f = np.tile(np.concatenate(blocks_f, axis=0).astype(np.float32), (1, 2))
    a_b = np.tile(np.concatenate(blocks_b, axis=0).astype(np.float32), (1, 2))
    x = t ^ u
    lv = np.where(x > 0, np.floor(np.log2(np.maximum(x, 1))).astype(np.int32) + 1, 0)
    lv_f = np.where(u <= t, lv, -1).astype(np.int32)
    lv_b = np.where(u >= t, lv, -1).astype(np.int32)
    lv_f = np.tile(lv_f, (n_heads, 1))
    lv_b = np.tile(lv_b, (n_heads, 1))
    return (jnp.asarray(a_f, BF16), jnp.asarray(a_b, BF16), jnp.asarray(lv_f), jnp.asarray(lv_b))


def _scan_sums(g, a):
    g2 = g * LOG2E
    g_hi = g2.astype(BF16)
    g_lo = (g2 - g_hi.astype(F32)).astype(BF16)
    return _dot(a, jnp.concatenate([g_hi, g_lo], axis=0))


def _local_exponent(b, backward):
    c = SCAN_CHUNK
    b3 = b.reshape(c // LOCAL_BLOCK, LOCAL_BLOCK, LANES)
    r0 = LOCAL_BLOCK // 2 if backward else LOCAL_BLOCK // 2 - 1
    return (b3 - b3[:, r0:r0 + 1, :]).reshape(c, LANES)


def _level_exponent(b, lvl, backward):
    c = SCAN_CHUNK
    h = 1 << (lvl - 1)
    r0 = h if backward else h - 1
    b3 = b.reshape(c // (2 * h), 2 * h, LANES)
    ref = b3[:, r0:r0 + 1, :]
    if h % 8 == 0:
        first, second = b3[:, :h, :], b3[:, h:, :]
        x = jnp.concatenate([first - ref, ref - second] if backward else [ref - first, second - ref], axis=1)
    else:
        x = -jnp.abs(b3 - ref)
    return x.reshape(c, LANES)


def _by_head(xb, n_heads):
    if n_heads == 1:
        return xb
    dk = LANES // n_heads
    lane = lax.broadcasted_iota(jnp.int32, xb.shape, 1)
    masks = [jnp.where((lane >= hh * dk) & (lane < (hh + 1) * dk), 1.0, 0.0).astype(BF16) for hh in range(n_heads)]
    return jnp.concatenate([xb * hm for hm in masks], axis=0)


def _head_block_mask(shape, row_block, col_block):
    r = lax.broadcasted_iota(jnp.int32, shape, 0) // row_block
    cc = lax.broadcasted_iota(jnp.int32, shape, 1) // col_block
    return r == cc


def _level_chunk(qb, kb, b, fine, vb, st, lv_ref, *, n_heads, backward):
    c = SCAN_CHUNK
    nl = _scan_levels()
    edge = b[0:1, :] if backward else b[c - 1:c, :]
    e_in = jnp.exp2(b).astype(BF16)
    e_out = jnp.exp2(edge - b).astype(BF16)

    def level_decay(lvl):
        if lvl <= MATMUL_LEVELS:
            return jnp.exp2(fine[(lvl - 1) * c:lvl * c]).astype(BF16)
        return jnp.exp2(_level_exponent(b, lvl, backward)).astype(BF16)

    o = _dot_nt(qb * e_in, st.astype(BF16))

    lv = lv_ref[...]
    att = jnp.zeros((n_heads * c, c), F32)
    for lvl in range(nl + 1):
        if lvl == 0:
            qh, kh = qb, kb
        else:
            el = level_decay(lvl)
            qh, kh = qb * el, kb * el
        att = jnp.where(lv == lvl, _dot_nt(_by_head(qh, n_heads), kh), att)
    attb = att.astype(BF16)
    intra = [_dot(attb[hh * c:(hh + 1) * c], vb[:, hh * LANES:(hh + 1) * LANES]) for hh in range(n_heads)]
    o = o + (intra[0] if n_heads == 1 else jnp.concatenate(intra, axis=1))

    dec = jnp.exp2(edge)
    upd = _dot_tn(vb, kb * e_out)
    if n_heads > 1:
        upd = jnp.where(_head_block_mask(upd.shape, LANES, LANES // n_heads), upd, 0.0)
    return o, dec * st + upd


def _shortcut_pairs(qp, kp, qt, kt, kd, vb, lv_ref, *, n_heads):
    lv = lv_ref[...]
    local = _dot_nt(_by_head(qp, n_heads), kp)
    top = _dot_nt(_by_head(qt, n_heads), kt)
    att = jnp.where(lv == _scan_levels(), top, jnp.where(lv >= 0, local, 0.0)).astype(BF16)
    upd = _dot_tn(kd, vb)
    if n_heads > 1:
        upd = jnp.where(_head_block_mask(upd.shape, LANES // n_heads, LANES), upd, 0.0)
    return att, upd


def _shortcut_output(att, upd, qe, vb, dec, s, *, n_heads):
    c = SCAN_CHUNK
    sb = s.astype(BF16)
    outs = []
    for hh in range(n_heads):
        cols = slice(hh * LANES, (hh + 1) * LANES)
        lhs = jnp.concatenate([att[hh * c:(hh + 1) * c], qe], axis=1)
        rhs = jnp.concatenate([vb[:, cols], sb[:, cols]], axis=0)
        outs.append(_dot(lhs, rhs))
    o = outs[0] if n_heads == 1 else jnp.concatenate(outs, axis=1)
    if n_heads > 1:
        dec = jnp.concatenate([dec] * n_heads, axis=1)
    return o, dec * s + upd


def _scan_loop(load_common, load_dir, out_refs, a_refs, lv_refs, v_s, stage_refs, *, n_heads, seq_all):
    c = SCAN_CHUNK
    n = seq_all // c
    n_ctx = CTX_LEN // c

    def chunk_rows(j):
        return pl.ds(pl.multiple_of(j * c, c), c)

    def scan(prepare, finish, state0, unroll):
        def body(i, sts):
            sts = list(sts)
            todo = []
            for u in range(unroll):
                j = i * unroll + u
                jb = jnp.where(j < n_ctx, n_ctx - 1 - j, n - 1 - (j - n_ctx))
                for d, jj in ((0, j), (1, jb)):
                    rows = chunk_rows(jj)
                    todo.append((d, rows, prepare(d, rows)))
            for d, rows, values in todo:
                out_refs[d][rows, :], sts[d] = finish(d, rows, values, sts[d])
            return tuple(sts)
        lax.fori_loop(0, n // unroll, body, (state0, state0))

    def stage(j, worst):
        gated = []
        for u in range(STAGE_UNROLL):
            rows = chunk_rows(j * STAGE_UNROLL + u)
            q, vb = load_common(rows)
            v_s[rows, :] = vb
            gated.append((rows, q, [load_dir[d](rows) for d in (0, 1)]))
        summed = [(rows, q, [(k, _scan_sums(g, a_refs[d][0:c, :])) for d, (k, g) in enumerate(kg)])
                  for rows, q, kg in gated]
        for rows, q, kb in summed:
            for d, (k, b) in enumerate(kb):
                x = _local_exponent(b, bool(d))
                e_top = jnp.exp2(_level_exponent(b, _scan_levels(), bool(d)))
                edge = b[0:1, :] if d else b[c - 1:c, :]
                qp_s, kp_s, qt_s, kt_s, qe_s, kd_s, dec_s = stage_refs[d]
                qp_s[rows, :] = (q * jnp.exp2(x)).astype(BF16)
                kp_s[rows, :] = (k * jnp.exp2(-x)).astype(BF16)
                qt_s[rows, :] = (q * e_top).astype(BF16)
                kt_s[rows, :] = (k * e_top).astype(BF16)
                qe_s[rows, :] = (q * jnp.exp2(b)).astype(BF16)
                kd_s[rows, :] = (k * jnp.exp2(edge - b)).astype(BF16)
                dec_s[rows, :] = jnp.broadcast_to(jnp.exp2(edge), (LANES, LANES)).T
                worst = jnp.maximum(worst, jnp.max(jnp.abs(x).reshape(c // 8, 8, LANES), axis=0))
        return worst
    worst = lax.fori_loop(0, n // STAGE_UNROLL, stage, jnp.zeros((8, LANES), F32))
    safe = jnp.max(worst) < SAFE_LOG2

    def shortcut_pairs(d, rows):
        qp_s, kp_s, qt_s, kt_s, _, kd_s, _ = stage_refs[d]
        return _shortcut_pairs(qp_s[rows, :], kp_s[rows, :], qt_s[rows, :], kt_s[rows, :], kd_s[rows, :],
                               v_s[rows, :], lv_refs[d], n_heads=n_heads)

    def shortcut_output(d, rows, values, s):
        _, _, _, _, qe_s, _, dec_s = stage_refs[d]
        return _shortcut_output(*values, qe_s[rows, :], v_s[rows, :], dec_s[rows, :], s, n_heads=n_heads)

    def levels(d, rows, values, st):
        q, vb = load_common(rows)
        k, g = load_dir[d](rows)
        sums = _scan_sums(g, a_refs[d][...])
        return _level_chunk(q.astype(BF16), k.astype(BF16), sums[0:c], sums[c:], vb, st, lv_refs[d],
                            n_heads=n_heads, backward=bool(d))

    pl.when(safe)(lambda: scan(shortcut_pairs, shortcut_output, jnp.zeros((LANES, n_heads * LANES), F32), SCAN_UNROLL))
    pl.when(jnp.logical_not(safe))(lambda: scan(lambda d, rows: None, levels, jnp.zeros((n_heads * LANES, LANES), F32), 1))


def _scan_scratch(seq_all, n_heads):
    assert SCAN_CHUNK == LANES == 2 * LOCAL_BLOCK
    width = n_heads * LANES
    out = [pltpu.VMEM((seq_all, width), F32)] * 2
    stage = [pltpu.VMEM((seq_all, LANES), BF16)] * N_STAGED + [pltpu.VMEM((seq_all, LANES), F32)]
    return out + [pltpu.VMEM((seq_all, width), BF16)] + stage * 2


def _log_sigmoid(x):
    return jnp.minimum(x, 0.0) - jnp.log(1.0 + jnp.exp(-jnp.abs(x)))


def _gla_kernel(q_ref, k_ref, v_ref, g_ref, lr_ref, wgf_ref, wgb_ref, bf_ref, bb_ref, gain_ref,
                af_ref, ab_ref, lvf_ref, lvb_ref, o_ref, of_ref, ob_ref, v_s, *stage_refs, seq_all):
    def load_common(rows):
        return q_ref[rows, :] * (B_DK ** -0.5), v_ref[rows, :].astype(BF16)

    def load_dir(wg_ref, b_ref):
        def load(rows):
            pre = _dot(lr_ref[rows, :].astype(BF16), wg_ref[...]) + b_ref[...]
            return k_ref[rows, :], _log_sigmoid(pre) * (1.0 / B_GATE_NORM)
        return load

    _scan_loop(load_common, (load_dir(wgf_ref, bf_ref), load_dir(wgb_ref, bb_ref)), (of_ref, ob_ref),
               (af_ref, ab_ref), (lvf_ref, lvb_ref), v_s, (stage_refs[:N_STAGED + 1], stage_refs[N_STAGED + 1:]), n_heads=2, seq_all=seq_all)

    def post(j, carry):
        rows = pl.ds(pl.multiple_of(j * ROW_TILE, ROW_TILE), ROW_TILE)
        for hh in range(2):
            cols = slice(hh * LANES, (hh + 1) * LANES)
            o = of_ref[rows, cols] + ob_ref[rows, cols]
            var = jnp.mean(o * o, axis=-1, keepdims=True)
            o_ref[rows, cols] = (o * lax.rsqrt(var + EPS) * gain_ref[...] * _silu(g_ref[rows, cols])).astype(BF16)
        return carry
    lax.fori_loop(0, seq_all // ROW_TILE, post, 0, unroll=3)


def _gla(p3, w_gate_up, b_gate_up, gain, consts):
    batch, seq_all, _ = p3.shape
    pairs = B_HEADS // 2
    wg = jnp.zeros((2, pairs, LANES, LANES), F32)
    for d in range(2):
        w = w_gate_up[d].reshape(B_GATE_RANK, pairs, LANES).transpose(1, 0, 2)
        wg = wg.at[d, :, d * B_GATE_RANK:(d + 1) * B_GATE_RANK, :].set(w)
    wg = wg.astype(BF16)
    bias = b_gate_up.reshape(2, pairs, 1, LANES)
    a_f, a_b, lv_f, lv_b = consts
    col = lambda off: (lambda b, p: (b, 0, off + p))
    q_off = 3 * A_HEADS * LANES // LANES
    k_off = q_off + B_HEADS * B_DK // LANES
    v_off = (k_off * LANES + B_HEADS * B_DK) // (2 * LANES)
    g_off = v_off + pairs
    lr_off = (g_off + pairs) * 2
    wspec = pl.BlockSpec((None, LANES, LANES), lambda b, p: (p, 0, 0))
    bspec = pl.BlockSpec((None, 1, LANES), lambda b, p: (p, 0, 0))
    return pl.pallas_call(
        functools.partial(_gla_kernel, seq_all=seq_all),
        out_shape=jax.ShapeDtypeStruct((batch, seq_all, B_HEADS * LANES), BF16),
        grid=(batch, pairs),
        in_specs=[
            pl.BlockSpec((None, seq_all, LANES), col(q_off)),
            pl.BlockSpec((None, seq_all, LANES), col(k_off)),
            pl.BlockSpec((None, seq_all, 2 * LANES), col(v_off)),
            pl.BlockSpec((None, seq_all, 2 * LANES), col(g_off)),
            pl.BlockSpec((None, seq_all, LANES), lambda b, p: (b, 0, lr_off)),
            wspec, wspec, bspec, bspec,
            _resident((1, LANES)),
            _resident(a_f.shape), _resident(a_b.shape), _resident(lv_f.shape), _resident(lv_b.shape),
        ],
        out_specs=pl.BlockSpec((None, seq_all, 2 * LANES), lambda b, p: (b, 0, p)),
        scratch_shapes=_scan_scratch(seq_all, 2),
        compiler_params=_cparams(("parallel", "parallel")),
        name="gla",
    )(p3, p3, p3, p3, p3, wg[0], wg[1], bias[0], bias[1], gain.reshape(1, LANES), a_f, a_b, lv_f, lv_b)


def _hgrn_kernel(q_ref, ff_ref, fb_ref, i_ref, g_ref, lb_ref, gain_ref,
                 af_ref, ab_ref, lvf_ref, lvb_ref, o_ref, of_ref, ob_ref, v_s, *stage_refs, seq_all, layer):
    def lower_bound(d):
        raw = lb_ref[d]
        ex = jnp.exp(raw - jnp.max(raw, axis=0, keepdims=True))
        p = ex / jnp.sum(ex, axis=0, keepdims=True)
        return jnp.sum(p[1:layer + 1], axis=0, keepdims=True)

    def load_common(rows):
        return _silu(q_ref[rows, :]) * (C_DH ** -0.5), i_ref[rows, :].astype(BF16)

    def load_dir(f_ref, lb):
        def load(rows):
            f = lb + (1.0 - lb) * jax.nn.sigmoid(f_ref[rows, :])
            return 1.0 - f, jnp.log(f)
        return load

    _scan_loop(load_common, (load_dir(ff_ref, lower_bound(0)), load_dir(fb_ref, lower_bound(1))), (of_ref, ob_ref),
               (af_ref, ab_ref), (lvf_ref, lvb_ref), v_s, (stage_refs[:N_STAGED + 1], stage_refs[N_STAGED + 1:]), n_heads=1, seq_all=seq_all)

    def post(j, carry):
        rows = pl.ds(pl.multiple_of(j * ROW_TILE, ROW_TILE), ROW_TILE)
        o = of_ref[rows, :] + ob_ref[rows, :]
        var = jnp.mean(o * o, axis=-1, keepdims=True)
        o_ref[rows, :] = (o * lax.rsqrt(var + EPS) * gain_ref[...] * _silu(g_ref[rows, :])).astype(BF16)
        return carry
    lax.fori_loop(0, seq_all // ROW_TILE, post, 0, unroll=3)


def _hgrn(p3, lb_raw, gain, consts, layer):
    batch, seq_all, _ = p3.shape
    a_f, a_b, lv_f, lv_b = consts
    col = lambda off: (lambda b, h: (b, 0, off + h))
    return pl.pallas_call(
        functools.partial(_hgrn_kernel, seq_all=seq_all, layer=layer),
        out_shape=jax.ShapeDtypeStruct((batch, seq_all, C_HEADS * LANES), BF16),
        grid=(batch, C_HEADS),
        in_specs=[pl.BlockSpec((None, seq_all, LANES), col(s * C_HEADS)) for s in range(5)] + [
            pl.BlockSpec((2, DEPTH, LANES), lambda b, h: (0, 0, h)),
            _resident((1, LANES)),
            _resident(a_f.shape), _resident(a_b.shape), _resident(lv_f.shape), _resident(lv_b.shape),
        ],
        out_specs=pl.BlockSpec((None, seq_all, LANES), lambda b, h: (b, 0, h)),
        scratch_shapes=_scan_scratch(seq_all, 1),
        compiler_params=_cparams(("parallel", "parallel")),
        name="hgrn",
    )(p3, p3, p3, p3, p3, lb_raw, gain.reshape(1, LANES), a_f, a_b, lv_f, lv_b)


def _rope_tables(seq):
    pos = jnp.arange(seq)
    row_ids = (pos // GRID_W).astype(F32)
    col_ids = (pos % GRID_W).astype(F32)
    n_axis = A_DQK // 2
    inv = ROPE_BASE ** (-jnp.arange(0, n_axis, 2, dtype=F32) / n_axis)
    ang_r = row_ids[:, None] * inv
    ang_c = col_ids[:, None] * inv
    zeros = jnp.zeros_like(ang_r)
    cos64 = jnp.concatenate([jnp.cos(ang_r)] * 2 + [jnp.cos(ang_c)] * 2, axis=-1)
    sa64 = jnp.concatenate([-jnp.sin(ang_r), zeros, -jnp.sin(ang_c), zeros], axis=-1)
    sb64 = jnp.concatenate([zeros, jnp.sin(ang_r), zeros, jnp.sin(ang_c)], axis=-1)
    ident = lambda v: jnp.full((CTX_LEN, LANES), v, F32)
    full = lambda t, v: jnp.concatenate([ident(v), jnp.tile(t, (1, 2))], axis=0)
    return full(cos64, 1.0), full(sa64, 0.0), full(sb64, 0.0)


def kernel(x, c, ctx, c_ctx, w_ada, b_ada, norm1_gain, norm2_gain, w_in_even, qk_gain_a, lambda_a, subln_gain_a, w_gate_up_b, b_gate_up_b, onorm_gain_b, w_out_even, w_in_odd, lb_raw_c, onorm_gain_c, w_out_odd, w_ffn_in, w_ffn_out):
    batch, seq, d = x.shape
    seq_all = CTX_LEN + seq
    assert d == D_MODEL and ctx.shape[1] == CTX_LEN == ROW_TILE == Q_TILE and seq % ROW_TILE == 0

    xz = jnp.concatenate([ctx, x], axis=1).reshape(batch * seq_all, d)
    mod_rows = -(-(batch + 1) // 8) * 8
    cc = jnp.zeros((mod_rows, d), F32).at[:batch].set(c).at[batch].set(c_ctx)
    mods = _ada(cc, w_ada, b_ada).reshape(DEPTH, mod_rows, 6, d)

    tabs = _rope_tables(seq)
    half = np.arange(LANES) // A_DQK
    bd = jnp.asarray((half[:, None] == half[None, :]) / A_DQK, BF16)
    consts_gla = _scan_consts(2)
    consts_hgrn = _scan_consts(1)

    for l in range(DEPTH):
        j = l // 2
        last = l == DEPTH - 1
        wi = w_ffn_in[l].astype(BF16)
        wf = w_ffn_out[l].astype(BF16)
        if l % 2 == 0:
            w = jnp.pad(w_in_even[j], ((0, 0), (0, EVEN_COLS_PAD - EVEN_COLS))).astype(BF16)
            p3 = _inproj(xz, mods[l], norm1_gain[l], w, batch).reshape(batch, seq_all, EVEN_COLS_PAD)
            lambda_init = 0.8 - 0.6 * math.exp(-0.3 * l)
            oa = _attn(p3, tabs, qk_gain_a[j], lambda_a[j], subln_gain_a[j], bd, lambda_init)
            ob = _gla(p3, w_gate_up_b[j], b_gate_up_b[j], onorm_gain_b[j], consts_gla)
            a_width = A_HEADS * LANES
            wo = w_out_even[j].astype(BF16)
            o_parts = [oa.reshape(batch * seq_all, -1), ob.reshape(batch * seq_all, -1)]
            wo_parts = [wo[:a_width], wo[a_width:]]
        else:
            p3 = _inproj(xz, mods[l], norm1_gain[l], w_in_odd[j].astype(BF16), batch).reshape(batch, seq_all, ODD_COLS)
            oc = _hgrn(p3, lb_raw_c, onorm_gain_c[j], consts_hgrn, l)
            o_parts = [oc.reshape(batch * seq_all, -1)]
            wo_parts = [w_out_odd[j].astype(BF16)]
        xz = _post(o_parts, xz, mods[l], norm2_gain[l], wo_parts, wi, wf, batch, last)
    return xz.reshape(batch, seq, d)
```

```python
import functools
import math

import numpy as np
import jax
import jax.numpy as jnp
from jax import lax
from jax.experimental import pallas as pl
from jax.experimental.pallas import tpu as pltpu

F32 = jnp.float32
BF16 = jnp.bfloat16

D_MODEL = 1024
DEPTH = 4
CTX_LEN = 256
GRID_W = 64
A_HEADS = 4
A_DQK = 64
B_HEADS = 4
B_DK = 64
B_GATE_RANK = 16
B_GATE_NORM = 16.0
C_HEADS = 8
C_DH = 128
ROPE_BASE = 10000.0
EPS = 1e-6
D_FF = 2816
EVEN_COLS = 3104
EVEN_COLS_PAD = 3200
ODD_COLS = 5120

LANES = 128
ROW_TILE = 256
COL_CHUNK = 512
FF_CHUNK = 256
Q_TILE = 256
Q_TILES_PER_STEP = 4
SUM_ROWS = 16
SAFE_SCORE_LOG2 = 60.0
LOG2E = math.log2(math.e)
SCAN_CHUNK = 128
SCAN_UNROLL = 6
STAGE_UNROLL = 6
VMEM_LIMIT = 52 * 1024 * 1024


def _cparams(sem):
    return pltpu.CompilerParams(dimension_semantics=sem, vmem_limit_bytes=VMEM_LIMIT)


def _resident(shape):
    nd = len(shape)
    return pl.BlockSpec(shape, lambda *_: (0,) * nd, pipeline_mode=pl.Buffered(1))


def _silu(x):
    return x * jax.nn.sigmoid(x)


def _dot(a, b):
    return jnp.dot(a, b, preferred_element_type=F32)


def _dot_nt(a, b):
    return lax.dot_general(a, b, (((1,), (1,)), ((), ())), preferred_element_type=F32)


def _dot_tn(a, b):
    return lax.dot_general(a, b, (((0,), (0,)), ((), ())), preferred_element_type=F32)


def _ada_kernel(c_ref, w_ref, b_ref, o_ref):
    s = _silu(c_ref[...]).astype(BF16)
    o_ref[...] = _dot(s, w_ref[...].astype(BF16)) + b_ref[...]


def _ada(cc, w_ada, b_ada):
    depth, d, n = w_ada.shape
    rows = cc.shape[0]
    tn = 1536
    return pl.pallas_call(
        _ada_kernel,
        out_shape=jax.ShapeDtypeStruct((depth, rows, n), F32),
        grid=(depth, n // tn),
        in_specs=[
            pl.BlockSpec((rows, d), lambda l, j: (0, 0)),
            pl.BlockSpec((None, d, tn), lambda l, j: (l, 0, j)),
            pl.BlockSpec((None, 1, tn), lambda l, j: (l, 0, j)),
        ],
        out_specs=pl.BlockSpec((None, rows, tn), lambda l, j: (l, 0, j)),
        compiler_params=_cparams(("parallel", "parallel")),
        name="ada",
    )(cc, w_ada, b_ada.reshape(depth, 1, n))


def _norm_mod(x, gain, shift, scale):
    var = jnp.mean(x * x, axis=-1, keepdims=True)
    return (x * lax.rsqrt(var + EPS) * gain) * (1.0 + scale) + shift


def _inproj_kernel(x_ref, mod_a_ref, mod_b_ref, gain_ref, w_ref, o_ref):
    halves = [_norm_mod(x_ref[t * ROW_TILE:(t + 1) * ROW_TILE, :], gain_ref[...], m[0:1, :], m[1:2, :]).astype(BF16)
              for t, m in enumerate((mod_a_ref, mod_b_ref))]
    h = jnp.concatenate(halves, axis=0)
    ncols = w_ref.shape[1]
    for c0 in range(0, ncols, COL_CHUNK):
        c1 = min(c0 + COL_CHUNK, ncols)
        o_ref[:, c0:c1] = _dot(h, w_ref[:, c0:c1])


def _tiles_per_sample(seq_all):
    return seq_all // ROW_TILE


def _mod_row(i, tiles, batch):
    return jnp.where(i % tiles == 0, batch, i // tiles)


def _inproj(xz, mod, gain, w, batch):
    rows, d = xz.shape
    ncols = w.shape[1]
    tiles = _tiles_per_sample(rows // batch)
    mod_spec = lambda t: pl.BlockSpec((None, 6, d), lambda i: (_mod_row(2 * i + t, tiles, batch), 0, 0))
    return pl.pallas_call(
        _inproj_kernel,
        out_shape=jax.ShapeDtypeStruct((rows, ncols), F32),
        grid=(rows // (2 * ROW_TILE),),
        in_specs=[
            pl.BlockSpec((2 * ROW_TILE, d), lambda i: (i, 0)),
            mod_spec(0), mod_spec(1),
            _resident((1, d)),
            _resident((d, ncols)),
        ],
        out_specs=pl.BlockSpec((2 * ROW_TILE, ncols), lambda i: (i, 0)),
        compiler_params=_cparams(("parallel",)),
        name="inproj",
    )(xz, mod, mod, gain.reshape(1, d), w)


def _post_kernel(*refs, n_parts):
    o_refs = [refs[2 * p:2 * p + 2] for p in range(n_parts)]
    rest = refs[2 * n_parts:]
    x_refs, mod_refs, gain_ref = rest[0:2], rest[2:4], rest[4]
    wo_refs = rest[5:5 + n_parts]
    wi_ref, wf_ref, out_ref = rest[5 + n_parts:]
    y = None
    for (oa_ref, ob_ref), wo_ref in zip(o_refs, wo_refs):
        part = _dot(jnp.concatenate([oa_ref[...], ob_ref[...]], axis=0), wo_ref[...])
        y = part if y is None else y + part
    x1 = [x_refs[t][...] + mod_refs[t][2:3, :] * y[t * ROW_TILE:(t + 1) * ROW_TILE] for t in range(2)]
    h = jnp.concatenate([_norm_mod(x1[t], gain_ref[...], mod_refs[t][3:4, :], mod_refs[t][4:5, :]).astype(BF16)
                         for t in range(2)], axis=0)
    acc = None
    for c0 in range(0, D_FF, FF_CHUNK):
        gate = _dot(h, wi_ref[:, c0:c0 + FF_CHUNK])
        up = _dot(h, wi_ref[:, D_FF + c0:D_FF + c0 + FF_CHUNK])
        a = (_silu(gate) * up).astype(BF16)
        part = _dot(a, wf_ref[c0:c0 + FF_CHUNK, :])
        acc = part if acc is None else acc + part
    for t in range(2):
        rows = slice(t * ROW_TILE, (t + 1) * ROW_TILE)
        out_ref[rows, :] = x1[t] + mod_refs[t][5:6, :] * acc[rows]


def _post(o_parts, xz, mod, gain, wo_parts, wi, wf, batch, latent_only):
    rows, d = xz.shape
    tiles = _tiles_per_sample(rows // batch)
    if latent_only:
        out_tiles = tiles - 1
        src = lambda i: (i // out_tiles) * tiles + 1 + i % out_tiles
        mrow = lambda i: i // out_tiles
    else:
        out_tiles = tiles
        src = lambda i: i
        mrow = lambda i: _mod_row(i, tiles, batch)
    assert (batch * out_tiles) % 2 == 0
    n_parts = len(o_parts)
    row_spec = lambda width, t: pl.BlockSpec((ROW_TILE, width), lambda i: (src(2 * i + t), 0))
    in_specs = [row_spec(o.shape[1], t) for o in o_parts for t in range(2)]
    in_specs += [row_spec(d, 0), row_spec(d, 1)]
    in_specs += [pl.BlockSpec((None, 6, d), (lambda t: lambda i: (mrow(2 * i + t), 0, 0))(t)) for t in range(2)]
    in_specs += [_resident((1, d))]
    in_specs += [_resident(w.shape) for w in wo_parts]
    in_specs += [_resident(wi.shape), _resident(wf.shape)]
    o_args = [o for o in o_parts for _ in range(2)]
    return pl.pallas_call(
        functools.partial(_post_kernel, n_parts=n_parts),
        out_shape=jax.ShapeDtypeStruct((batch * out_tiles * ROW_TILE, d), F32),
        grid=(batch * out_tiles // 2,),
        in_specs=in_specs,
        out_specs=pl.BlockSpec((2 * ROW_TILE, d), lambda i: (i, 0)),
        compiler_params=_cparams(("parallel",)),
        name="post",
    )(*o_args, xz, xz, mod, mod, gain.reshape(1, d), *wo_parts, wi, wf)


def _attn_kernel(qc_ref, *refs, lambda_init, seq_all):
    q_refs = refs[:Q_TILES_PER_STEP]
    (k_ref, v_ref, c_ref, sa_ref, sb_ref, qg_ref, kg_ref, sg_ref, lam_ref, bd_ref,
     octx_ref, olat_ref, kb_ref, vt_ref) = refs[Q_TILES_PER_STEP:]
    step = pl.program_id(2)

    def norm_rope(x, g, rows):
        ms = _dot((x * x).astype(BF16), bd_ref[...])
        y = x * lax.rsqrt(ms + EPS) * g
        return y * c_ref[rows, :] + pltpu.roll(y, LANES - 16, 1) * sa_ref[rows, :] + pltpu.roll(y, 16, 1) * sb_ref[rows, :]

    def tile_rows(t):
        return pl.ds(pl.multiple_of(t * Q_TILE, Q_TILE), Q_TILE)

    @pl.when(step == 0)
    def _prep_keys():
        def body(r, carry):
            rows = tile_rows(r)
            kb_ref[rows, :] = norm_rope(k_ref[rows, :], kg_ref[...], rows).astype(BF16)
            vt_ref[0:LANES, rows] = v_ref[rows, :].T.astype(BF16)
            return carry
        lax.fori_loop(0, seq_all // Q_TILE, body, 0, unroll=3)
        vt_ref[LANES:, :] = jnp.ones((SUM_ROWS, seq_all), BF16)

    lam_v = lam_ref[...]
    e1 = jnp.exp(jnp.sum(lam_v[0:1, :] * lam_v[1:2, :], axis=-1, keepdims=True))
    e2 = jnp.exp(jnp.sum(lam_v[2:3, :] * lam_v[3:4, :], axis=-1, keepdims=True))
    lam = e1 - e2 + lambda_init

    def scores(q_ref, tile, n_keys):
        qn = norm_rope(q_ref[...], qg_ref[...], tile_rows(tile)) * (A_DQK ** -0.5 * LOG2E)
        lane = lax.broadcasted_iota(jnp.int32, qn.shape, 1)
        q2 = jnp.concatenate([jnp.where(lane < A_DQK, qn, 0.0).astype(BF16),
                              jnp.where(lane >= A_DQK, qn, 0.0).astype(BF16)], axis=0)
        return _dot_nt(kb_ref[0:n_keys, :], q2)

    def outputs(s, n_keys, shift):
        if shift is None:
            shift = jnp.max(s, axis=0, keepdims=True)
        p = jnp.exp2(s - shift).astype(BF16)
        pv = _dot(vt_ref[:, 0:n_keys], p)
        pv = pv[0:LANES] / pv[LANES:LANES + 1]
        o = pv[:, :Q_TILE] - lam * pv[:, Q_TILE:]
        var = jnp.mean(o * o, axis=0, keepdims=True)
        o = o * lax.rsqrt(var + EPS) * (sg_ref[...] * (1.0 - lambda_init))
        return o.T.astype(BF16)

    def run(shift):
        first = 1 + Q_TILES_PER_STEP * step
        all_scores = [scores(q_ref, first + t, seq_all) for t, q_ref in enumerate(q_refs)]
        for t, s in enumerate(all_scores):
            olat_ref[t * Q_TILE:(t + 1) * Q_TILE, :] = outputs(s, seq_all, shift)

        @pl.when(step == 0)
        def _context_tile():
            octx_ref[...] = outputs(scores(qc_ref, 0, CTX_LEN), CTX_LEN, shift)

    bound = (A_DQK ** 0.5 * LOG2E) * (jnp.max(jnp.abs(qg_ref[...]), axis=-1, keepdims=True)
                                      * jnp.max(jnp.abs(kg_ref[...]), axis=-1, keepdims=True)) + 1.0
    small = jnp.max(bound) < SAFE_SCORE_LOG2
    pl.when(small)(lambda: run(bound))
    pl.when(jnp.logical_not(small))(lambda: run(None))


def _attn(p3, tabs, qk_gain, lam, sub_gain, bd, lambda_init):
    batch, seq_all, _ = p3.shape
    seq = seq_all - CTX_LEN
    n_t = Q_TILES_PER_STEP
    assert seq % (n_t * Q_TILE) == 0
    qg = jnp.tile(qk_gain[0], 2).reshape(1, LANES)
    kg = jnp.tile(qk_gain[1], 2).reshape(1, LANES)
    full = _resident((seq_all, LANES))
    kern = functools.partial(_attn_kernel, lambda_init=lambda_init, seq_all=seq_all)
    q_spec = lambda t: pl.BlockSpec((None, Q_TILE, LANES), lambda b, h, i: (b, 1 + n_t * i + t, h))
    o_ctx, o_lat = pl.pallas_call(
        kern,
        out_shape=(jax.ShapeDtypeStruct((batch, CTX_LEN, A_HEADS * LANES), BF16),
                   jax.ShapeDtypeStruct((batch, seq, A_HEADS * LANES), BF16)),
        grid=(batch, A_HEADS, seq // (n_t * Q_TILE)),
        in_specs=[pl.BlockSpec((None, Q_TILE, LANES), lambda b, h, i: (b, 0, h))]
        + [q_spec(t) for t in range(n_t)] + [
            pl.BlockSpec((None, seq_all, LANES), lambda b, h, i: (b, 0, A_HEADS + h)),
            pl.BlockSpec((None, seq_all, LANES), lambda b, h, i: (b, 0, 2 * A_HEADS + h)),
            full, full, full,
            _resident((1, LANES)), _resident((1, LANES)), _resident((LANES, 1)),
            _resident((4, A_DQK)), _resident((LANES, LANES)),
        ],
        out_specs=(pl.BlockSpec((None, CTX_LEN, LANES), lambda b, h, i: (b, 0, h)),
                   pl.BlockSpec((None, n_t * Q_TILE, LANES), lambda b, h, i: (b, i, h))),
        scratch_shapes=[pltpu.VMEM((seq_all, LANES), BF16), pltpu.VMEM((LANES + SUM_ROWS, seq_all), BF16)],
        compiler_params=_cparams(("parallel", "parallel", "arbitrary")),
        name="attn",
    )(*([p3] * (n_t + 3)), *tabs, qg, kg, sub_gain.reshape(LANES, 1), lam, bd)
    return jnp.concatenate([o_ctx, o_lat], axis=1)


MATMUL_LEVELS = 2
LOCAL_BLOCK = 64
N_STAGED = 6
SAFE_LOG2 = 100.0


def _scan_levels():
    return int(math.log2(SCAN_CHUNK))


def _scan_consts(n_heads):
    c = SCAN_CHUNK
    t = np.arange(c)[:, None]
    u = np.arange(c)[None, :]
    blocks_f = [u <= t]
    blocks_b = [u >= t]
    for lvl in range(1, MATMUL_LEVELS + 1):
        h = 1 << (lvl - 1)
        m = (t // (2 * h)) * (2 * h) + h
        second = (t % (2 * h)) >= h
        blocks_f.append(np.where(second, (u >= m) & (u <= t), (u > t) & (u <= m - 1)))
        blocks_b.append(np.where(second, (u >= m) & (u < t), (u >= t) & (u <= m - 1)))
    a_f = np.tile(np.concatenate(blocks_f, axis=0).astype(np.float32), (1, 2))
    a_b = np.tile(np.concatenate(blocks_b, axis=0).astype(np.float32), (1, 2))
    x = t ^ u
    lv = np.where(x > 0, np.floor(np.log2(np.maximum(x, 1))).astype(np.int32) + 1, 0)
    lv_f = np.where(u <= t, lv, -1).astype(np.int32)
    lv_b = np.where(u >= t, lv, -1).astype(np.int32)
    lv_f = np.tile(lv_f, (n_heads, 1))
    lv_b = np.tile(lv_b, (n_heads, 1))
    return (jnp.asarray(a_f, BF16), jnp.asarray(a_b, BF16), jnp.asarray(lv_f), jnp.asarray(lv_b))


def _scan_sums(g, a):
    g2 = g * LOG2E
    g_hi = g2.astype(BF16)
    g_lo = (g2 - g_hi.astype(F32)).astype(BF16)
    return _dot(a, jnp.concatenate([g_hi, g_lo], axis=0))


def _local_exponent(b, backward):
    c = SCAN_CHUNK
    b3 = b.reshape(c // LOCAL_BLOCK, LOCAL_BLOCK, LANES)
    r0 = LOCAL_BLOCK // 2 if backward else LOCAL_BLOCK // 2 - 1
    return (b3 - b3[:, r0:r0 + 1, :]).reshape(c, LANES)


def _level_exponent(b, lvl, backward):
    c = SCAN_CHUNK
    h = 1 << (lvl - 1)
    r0 = h if backward else h - 1
    b3 = b.reshape(c // (2 * h), 2 * h, LANES)
    ref = b3[:, r0:r0 + 1, :]
    if h % 8 == 0:
        first, second = b3[:, :h, :], b3[:, h:, :]
        x = jnp.concatenate([first - ref, ref - second] if backward else [ref - first, second - ref], axis=1)
    else:
        x = -jnp.abs(b3 - ref)
    return x.reshape(c, LANES)


def _by_head(xb, n_heads):
    if n_heads == 1:
        return xb
    dk = LANES // n_heads
    lane = lax.broadcasted_iota(jnp.int32, xb.shape, 1)
    masks = [jnp.where((lane >= hh * dk) & (lane < (hh + 1) * dk), 1.0, 0.0).astype(BF16) for hh in range(n_heads)]
    return jnp.concatenate([xb * hm for hm in masks], axis=0)


def _head_block_mask(shape, row_block, col_block):
    r = lax.broadcasted_iota(jnp.int32, shape, 0) // row_block
    cc = lax.broadcasted_iota(jnp.int32, shape, 1) // col_block
    return r == cc


def _level_chunk(qb, kb, b, fine, vb, st, lv_ref, *, n_heads, backward):
    c = SCAN_CHUNK
    nl = _scan_levels()
    edge = b[0:1, :] if backward else b[c - 1:c, :]
    e_in = jnp.exp2(b).astype(BF16)
    e_out = jnp.exp2(edge - b).astype(BF16)

    def level_decay(lvl):
        if lvl <= MATMUL_LEVELS:
            return jnp.exp2(fine[(lvl - 1) * c:lvl * c]).astype(BF16)
        return jnp.exp2(_level_exponent(b, lvl, backward)).astype(BF16)

    o = _dot_nt(qb * e_in, st.astype(BF16))

    lv = lv_ref[...]
    att = jnp.zeros((n_heads * c, c), F32)
    for lvl in range(nl + 1):
        if lvl == 0:
            qh, kh = qb, kb
        else:
            el = level_decay(lvl)
            qh, kh = qb * el, kb * el
        att = jnp.where(lv == lvl, _dot_nt(_by_head(qh, n_heads), kh), att)
    attb = att.astype(BF16)
    intra = [_dot(attb[hh * c:(hh + 1) * c], vb[:, hh * LANES:(hh + 1) * LANES]) for hh in range(n_heads)]
    o = o + (intra[0] if n_heads == 1 else jnp.concatenate(intra, axis=1))

    dec = jnp.exp2(edge)
    upd = _dot_tn(vb, kb * e_out)
    if n_heads > 1:
        upd = jnp.where(_head_block_mask(upd.shape, LANES, LANES // n_heads), upd, 0.0)
    return o, dec * st + upd


def _shortcut_pairs(qp, kp, qt, kt, kd, vb, lv_ref, *, n_heads):
    lv = lv_ref[...]
    local = _dot_nt(_by_head(qp, n_heads), kp)
    top = _dot_nt(_by_head(qt, n_heads), kt)
    att = jnp.where(lv == _scan_levels(), top, jnp.where(lv >= 0, local, 0.0)).astype(BF16)
    upd = _dot_tn(kd, vb)
    if n_heads > 1:
        upd = jnp.where(_head_block_mask(upd.shape, LANES // n_heads, LANES), upd, 0.0)
    return att, upd


def _shortcut_output(att, upd, qe, vb, dec, s, *, n_heads):
    c = SCAN_CHUNK
    sb = s.astype(BF16)
    outs = []
    for hh in range(n_heads):
        cols = slice(hh * LANES, (hh + 1) * LANES)
        lhs = jnp.concatenate([att[hh * c:(hh + 1) * c], qe], axis=1)
        rhs = jnp.concatenate([vb[:, cols], sb[:, cols]], axis=0)
        outs.append(_dot(lhs, rhs))
    o = outs[0] if n_heads == 1 else jnp.concatenate(outs, axis=1)
    if n_heads > 1:
        dec = jnp.concatenate([dec] * n_heads, axis=1)
    return o, dec * s + upd


def _scan_loop(load_common, load_dir, out_refs, a_refs, lv_refs, v_s, stage_refs, *, n_heads, seq_all):
    c = SCAN_CHUNK
    n = seq_all // c
    n_ctx = CTX_LEN // c

    def chunk_rows(j):
        return pl.ds(pl.multiple_of(j * c, c), c)

    def scan(prepare, finish, state0, unroll):
        def body(i, sts):
            sts = list(sts)
            todo = []
            for u in range(unroll):
                j = i * unroll + u
                jb = jnp.where(j < n_ctx, n_ctx - 1 - j, n - 1 - (j - n_ctx))
                for d, jj in ((0, j), (1, jb)):
                    rows = chunk_rows(jj)
                    todo.append((d, rows, prepare(d, rows)))
            for d, rows, values in todo:
                out_refs[d][rows, :], sts[d] = finish(d, rows, values, sts[d])
            return tuple(sts)
        lax.fori_loop(0, n // unroll, body, (state0, state0))

    def stage(j, worst):
        gated = []
        for u in range(STAGE_UNROLL):
            rows = chunk_rows(j * STAGE_UNROLL + u)
            q, vb = load_common(rows)
            v_s[rows, :] = vb
            gated.append((rows, q, [load_dir[d](rows) for d in (0, 1)]))
        summed = [(rows, q, [(k, _scan_sums(g, a_refs[d][0:c, :])) for d, (k, g) in enumerate(kg)])
                  for rows, q, kg in gated]
        for rows, q, kb in summed:
            for d, (k, b) in enumerate(kb):
                x = _local_exponent(b, bool(d))
                e_top = jnp.exp2(_level_exponent(b, _scan_levels(), bool(d)))
                edge = b[0:1, :] if d else b[c - 1:c, :]
                qp_s, kp_s, qt_s, kt_s, qe_s, kd_s, dec_s = stage_refs[d]
                qp_s[rows, :] = (q * jnp.exp2(x)).astype(BF16)
                kp_s[rows, :] = (k * jnp.exp2(-x)).astype(BF16)
                qt_s[rows, :] = (q * e_top).astype(BF16)
                kt_s[rows, :] = (k * e_top).astype(BF16)
                qe_s[rows, :] = (q * jnp.exp2(b)).astype(BF16)
                kd_s[rows, :] = (k * jnp.exp2(edge - b)).astype(BF16)
                dec_s[rows, :] = jnp.broadcast_to(jnp.exp2(edge), (LANES, LANES)).T
                worst = jnp.maximum(worst, jnp.max(jnp.abs(x).reshape(c // 8, 8, LANES), axis=0))
        return worst
    worst = lax.fori_loop(0, n // STAGE_UNROLL, stage, jnp.zeros((8, LANES), F32))
    safe = jnp.max(worst) < SAFE_LOG2

    def shortcut_pairs(d, rows):
        qp_s, kp_s, qt_s, kt_s, _, kd_s, _ = stage_refs[d]
        return _shortcut_pairs(qp_s[rows, :], kp_s[rows, :], qt_s[rows, :], kt_s[rows, :], kd_s[rows, :],
                               v_s[rows, :], lv_refs[d], n_heads=n_heads)

    def shortcut_output(d, rows, values, s):
        _, _, _, _, qe_s, _, dec_s = stage_refs[d]
        return _shortcut_output(*values, qe_s[rows, :], v_s[rows, :], dec_s[rows, :], s, n_heads=n_heads)

    def levels(d, rows, values, st):
        q, vb = load_common(rows)
        k, g = load_dir[d](rows)
        sums = _scan_sums(g, a_refs[d][...])
        return _level_chunk(q.astype(BF16), k.astype(BF16), sums[0:c], sums[c:], vb, st, lv_refs[d],
                            n_heads=n_heads, backward=bool(d))

    pl.when(safe)(lambda: scan(shortcut_pairs, shortcut_output, jnp.zeros((LANES, n_heads * LANES), F32), SCAN_UNROLL))
    pl.when(jnp.logical_not(safe))(lambda: scan(lambda d, rows: None, levels, jnp.zeros((n_heads * LANES, LANES), F32), 1))


def _scan_scratch(seq_all, n_heads):
    assert SCAN_CHUNK == LANES == 2 * LOCAL_BLOCK
    width = n_heads * LANES
    out = [pltpu.VMEM((seq_all, width), F32)] * 2
    stage = [pltpu.VMEM((seq_all, LANES), BF16)] * N_STAGED + [pltpu.VMEM((seq_all, LANES), F32)]
    return out + [pltpu.VMEM((seq_all, width), BF16)] + stage * 2


def _log_sigmoid(x):
    return jnp.minimum(x, 0.0) - jnp.log(1.0 + jnp.exp(-jnp.abs(x)))


def _gla_kernel(q_ref, k_ref, v_ref, g_ref, lr_ref, wgf_ref, wgb_ref, bf_ref, bb_ref, gain_ref,
                af_ref, ab_ref, lvf_ref, lvb_ref, o_ref, of_ref, ob_ref, v_s, *stage_refs, seq_all):
    def load_common(rows):
        return q_ref[rows, :] * (B_DK ** -0.5), v_ref[rows, :].astype(BF16)

    def load_dir(wg_ref, b_ref):
        def load(rows):
            pre = _dot(lr_ref[rows, :].astype(BF16), wg_ref[...]) + b_ref[...]
            return k_ref[rows, :], _log_sigmoid(pre) * (1.0 / B_GATE_NORM)
        return load

    _scan_loop(load_common, (load_dir(wgf_ref, bf_ref), load_dir(wgb_ref, bb_ref)), (of_ref, ob_ref),
               (af_ref, ab_ref), (lvf_ref, lvb_ref), v_s, (stage_refs[:N_STAGED + 1], stage_refs[N_STAGED + 1:]), n_heads=2, seq_all=seq_all)

    def post(j, carry):
        rows = pl.ds(pl.multiple_of(j * ROW_TILE, ROW_TILE), ROW_TILE)
        for hh in range(2):
            cols = slice(hh * LANES, (hh + 1) * LANES)
            o = of_ref[rows, cols] + ob_ref[rows, cols]
            var = jnp.mean(o * o, axis=-1, keepdims=True)
            o_ref[rows, cols] = (o * lax.rsqrt(var + EPS) * gain_ref[...] * _silu(g_ref[rows, cols])).astype(BF16)
        return carry
    lax.fori_loop(0, seq_all // ROW_TILE, post, 0, unroll=3)


def _gla(p3, w_gate_up, b_gate_up, gain, consts):
    batch, seq_all, _ = p3.shape
    pairs = B_HEADS // 2
    wg = jnp.zeros((2, pairs, LANES, LANES), F32)
    for d in range(2):
        w = w_gate_up[d].reshape(B_GATE_RANK, pairs, LANES).transpose(1, 0, 2)
        wg = wg.at[d, :, d * B_GATE_RANK:(d + 1) * B_GATE_RANK, :].set(w)
    wg = wg.astype(BF16)
    bias = b_gate_up.reshape(2, pairs, 1, LANES)
    a_f, a_b, lv_f, lv_b = consts
    col = lambda off: (lambda b, p: (b, 0, off + p))
    q_off = 3 * A_HEADS * LANES // LANES
    k_off = q_off + B_HEADS * B_DK // LANES
    v_off = (k_off * LANES + B_HEADS * B_DK) // (2 * LANES)
    g_off = v_off + pairs
    lr_off = (g_off + pairs) * 2
    wspec = pl.BlockSpec((None, LANES, LANES), lambda b, p: (p, 0, 0))
    bspec = pl.BlockSpec((None, 1, LANES), lambda b, p: (p, 0, 0))
    return pl.pallas_call(
        functools.partial(_gla_kernel, seq_all=seq_all),
        out_shape=jax.ShapeDtypeStruct((batch, seq_all, B_HEADS * LANES), BF16),
        grid=(batch, pairs),
        in_specs=[
            pl.BlockSpec((None, seq_all, LANES), col(q_off)),
            pl.BlockSpec((None, seq_all, LANES), col(k_off)),
            pl.BlockSpec((None, seq_all, 2 * LANES), col(v_off)),
            pl.BlockSpec((None, seq_all, 2 * LANES), col(g_off)),
            pl.BlockSpec((None, seq_all, LANES), lambda b, p: (b, 0, lr_off)),
            wspec, wspec, bspec, bspec,
            _resident((1, LANES)),
            _resident(a_f.shape), _resident(a_b.shape), _resident(lv_f.shape), _resident(lv_b.shape),
        ],
        out_specs=pl.BlockSpec((None, seq_all, 2 * LANES), lambda b, p: (b, 0, p)),
        scratch_shapes=_scan_scratch(seq_all, 2),
        compiler_params=_cparams(("parallel", "parallel")),
        name="gla",
    )(p3, p3, p3, p3, p3, wg[0], wg[1], bias[0], bias[1], gain.reshape(1, LANES), a_f, a_b, lv_f, lv_b)


def _hgrn_kernel(q_ref, ff_ref, fb_ref, i_ref, g_ref, lb_ref, gain_ref,
                 af_ref, ab_ref, lvf_ref, lvb_ref, o_ref, of_ref, ob_ref, v_s, *stage_refs, seq_all, layer):
    def lower_bound(d):
        raw = lb_ref[d]
        ex = jnp.exp(raw - jnp.max(raw, axis=0, keepdims=True))
        p = ex / jnp.sum(ex, axis=0, keepdims=True)
        return jnp.sum(p[1:layer + 1], axis=0, keepdims=True)

    def load_common(rows):
        return _silu(q_ref[rows, :]) * (C_DH ** -0.5), i_ref[rows, :].astype(BF16)

    def load_dir(f_ref, lb):
        def load(rows):
            f = lb + (1.0 - lb) * jax.nn.sigmoid(f_ref[rows, :])
            return 1.0 - f, jnp.log(f)
        return load

    _scan_loop(load_common, (load_dir(ff_ref, lower_bound(0)), load_dir(fb_ref, lower_bound(1))), (of_ref, ob_ref),
               (af_ref, ab_ref), (lvf_ref, lvb_ref), v_s, (stage_refs[:N_STAGED + 1], stage_refs[N_STAGED + 1:]), n_heads=1, seq_all=seq_all)

    def post(j, carry):
        rows = pl.ds(pl.multiple_of(j * ROW_TILE, ROW_TILE), ROW_TILE)
        o = of_ref[rows, :] + ob_ref[rows, :]
        var = jnp.mean(o * o, axis=-1, keepdims=True)
        o_ref[rows, :] = (o * lax.rsqrt(var + EPS) * gain_ref[...] * _silu(g_ref[rows, :])).astype(BF16)
        return carry
    lax.fori_loop(0, seq_all // ROW_TILE, post, 0, unroll=3)


def _hgrn(p3, lb_raw, gain, consts, layer):
    batch, seq_all, _ = p3.shape
    a_f, a_b, lv_f, lv_b = consts
    col = lambda off: (lambda b, h: (b, 0, off + h))
    return pl.pallas_call(
        functools.partial(_hgrn_kernel, seq_all=seq_all, layer=layer),
        out_shape=jax.ShapeDtypeStruct((batch, seq_all, C_HEADS * LANES), BF16),
        grid=(batch, C_HEADS),
        in_specs=[pl.BlockSpec((None, seq_all, LANES), col(s * C_HEADS)) for s in range(5)] + [
            pl.BlockSpec((2, DEPTH, LANES), lambda b, h: (0, 0, h)),
            _resident((1, LANES)),
            _resident(a_f.shape), _resident(a_b.shape), _resident(lv_f.shape), _resident(lv_b.shape),
        ],
        out_specs=pl.BlockSpec((None, seq_all, LANES), lambda b, h: (b, 0, h)),
        scratch_shapes=_scan_scratch(seq_all, 1),
        compiler_params=_cparams(("parallel", "parallel")),
        name="hgrn",
    )(p3, p3, p3, p3, p3, lb_raw, gain.reshape(1, LANES), a_f, a_b, lv_f, lv_b)


def _rope_tables(seq):
    pos = jnp.arange(seq)
    row_ids = (pos // GRID_W).astype(F32)
    col_ids = (pos % GRID_W).astype(F32)
    n_axis = A_DQK // 2
    inv = ROPE_BASE ** (-jnp.arange(0, n_axis, 2, dtype=F32) / n_axis)
    ang_r = row_ids[:, None] * inv
    ang_c = col_ids[:, None] * inv
    zeros = jnp.zeros_like(ang_r)
    cos64 = jnp.concatenate([jnp.cos(ang_r)] * 2 + [jnp.cos(ang_c)] * 2, axis=-1)
    sa64 = jnp.concatenate([-jnp.sin(ang_r), zeros, -jnp.sin(ang_c), zeros], axis=-1)
    sb64 = jnp.concatenate([zeros, jnp.sin(ang_r), zeros, jnp.sin(ang_c)], axis=-1)
    ident = lambda v: jnp.full((CTX_LEN, LANES), v, F32)
    full = lambda t, v: jnp.concatenate([ident(v), jnp.tile(t, (1, 2))], axis=0)
    return full(cos64, 1.0), full(sa64, 0.0), full(sb64, 0.0)


def kernel(x, c, ctx, c_ctx, w_ada, b_ada, norm1_gain, norm2_gain, w_in_even, qk_gain_a, lambda_a, subln_gain_a, w_gate_up_b, b_gate_up_b, onorm_gain_b, w_out_even, w_in_odd, lb_raw_c, onorm_gain_c, w_out_odd, w_ffn_in, w_ffn_out):
    batch, seq, d = x.shape
    seq_all = CTX_LEN + seq
    assert d == D_MODEL and ctx.shape[1] == CTX_LEN == ROW_TILE == Q_TILE and seq % ROW_TILE == 0

    xz = jnp.concatenate([ctx, x], axis=1).reshape(batch * seq_all, d)
    mod_rows = -(-(batch + 1) // 8) * 8
    cc = jnp.zeros((mod_rows, d), F32).at[:batch].set(c).at[batch].set(c_ctx)
    mods = _ada(cc, w_ada, b_ada).reshape(DEPTH, mod_rows, 6, d)

    tabs = _rope_tables(seq)
    half = np.arange(LANES) // A_DQK
    bd = jnp.asarray((half[:, None] == half[None, :]) / A_DQK, BF16)
    consts_gla = _scan_consts(2)
    consts_hgrn = _scan_consts(1)

    for l in range(DEPTH):
        j = l // 2
        last = l == DEPTH - 1
        wi = w_ffn_in[l].astype(BF16)
        wf = w_ffn_out[l].astype(BF16)
        if l % 2 == 0:
            w = jnp.pad(w_in_even[j], ((0, 0), (0, EVEN_COLS_PAD - EVEN_COLS))).astype(BF16)
            p3 = _inproj(xz, mods[l], norm1_gain[l], w, batch).reshape(batch, seq_all, EVEN_COLS_PAD)
            lambda_init = 0.8 - 0.6 * math.exp(-0.3 * l)
            oa = _attn(p3, tabs, qk_gain_a[j], lambda_a[j], subln_gain_a[j], bd, lambda_init)
            ob = _gla(p3, w_gate_up_b[j], b_gate_up_b[j], onorm_gain_b[j], consts_gla)
            a_width = A_HEADS * LANES
            wo = w_out_even[j].astype(BF16)
            o_parts = [oa.reshape(batch * seq_all, -1), ob.reshape(batch * seq_all, -1)]
            wo_parts = [wo[:a_width], wo[a_width:]]
        else:
            p3 = _inproj(xz, mods[l], norm1_gain[l], w_in_odd[j].astype(BF16), batch).reshape(batch, seq_all, ODD_COLS)
            oc = _hgrn(p3, lb_raw_c, onorm_gain_c[j], consts_hgrn, l)
            o_parts = [oc.reshape(batch * seq_all, -1)]
            wo_parts = [w_out_odd[j].astype(BF16)]
        xz = _post(o_parts, xz, mods[l], norm2_gain[l], wo_parts, wi, wf, batch, last)
    return xz.reshape(batch, seq, d)
```

```python
import functools
import math

import numpy as np
import jax
import jax.numpy as jnp
from jax import lax
from jax.experimental import pallas as pl
from jax.experimental.pallas import tpu as pltpu

F32 = jnp.float32
BF16 = jnp.bfloat16

D_MODEL = 1024
DEPTH = 4
CTX_LEN = 256
GRID_W = 64
A_HEADS = 4
A_DQK = 64
B_HEADS = 4
B_DK = 64
B_GATE_RANK = 16
B_GATE_NORM = 16.0
C_HEADS = 8
C_DH = 128
ROPE_BASE = 10000.0
EPS = 1e-6
D_FF = 2816
EVEN_COLS = 3104
EVEN_COLS_PAD = 3200
ODD_COLS = 5120

LANES = 128
ROW_TILE = 256
COL_CHUNK = 512
FF_CHUNK = 256
Q_TILE = 256
Q_TILES_PER_STEP = 4
SUM_ROWS = 16
SAFE_SCORE_LOG2 = 60.0
LOG2E = math.log2(math.e)
SCAN_CHUNK = 128
SCAN_UNROLL = 6
STAGE_UNROLL = 6
VMEM_LIMIT = 52 * 1024 * 1024


def _cparams(sem):
    return pltpu.CompilerParams(dimension_semantics=sem, vmem_limit_bytes=VMEM_LIMIT)


def _resident(shape):
    nd = len(shape)
    return pl.BlockSpec(shape, lambda *_: (0,) * nd, pipeline_mode=pl.Buffered(1))


def _silu(x):
    return x * jax.nn.sigmoid(x)


def _dot(a, b):
    return jnp.dot(a, b, preferred_element_type=F32)


def _dot_nt(a, b):
    return lax.dot_general(a, b, (((1,), (1,)), ((), ())), preferred_element_type=F32)


def _dot_tn(a, b):
    return lax.dot_general(a, b, (((0,), (0,)), ((), ())), preferred_element_type=F32)


def _ada_kernel(c_ref, w_ref, b_ref, o_ref):
    s = _silu(c_ref[...]).astype(BF16)
    o_ref[...] = _dot(s, w_ref[...].astype(BF16)) + b_ref[...]


def _ada(cc, w_ada, b_ada):
    depth, d, n = w_ada.shape
    rows = cc.shape[0]
    tn = 1536
    return pl.pallas_call(
        _ada_kernel,
        out_shape=jax.ShapeDtypeStruct((depth, rows, n), F32),
        grid=(depth, n // tn),
        in_specs=[
            pl.BlockSpec((rows, d), lambda l, j: (0, 0)),
            pl.BlockSpec((None, d, tn), lambda l, j: (l, 0, j)),
            pl.BlockSpec((None, 1, tn), lambda l, j: (l, 0, j)),
        ],
        out_specs=pl.BlockSpec((None, rows, tn), lambda l, j: (l, 0, j)),
        compiler_params=_cparams(("parallel", "parallel")),
        name="ada",
    )(cc, w_ada, b_ada.reshape(depth, 1, n))


def _norm_mod(x, gain, shift, scale):
    var = jnp.mean(x * x, axis=-1, keepdims=True)
    return (x * lax.rsqrt(var + EPS) * gain) * (1.0 + scale) + shift


def _row_sources(rt):
    return list(rt) if isinstance(rt, tuple) else [rt]


def _row_specs(rt, tile_fn, tiles):
    if not isinstance(rt, tuple):
        return [pl.BlockSpec((ROW_TILE, rt.shape[1]), lambda i: (tile_fn(i), 0))]
    sample = lambda i: tile_fn(i) // tiles
    latent = lambda i: sample(i) * (tiles - 1) + jnp.maximum(tile_fn(i) % tiles - 1, 0)
    return [pl.BlockSpec((ROW_TILE, rt[0].shape[1]), lambda i: (sample(i), 0)),
            pl.BlockSpec((ROW_TILE, rt[1].shape[1]), lambda i: (latent(i), 0))]


def _row_tile(refs, is_context):
    if len(refs) == 1:
        return refs[0][...]
    return jnp.where(is_context, refs[0][...], refs[1][...])


def _take(refs, counts):
    groups, pos = [], 0
    for n in counts:
        groups.append(refs[pos:pos + n])
        pos += n
    return groups, refs[pos:]


def _inproj_kernel(*refs, n_x, tiles):
    x_groups, (mod_a_ref, mod_b_ref, gain_ref, w_ref, o_ref) = _take(refs, [n_x, n_x])
    step = pl.program_id(0)
    halves = []
    for t, m in enumerate((mod_a_ref, mod_b_ref)):
        x = _row_tile(x_groups[t], (2 * step + t) % tiles == 0)
        halves.append(_norm_mod(x, gain_ref[...], m[0:1, :], m[1:2, :]).astype(BF16))
    h = jnp.concatenate(halves, axis=0)
    ncols = w_ref.shape[1]
    for c0 in range(0, ncols, COL_CHUNK):
        c1 = min(c0 + COL_CHUNK, ncols)
        o_ref[:, c0:c1] = _dot(h, w_ref[:, c0:c1])


def _mod_row(i, tiles, batch):
    return jnp.where(i % tiles == 0, batch, i // tiles)


def _inproj(xz, mod, gain, w, batch, tiles):
    d, ncols = w.shape
    rows = batch * tiles * ROW_TILE
    mod_spec = lambda t: pl.BlockSpec((None, 6, d), lambda i: (_mod_row(2 * i + t, tiles, batch), 0, 0))
    x_specs = [spec for t in range(2) for spec in _row_specs(xz, (lambda t: lambda i: 2 * i + t)(t), tiles)]
    return pl.pallas_call(
        functools.partial(_inproj_kernel, n_x=len(_row_sources(xz)), tiles=tiles),
        out_shape=jax.ShapeDtypeStruct((rows, ncols), F32),
        grid=(rows // (2 * ROW_TILE),),
        in_specs=x_specs + [mod_spec(0), mod_spec(1), _resident((1, d)), _resident((d, ncols))],
        out_specs=pl.BlockSpec((2 * ROW_TILE, ncols), lambda i: (i, 0)),
        compiler_params=_cparams(("parallel",)),
        name="inproj",
    )(*(_row_sources(xz) * 2), mod, mod, gain.reshape(1, d), w)


def _post_kernel(*refs, n_src, tiles):
    groups, rest = _take(refs, [n for n in n_src for _ in range(2)])
    n_parts = len(n_src) - 1
    mod_refs, gain_ref = rest[0:2], rest[2]
    wo_refs = rest[3:3 + n_parts]
    wi_ref, wf_ref, out_ref = rest[3 + n_parts:]
    step = pl.program_id(0)
    is_context = [(2 * step + t) % tiles == 0 for t in range(2)]
    tile = lambda p, t: _row_tile(groups[2 * p + t], is_context[t])
    y = None
    for p, wo_ref in enumerate(wo_refs):
        part = _dot(jnp.concatenate([tile(p, 0), tile(p, 1)], axis=0), wo_ref[...])
        y = part if y is None else y + part
    x1 = [tile(n_parts, t) + mod_refs[t][2:3, :] * y[t * ROW_TILE:(t + 1) * ROW_TILE] for t in range(2)]
    h = jnp.concatenate([_norm_mod(x1[t], gain_ref[...], mod_refs[t][3:4, :], mod_refs[t][4:5, :]).astype(BF16)
                         for t in range(2)], axis=0)
    acc = None
    for c0 in range(0, D_FF, FF_CHUNK):
        gate = _dot(h, wi_ref[:, c0:c0 + FF_CHUNK])
        up = _dot(h, wi_ref[:, D_FF + c0:D_FF + c0 + FF_CHUNK])
        a = (_silu(gate) * up).astype(BF16)
        part = _dot(a, wf_ref[c0:c0 + FF_CHUNK, :])
        acc = part if acc is None else acc + part
    for t in range(2):
        rows = slice(t * ROW_TILE, (t + 1) * ROW_TILE)
        out_ref[rows, :] = x1[t] + mod_refs[t][5:6, :] * acc[rows]


def _post(o_parts, xz, mod, gain, wo_parts, wi, wf, batch, tiles, latent_only):
    d = wi.shape[0]
    tensors = list(o_parts) + [xz]
    if latent_only:
        assert not any(isinstance(rt, tuple) for rt in tensors)
        out_tiles = tiles - 1
        src = lambda i: (i // out_tiles) * tiles + 1 + i % out_tiles
        mrow = lambda i: i // out_tiles
    else:
        out_tiles = tiles
        src = lambda i: i
        mrow = lambda i: _mod_row(i, tiles, batch)
    assert (batch * out_tiles) % 2 == 0
    tile_fn = lambda t: lambda i: src(2 * i + t)
    in_specs = [spec for rt in tensors for t in range(2) for spec in _row_specs(rt, tile_fn(t), tiles)]
    in_specs += [pl.BlockSpec((None, 6, d), (lambda t: lambda i: (mrow(2 * i + t), 0, 0))(t)) for t in range(2)]
    in_specs += [_resident((1, d))]
    in_specs += [_resident(w.shape) for w in wo_parts]
    in_specs += [_resident(wi.shape), _resident(wf.shape)]
    row_args = [a for rt in tensors for _ in range(2) for a in _row_sources(rt)]
    return pl.pallas_call(
        functools.partial(_post_kernel, n_src=tuple(len(_row_sources(rt)) for rt in tensors), tiles=tiles),
        out_shape=jax.ShapeDtypeStruct((batch * out_tiles * ROW_TILE, d), F32),
        grid=(batch * out_tiles // 2,),
        in_specs=in_specs,
        out_specs=pl.BlockSpec((2 * ROW_TILE, d), lambda i: (i, 0)),
        compiler_params=_cparams(("parallel",)),
        name="post",
    )(*row_args, mod, mod, gain.reshape(1, d), *wo_parts, wi, wf)


def _attn_kernel(qc_ref, *refs, lambda_init, seq_all):
    q_refs = refs[:Q_TILES_PER_STEP]
    (k_ref, v_ref, c_ref, sa_ref, sb_ref, qg_ref, kg_ref, sg_ref, lam_ref, bd_ref,
     octx_ref, olat_ref, kb_ref, vt_ref) = refs[Q_TILES_PER_STEP:]
    step = pl.program_id(2)

    def norm_rope(x, g, rows):
        ms = _dot((x * x).astype(BF16), bd_ref[...])
        y = x * lax.rsqrt(ms + EPS) * g
        return y * c_ref[rows, :] + pltpu.roll(y, LANES - 16, 1) * sa_ref[rows, :] + pltpu.roll(y, 16, 1) * sb_ref[rows, :]

    def tile_rows(t):
        return pl.ds(pl.multiple_of(t * Q_TILE, Q_TILE), Q_TILE)

    @pl.when(step == 0)
    def _prep_keys():
        def body(r, carry):
            rows = tile_rows(r)
            kb_ref[rows, :] = norm_rope(k_ref[rows, :], kg_ref[...], rows).astype(BF16)
            vt_ref[0:LANES, rows] = v_ref[rows, :].T.astype(BF16)
            return carry
        lax.fori_loop(0, seq_all // Q_TILE, body, 0, unroll=3)
        vt_ref[LANES:, :] = jnp.ones((SUM_ROWS, seq_all), BF16)

    lam_v = lam_ref[...]
    e1 = jnp.exp(jnp.sum(lam_v[0:1, :] * lam_v[1:2, :], axis=-1, keepdims=True))
    e2 = jnp.exp(jnp.sum(lam_v[2:3, :] * lam_v[3:4, :], axis=-1, keepdims=True))
    lam = e1 - e2 + lambda_init

    def scores(q_ref, tile, n_keys):
        qn = norm_rope(q_ref[...], qg_ref[...], tile_rows(tile)) * (A_DQK ** -0.5 * LOG2E)
        lane = lax.broadcasted_iota(jnp.int32, qn.shape, 1)
        q2 = jnp.concatenate([jnp.where(lane < A_DQK, qn, 0.0).astype(BF16),
                              jnp.where(lane >= A_DQK, qn, 0.0).astype(BF16)], axis=0)
        return _dot_nt(kb_ref[0:n_keys, :], q2)

    def outputs(s, n_keys, shift):
        if shift is None:
            shift = jnp.max(s, axis=0, keepdims=True)
        p = jnp.exp2(s - shift).astype(BF16)
        pv = _dot(vt_ref[:, 0:n_keys], p)
        pv = pv[0:LANES] / pv[LANES:LANES + 1]
        o = pv[:, :Q_TILE] - lam * pv[:, Q_TILE:]
        var = jnp.mean(o * o, axis=0, keepdims=True)
        o = o * lax.rsqrt(var + EPS) * (sg_ref[...] * (1.0 - lambda_init))
        return o.T.astype(BF16)

    def run(shift):
        first = 1 + Q_TILES_PER_STEP * step
        all_scores = [scores(q_ref, first + t, seq_all) for t, q_ref in enumerate(q_refs)]
        for t, s in enumerate(all_scores):
            olat_ref[t * Q_TILE:(t + 1) * Q_TILE, :] = outputs(s, seq_all, shift)

        @pl.when(step == 0)
        def _context_tile():
            octx_ref[...] = outputs(scores(qc_ref, 0, CTX_LEN), CTX_LEN, shift)

    bound = (A_DQK ** 0.5 * LOG2E) * (jnp.max(jnp.abs(qg_ref[...]), axis=-1, keepdims=True)
                                      * jnp.max(jnp.abs(kg_ref[...]), axis=-1, keepdims=True)) + 1.0
    small = jnp.max(bound) < SAFE_SCORE_LOG2
    pl.when(small)(lambda: run(bound))
    pl.when(jnp.logical_not(small))(lambda: run(None))


def _attn(p3, tabs, qk_gain, lam, sub_gain, bd, lambda_init):
    batch, seq_all, _ = p3.shape
    seq = seq_all - CTX_LEN
    n_t = Q_TILES_PER_STEP
    assert seq % (n_t * Q_TILE) == 0
    qg = jnp.tile(qk_gain[0], 2).reshape(1, LANES)
    kg = jnp.tile(qk_gain[1], 2).reshape(1, LANES)
    full = _resident((seq_all, LANES))
    kern = functools.partial(_attn_kernel, lambda_init=lambda_init, seq_all=seq_all)
    q_spec = lambda t: pl.BlockSpec((None, Q_TILE, LANES), lambda b, h, i: (b, 1 + n_t * i + t, h))
    o_ctx, o_lat = pl.pallas_call(
        kern,
        out_shape=(jax.ShapeDtypeStruct((batch, CTX_LEN, A_HEADS * LANES), BF16),
                   jax.ShapeDtypeStruct((batch, seq, A_HEADS * LANES), BF16)),
        grid=(batch, A_HEADS, seq // (n_t * Q_TILE)),
        in_specs=[pl.BlockSpec((None, Q_TILE, LANES), lambda b, h, i: (b, 0, h))]
        + [q_spec(t) for t in range(n_t)] + [
            pl.BlockSpec((None, seq_all, LANES), lambda b, h, i: (b, 0, A_HEADS + h)),
            pl.BlockSpec((None, seq_all, LANES), lambda b, h, i: (b, 0, 2 * A_HEADS + h)),
            full, full, full,
            _resident((1, LANES)), _resident((1, LANES)), _resident((LANES, 1)),
            _resident((4, A_DQK)), _resident((LANES, LANES)),
        ],
        out_specs=(pl.BlockSpec((None, CTX_LEN, LANES), lambda b, h, i: (b, 0, h)),
                   pl.BlockSpec((None, n_t * Q_TILE, LANES), lambda b, h, i: (b, i, h))),
        scratch_shapes=[pltpu.VMEM((seq_all, LANES), BF16), pltpu.VMEM((LANES + SUM_ROWS, seq_all), BF16)],
        compiler_params=_cparams(("parallel", "parallel", "arbitrary")),
        name="attn",
    )(*([p3] * (n_t + 3)), *tabs, qg, kg, sub_gain.reshape(LANES, 1), lam, bd)
    return o_ctx, o_lat


MATMUL_LEVELS = 2
LOCAL_BLOCK = 64
N_STAGED = 5
SAFE_LOG2 = 100.0


def _scan_levels():
    return int(math.log2(SCAN_CHUNK))


def _scan_consts(n_heads):
    c = SCAN_CHUNK
    t = np.arange(c)[:, None]
    u = np.arange(c)[None, :]
    blocks_f = [u <= t]
    blocks_b = [u >= t]
    for lvl in range(1, MATMUL_LEVELS + 1):
        h = 1 << (lvl - 1)
        m = (t // (2 * h)) * (2 * h) + h
        second = (t % (2 * h)) >= h
        blocks_f.append(np.where(second, (u >= m) & (u <= t), (u > t) & (u <= m - 1)))
        blocks_b.append(np.where(second, (u >= m) & (u < t), (u >= t) & (u <= m - 1)))
    a_f = np.tile(np.concatenate(blocks_f, axis=0).astype(np.float32), (1, 2))
    a_b = np.tile(np.concatenate(blocks_b, axis=0).astype(np.float32), (1, 2))
    x = t ^ u
    lv = np.where(x > 0, np.floor(np.log2(np.maximum(x, 1))).astype(np.int32) + 1, 0)
    lv_f = np.where(u <= t, lv, -1).astype(np.int32)
    lv_b = np.where(u >= t, lv, -1).astype(np.int32)
    lv_f = np.tile(lv_f, (n_heads, 1))
    lv_b = np.tile(lv_b, (n_heads, 1))
    return (jnp.asarray(a_f, BF16), jnp.asarray(a_b, BF16), jnp.asarray(lv_f), jnp.asarray(lv_b))


def _scan_sums(g, a):
    g2 = g * LOG2E
    g_hi = g2.astype(BF16)
    g_lo = (g2 - g_hi.astype(F32)).astype(BF16)
    return _dot(a, jnp.concatenate([g_hi, g_lo], axis=0))


def _level_exponent(b, lvl, backward):
    c = SCAN_CHUNK
    h = 1 << (lvl - 1)
    r0 = h if backward else h - 1
    b3 = b.reshape(c // (2 * h), 2 * h, LANES)
    ref = b3[:, r0:r0 + 1, :]
    if h % 8 == 0:
        first, second = b3[:, :h, :], b3[:, h:, :]
        x = jnp.concatenate([first - ref, ref - second] if backward else [ref - first, second - ref], axis=1)
    else:
        x = -jnp.abs(b3 - ref)
    return x.reshape(c, LANES)


def _by_head(xb, n_heads):
    if n_heads == 1:
        return xb
    dk = LANES // n_heads
    lane = lax.broadcasted_iota(jnp.int32, xb.shape, 1)
    masks = [jnp.where((lane >= hh * dk) & (lane < (hh + 1) * dk), 1.0, 0.0).astype(BF16) for hh in range(n_heads)]
    return jnp.concatenate([xb * hm for hm in masks], axis=0)


def _head_block_mask(shape, row_block, col_block):
    r = lax.broadcasted_iota(jnp.int32, shape, 0) // row_block
    cc = lax.broadcasted_iota(jnp.int32, shape, 1) // col_block
    return r == cc


def _level_chunk(qb, kb, b, fine, vb, st, lv_ref, *, n_heads, backward):
    c = SCAN_CHUNK
    nl = _scan_levels()
    edge = b[0:1, :] if backward else b[c - 1:c, :]
    e_in = jnp.exp2(b).astype(BF16)
    e_out = jnp.exp2(edge - b).astype(BF16)

    def level_decay(lvl):
        if lvl <= MATMUL_LEVELS:
            return jnp.exp2(fine[(lvl - 1) * c:lvl * c]).astype(BF16)
        return jnp.exp2(_level_exponent(b, lvl, backward)).astype(BF16)

    o = _dot_nt(qb * e_in, st.astype(BF16))

    lv = lv_ref[...]
    att = jnp.zeros((n_heads * c, c), F32)
    for lvl in range(nl + 1):
        if lvl == 0:
            qh, kh = qb, kb
        else:
            el = level_decay(lvl)
            qh, kh = qb * el, kb * el
        att = jnp.where(lv == lvl, _dot_nt(_by_head(qh, n_heads), kh), att)
    attb = att.astype(BF16)
    intra = [_dot(attb[hh * c:(hh + 1) * c], vb[:, hh * LANES:(hh + 1) * LANES]) for hh in range(n_heads)]
    o = o + (intra[0] if n_heads == 1 else jnp.concatenate(intra, axis=1))

    dec = jnp.exp2(edge)
    upd = _dot_tn(vb, kb * e_out)
    if n_heads > 1:
        upd = jnp.where(_head_block_mask(upd.shape, LANES, LANES // n_heads), upd, 0.0)
    return o, dec * st + upd


def _shortcut_pairs(qp, kp, qt, kd, vb, lv_ref, *, n_heads):
    lv = lv_ref[...]
    rows = lv.shape[0]
    both = _dot_nt(jnp.concatenate([_by_head(qp, n_heads), _by_head(qt, n_heads)], axis=0), kp)
    att = jnp.where(lv == _scan_levels(), both[rows:], jnp.where(lv >= 0, both[:rows], 0.0)).astype(BF16)
    upd = _dot_tn(kd, vb)
    if n_heads > 1:
        upd = jnp.where(_head_block_mask(upd.shape, LANES // n_heads, LANES), upd, 0.0)
    return att, upd


def _shortcut_output(att, upd, qe, vb, dec, s, *, n_heads):
    c = SCAN_CHUNK
    sb = s.astype(BF16)
    outs = []
    for hh in range(n_heads):
        cols = slice(hh * LANES, (hh + 1) * LANES)
        lhs = jnp.concatenate([att[hh * c:(hh + 1) * c], qe], axis=1)
        rhs = jnp.concatenate([vb[:, cols], sb[:, cols]], axis=0)
        outs.append(_dot(lhs, rhs))
    o = outs[0] if n_heads == 1 else jnp.concatenate(outs, axis=1)
    if n_heads > 1:
        dec = jnp.concatenate([dec] * n_heads, axis=1)
    return o, dec * s + upd


def _scan_loop(load_common, load_dir, out_refs, a_refs, lv_refs, v_s, stage_refs, *, n_heads, seq_all):
    c = SCAN_CHUNK
    n = seq_all // c
    n_ctx = CTX_LEN // c

    def chunk_rows(j):
        return pl.ds(pl.multiple_of(j * c, c), c)

    def scan(prepare, finish, state0, unroll):
        def body(i, sts):
            sts = list(sts)
            todo = []
            for u in range(unroll):
                j = i * unroll + u
                jb = jnp.where(j < n_ctx, n_ctx - 1 - j, n - 1 - (j - n_ctx))
                for d, jj in ((0, j), (1, jb)):
                    rows = chunk_rows(jj)
                    todo.append((d, rows, prepare(d, rows)))
            for d, rows, values in todo:
                out_refs[d][rows, :], sts[d] = finish(d, rows, values, sts[d])
            return tuple(sts)
        lax.fori_loop(0, n // unroll, body, (state0, state0))

    def stage(j, worst):
        gated = []
        for u in range(STAGE_UNROLL):
            rows = chunk_rows(j * STAGE_UNROLL + u)
            q, vb = load_common(rows)
            v_s[rows, :] = vb
            gated.append((rows, q, [load_dir[d](rows) for d in (0, 1)]))
        summed = [(rows, q, [(k, _scan_sums(g, a_refs[d][0:c, :])) for d, (k, g) in enumerate(kg)])
                  for rows, q, kg in gated]
        for rows, q, kb in summed:
            for d, (k, b) in enumerate(kb):
                b3 = b.reshape(c // LOCAL_BLOCK, LOCAL_BLOCK, LANES)
                r0 = LOCAL_BLOCK // 2 if d else LOCAL_BLOCK // 2 - 1
                ref = b3[:, r0:r0 + 1, :]
                x = b3 - ref
                edge = b[0:1, :] if d else b[c - 1:c, :]
                qp = q * jnp.exp2(x).reshape(c, LANES)
                kp = k * jnp.exp2(-x).reshape(c, LANES)
                entry = jnp.broadcast_to(jnp.exp2(ref), b3.shape).reshape(c, LANES)
                leave = jnp.broadcast_to(jnp.exp2(edge - ref), b3.shape).reshape(c, LANES)
                cross = jnp.exp2(ref[0] - ref[1] if d else ref[1] - ref[0])
                qp_s, kp_s, qt_s, qe_s, kd_s, dec_s = stage_refs[d]
                qp_s[rows, :] = qp.astype(BF16)
                kp_s[rows, :] = kp.astype(BF16)
                qt_s[rows, :] = (qp * cross).astype(BF16)
                qe_s[rows, :] = (qp * entry).astype(BF16)
                kd_s[rows, :] = (kp * leave).astype(BF16)
                dec_s[rows, :] = jnp.broadcast_to(jnp.exp2(edge), (LANES, LANES)).T
                worst = jnp.maximum(worst, jnp.max(jnp.abs(x).reshape(c // 8, 8, LANES), axis=0))
        return worst
    worst = lax.fori_loop(0, n // STAGE_UNROLL, stage, jnp.zeros((8, LANES), F32))
    safe = jnp.max(worst) < SAFE_LOG2

    def shortcut_pairs(d, rows):
        qp_s, kp_s, qt_s, _, kd_s, _ = stage_refs[d]
        return _shortcut_pairs(qp_s[rows, :], kp_s[rows, :], qt_s[rows, :], kd_s[rows, :],
                               v_s[rows, :], lv_refs[d], n_heads=n_heads)

    def shortcut_output(d, rows, values, s):
        _, _, _, qe_s, _, dec_s = stage_refs[d]
        return _shortcut_output(*values, qe_s[rows, :], v_s[rows, :], dec_s[rows, :], s, n_heads=n_heads)

    def levels(d, rows, values, st):
        q, vb = load_common(rows)
        k, g = load_dir[d](rows)
        sums = _scan_sums(g, a_refs[d][...])
        return _level_chunk(q.astype(BF16), k.astype(BF16), sums[0:c], sums[c:], vb, st, lv_refs[d],
                            n_heads=n_heads, backward=bool(d))

    pl.when(safe)(lambda: scan(shortcut_pairs, shortcut_output, jnp.zeros((LANES, n_heads * LANES), F32), SCAN_UNROLL))
    pl.when(jnp.logical_not(safe))(lambda: scan(lambda d, rows: None, levels, jnp.zeros((n_heads * LANES, LANES), F32), 1))


def _scan_scratch(seq_all, n_heads):
    assert SCAN_CHUNK == LANES == 2 * LOCAL_BLOCK
    width = n_heads * LANES
    out = [pltpu.VMEM((seq_all, width), F32)] * 2
    stage = [pltpu.VMEM((seq_all, LANES), BF16)] * N_STAGED + [pltpu.VMEM((seq_all, LANES), F32)]
    return out + [pltpu.VMEM((seq_all, width), BF16)] + stage * 2


def _log_sigmoid(x):
    return jnp.minimum(x, 0.0) - jnp.log(1.0 + jnp.exp(-jnp.abs(x)))


def _gla_kernel(q_ref, k_ref, v_ref, g_ref, lr_ref, wgf_ref, wgb_ref, bf_ref, bb_ref, gain_ref,
                af_ref, ab_ref, lvf_ref, lvb_ref, o_ref, of_ref, ob_ref, v_s, *stage_refs, seq_all):
    def load_common(rows):
        return q_ref[rows, :] * (B_DK ** -0.5), v_ref[rows, :].astype(BF16)

    def load_dir(wg_ref, b_ref):
        def load(rows):
            pre = _dot(lr_ref[rows, :].astype(BF16), wg_ref[...]) + b_ref[...]
            return k_ref[rows, :], _log_sigmoid(pre) * (1.0 / B_GATE_NORM)
        return load

    _scan_loop(load_common, (load_dir(wgf_ref, bf_ref), load_dir(wgb_ref, bb_ref)), (of_ref, ob_ref),
               (af_ref, ab_ref), (lvf_ref, lvb_ref), v_s, (stage_refs[:N_STAGED + 1], stage_refs[N_STAGED + 1:]), n_heads=2, seq_all=seq_all)

    def post(j, carry):
        rows = pl.ds(pl.multiple_of(j * ROW_TILE, ROW_TILE), ROW_TILE)
        for hh in range(2):
            cols = slice(hh * LANES, (hh + 1) * LANES)
            o = of_ref[rows, cols] + ob_ref[rows, cols]
            var = jnp.mean(o * o, axis=-1, keepdims=True)
            o_ref[rows, cols] = (o * lax.rsqrt(var + EPS) * gain_ref[...] * _silu(g_ref[rows, cols])).astype(BF16)
        return carry
    lax.fori_loop(0, seq_all // ROW_TILE, post, 0, unroll=3)


def _gla(p3, w_gate_up, b_gate_up, gain, consts):
    batch, seq_all, _ = p3.shape
    pairs = B_HEADS // 2
    wg = jnp.zeros((2, pairs, LANES, LANES), F32)
    for d in range(2):
        w = w_gate_up[d].reshape(B_GATE_RANK, pairs, LANES).transpose(1, 0, 2)
        wg = wg.at[d, :, d * B_GATE_RANK:(d + 1) * B_GATE_RANK, :].set(w)
    wg = wg.astype(BF16)
    bias = b_gate_up.reshape(2, pairs, 1, LANES)
    a_f, a_b, lv_f, lv_b = consts
    col = lambda off: (lambda b, p: (b, 0, off + p))
    q_off = 3 * A_HEADS * LANES // LANES
    k_off = q_off + B_HEADS * B_DK // LANES
    v_off = (k_off * LANES + B_HEADS * B_DK) // (2 * LANES)
    g_off = v_off + pairs
    lr_off = (g_off + pairs) * 2
    wspec = pl.BlockSpec((None, LANES, LANES), lambda b, p: (p, 0, 0))
    bspec = pl.BlockSpec((None, 1, LANES), lambda b, p: (p, 0, 0))
    return pl.pallas_call(
        functools.partial(_gla_kernel, seq_all=seq_all),
        out_shape=jax.ShapeDtypeStruct((batch, seq_all, B_HEADS * LANES), BF16),
        grid=(batch, pairs),
        in_specs=[
            pl.BlockSpec((None, seq_all, LANES), col(q_off)),
            pl.BlockSpec((None, seq_all, LANES), col(k_off)),
            pl.BlockSpec((None, seq_all, 2 * LANES), col(v_off)),
            pl.BlockSpec((None, seq_all, 2 * LANES), col(g_off)),
            pl.BlockSpec((None, seq_all, LANES), lambda b, p: (b, 0, lr_off)),
            wspec, wspec, bspec, bspec,
            _resident((1, LANES)),
            _resident(a_f.shape), _resident(a_b.shape), _resident(lv_f.shape), _resident(lv_b.shape),
        ],
        out_specs=pl.BlockSpec((None, seq_all, 2 * LANES), lambda b, p: (b, 0, p)),
        scratch_shapes=_scan_scratch(seq_all, 2),
        compiler_params=_cparams(("parallel", "parallel")),
        name="gla",
    )(p3, p3, p3, p3, p3, wg[0], wg[1], bias[0], bias[1], gain.reshape(1, LANES), a_f, a_b, lv_f, lv_b)


def _hgrn_kernel(q_ref, ff_ref, fb_ref, i_ref, g_ref, lb_ref, gain_ref,
                 af_ref, ab_ref, lvf_ref, lvb_ref, o_ref, of_ref, ob_ref, v_s, *stage_refs, seq_all, layer):
    def lower_bound(d):
        raw = lb_ref[d]
        ex = jnp.exp(raw - jnp.max(raw, axis=0, keepdims=True))
        p = ex / jnp.sum(ex, axis=0, keepdims=True)
        return jnp.sum(p[1:layer + 1], axis=0, keepdims=True)

    def load_common(rows):
        return _silu(q_ref[rows, :]) * (C_DH ** -0.5), i_ref[rows, :].astype(BF16)

    def load_dir(f_ref, lb):
        def load(rows):
            f = lb + (1.0 - lb) * jax.nn.sigmoid(f_ref[rows, :])
            return 1.0 - f, jnp.log(f)
        return load

    _scan_loop(load_common, (load_dir(ff_ref, lower_bound(0)), load_dir(fb_ref, lower_bound(1))), (of_ref, ob_ref),
               (af_ref, ab_ref), (lvf_ref, lvb_ref), v_s, (stage_refs[:N_STAGED + 1], stage_refs[N_STAGED + 1:]), n_heads=1, seq_all=seq_all)

    def post(j, carry):
        rows = pl.ds(pl.multiple_of(j * ROW_TILE, ROW_TILE), ROW_TILE)
        o = of_ref[rows, :] + ob_ref[rows, :]
        var = jnp.mean(o * o, axis=-1, keepdims=True)
        o_ref[rows, :] = (o * lax.rsqrt(var + EPS) * gain_ref[...] * _silu(g_ref[rows, :])).astype(BF16)
        return carry
    lax.fori_loop(0, seq_all // ROW_TILE, post, 0, unroll=3)


def _hgrn(p3, lb_raw, gain, consts, layer):
    batch, seq_all, _ = p3.shape
    a_f, a_b, lv_f, lv_b = consts
    col = lambda off: (lambda b, h: (b, 0, off + h))
    return pl.pallas_call(
        functools.partial(_hgrn_kernel, seq_all=seq_all, layer=layer),
        out_shape=jax.ShapeDtypeStruct((batch, seq_all, C_HEADS * LANES), BF16),
        grid=(batch, C_HEADS),
        in_specs=[pl.BlockSpec((None, seq_all, LANES), col(s * C_HEADS)) for s in range(5)] + [
            pl.BlockSpec((2, DEPTH, LANES), lambda b, h: (0, 0, h)),
            _resident((1, LANES)),
            _resident(a_f.shape), _resident(a_b.shape), _resident(lv_f.shape), _resident(lv_b.shape),
        ],
        out_specs=pl.BlockSpec((None, seq_all, LANES), lambda b, h: (b, 0, h)),
        scratch_shapes=_scan_scratch(seq_all, 1),
        compiler_params=_cparams(("parallel", "parallel")),
        name="hgrn",
    )(p3, p3, p3, p3, p3, lb_raw, gain.reshape(1, LANES), a_f, a_b, lv_f, lv_b)


def _rope_tables(seq):
    pos = jnp.arange(seq)
    row_ids = (pos // GRID_W).astype(F32)
    col_ids = (pos % GRID_W).astype(F32)
    n_axis = A_DQK // 2
    inv = ROPE_BASE ** (-jnp.arange(0, n_axis, 2, dtype=F32) / n_axis)
    ang_r = row_ids[:, None] * inv
    ang_c = col_ids[:, None] * inv
    zeros = jnp.zeros_like(ang_r)
    cos64 = jnp.concatenate([jnp.cos(ang_r)] * 2 + [jnp.cos(ang_c)] * 2, axis=-1)
    sa64 = jnp.concatenate([-jnp.sin(ang_r), zeros, -jnp.sin(ang_c), zeros], axis=-1)
    sb64 = jnp.concatenate([zeros, jnp.sin(ang_r), zeros, jnp.sin(ang_c)], axis=-1)
    ident = lambda v: jnp.full((CTX_LEN, LANES), v, F32)
    full = lambda t, v: jnp.concatenate([ident(v), jnp.tile(t, (1, 2))], axis=0)
    return full(cos64, 1.0), full(sa64, 0.0), full(sb64, 0.0)


def kernel(x, c, ctx, c_ctx, w_ada, b_ada, norm1_gain, norm2_gain, w_in_even, qk_gain_a, lambda_a, subln_gain_a, w_gate_up_b, b_gate_up_b, onorm_gain_b, w_out_even, w_in_odd, lb_raw_c, onorm_gain_c, w_out_odd, w_ffn_in, w_ffn_out):
    batch, seq, d = x.shape
    seq_all = CTX_LEN + seq
    assert d == D_MODEL and ctx.shape[1] == CTX_LEN == ROW_TILE == Q_TILE and seq % ROW_TILE == 0

    tiles = seq_all // ROW_TILE
    flat = lambda a: a.reshape(-1, a.shape[-1])
    xz = (flat(ctx), flat(x))
    mod_rows = -(-(batch + 1) // 8) * 8
    cc = jnp.zeros((mod_rows, d), F32).at[:batch].set(c).at[batch].set(c_ctx)
    mods = _ada(cc, w_ada, b_ada).reshape(DEPTH, mod_rows, 6, d)

    tabs = _rope_tables(seq)
    half = np.arange(LANES) // A_DQK
    bd = jnp.asarray((half[:, None] == half[None, :]) / A_DQK, BF16)
    consts_gla = _scan_consts(2)
    consts_hgrn = _scan_consts(1)

    for l in range(DEPTH):
        j = l // 2
        last = l == DEPTH - 1
        wi = w_ffn_in[l].astype(BF16)
        wf = w_ffn_out[l].astype(BF16)
        if l % 2 == 0:
            w = jnp.pad(w_in_even[j], ((0, 0), (0, EVEN_COLS_PAD - EVEN_COLS))).astype(BF16)
            p3 = _inproj(xz, mods[l], norm1_gain[l], w, batch, tiles).reshape(batch, seq_all, EVEN_COLS_PAD)
            lambda_init = 0.8 - 0.6 * math.exp(-0.3 * l)
            oa_ctx, oa_lat = _attn(p3, tabs, qk_gain_a[j], lambda_a[j], subln_gain_a[j], bd, lambda_init)
            ob = _gla(p3, w_gate_up_b[j], b_gate_up_b[j], onorm_gain_b[j], consts_gla)
            a_width = A_HEADS * LANES
            wo = w_out_even[j].astype(BF16)
            o_parts = [(flat(oa_ctx), flat(oa_lat)), flat(ob)]
            wo_parts = [wo[:a_width], wo[a_width:]]
        else:
            p3 = _inproj(xz, mods[l], norm1_gain[l], w_in_odd[j].astype(BF16), batch, tiles).reshape(batch, seq_all, ODD_COLS)
            oc = _hgrn(p3, lb_raw_c, onorm_gain_c[j], consts_hgrn, l)
            o_parts = [flat(oc)]
            wo_parts = [w_out_odd[j].astype(BF16)]
        xz = _post(o_parts, xz, mods[l], norm2_gain[l], wo_parts, wi, wf, batch, tiles, last)
    return xz.reshape(batch, seq, d)
```

```python
import functools
import math

import numpy as np
import jax
import jax.numpy as jnp
from jax import lax
from jax.experimental import pallas as pl
from jax.experimental.pallas import tpu as pltpu

F32 = jnp.float32
BF16 = jnp.bfloat16

D_MODEL = 1024
DEPTH = 4
CTX_LEN = 256
GRID_W = 64
A_HEADS = 4
A_DQK = 64
B_HEADS = 4
B_DK = 64
B_GATE_RANK = 16
B_GATE_NORM = 16.0
C_HEADS = 8
C_DH = 128
ROPE_BASE = 10000.0
EPS = 1e-6
D_FF = 2816
EVEN_COLS = 3104
EVEN_COLS_PAD = 3200
ODD_COLS = 5120

LANES = 128
ROW_TILE = 256
COL_CHUNK = 512
FF_CHUNK = 256
Q_TILE = 256
Q_TILES_PER_STEP = 4
SUM_ROWS = 16
SAFE_SCORE_LOG2 = 60.0
LOG2E = math.log2(math.e)
SCAN_CHUNK = 128
SCAN_UNROLL = 6
STAGE_UNROLL = 6
VMEM_LIMIT = 52 * 1024 * 1024


def _cparams(sem):
    return pltpu.CompilerParams(dimension_semantics=sem, vmem_limit_bytes=VMEM_LIMIT)


def _resident(shape):
    nd = len(shape)
    return pl.BlockSpec(shape, lambda *_: (0,) * nd, pipeline_mode=pl.Buffered(1))


def _silu(x):
    return x * jax.nn.sigmoid(x)


def _dot(a, b):
    return jnp.dot(a, b, preferred_element_type=F32)


def _dot_nt(a, b):
    return lax.dot_general(a, b, (((1,), (1,)), ((), ())), preferred_element_type=F32)


def _dot_tn(a, b):
    return lax.dot_general(a, b, (((0,), (0,)), ((), ())), preferred_element_type=F32)


def _ada_kernel(c_ref, w_ref, b_ref, o_ref):
    s = _silu(c_ref[...]).astype(BF16)
    o_ref[...] = _dot(s, w_ref[...].astype(BF16)) + b_ref[...]


def _ada(cc, w_ada, b_ada):
    depth, d, n = w_ada.shape
    rows = cc.shape[0]
    tn = 1536
    return pl.pallas_call(
        _ada_kernel,
        out_shape=jax.ShapeDtypeStruct((depth, rows, n), F32),
        grid=(depth, n // tn),
        in_specs=[
            pl.BlockSpec((rows, d), lambda l, j: (0, 0)),
            pl.BlockSpec((None, d, tn), lambda l, j: (l, 0, j)),
            pl.BlockSpec((None, 1, tn), lambda l, j: (l, 0, j)),
        ],
        out_specs=pl.BlockSpec((None, rows, tn), lambda l, j: (l, 0, j)),
        compiler_params=_cparams(("parallel", "parallel")),
        name="ada",
    )(cc, w_ada, b_ada.reshape(depth, 1, n))


def _norm_mod(x, gain, shift, scale):
    var = jnp.mean(x * x, axis=-1, keepdims=True)
    return (x * lax.rsqrt(var + EPS) * gain) * (1.0 + scale) + shift


def _row_sources(rt):
    return list(rt) if isinstance(rt, tuple) else [rt]


def _row_specs(rt, tile_fn, tiles):
    if not isinstance(rt, tuple):
        return [pl.BlockSpec((ROW_TILE, rt.shape[1]), lambda i: (tile_fn(i), 0))]
    sample = lambda i: tile_fn(i) // tiles
    latent = lambda i: sample(i) * (tiles - 1) + jnp.maximum(tile_fn(i) % tiles - 1, 0)
    return [pl.BlockSpec((ROW_TILE, rt[0].shape[1]), lambda i: (sample(i), 0)),
            pl.BlockSpec((ROW_TILE, rt[1].shape[1]), lambda i: (latent(i), 0))]


def _row_tile(refs, is_context):
    if len(refs) == 1:
        return refs[0][...]
    return jnp.where(is_context, refs[0][...], refs[1][...])


def _take(refs, counts):
    groups, pos = [], 0
    for n in counts:
        groups.append(refs[pos:pos + n])
        pos += n
    return groups, refs[pos:]


def _inproj_kernel(*refs, n_x, tiles, qk_heads):
    x_groups, rest = _take(refs, [n_x, n_x])
    mod_a_ref, mod_b_ref, gain_ref, w_ref = rest[:4]
    o_ref = rest[-1]
    step = pl.program_id(0)
    halves = []
    for t, m in enumerate((mod_a_ref, mod_b_ref)):
        x = _row_tile(x_groups[t], (2 * step + t) % tiles == 0)
        halves.append(_norm_mod(x, gain_ref[...], m[0:1, :], m[1:2, :]).astype(BF16))
    h = jnp.concatenate(halves, axis=0)
    ncols = w_ref.shape[1]
    for c0 in range(0, ncols, COL_CHUNK):
        c1 = min(c0 + COL_CHUNK, ncols)
        res = _dot(h, w_ref[:, c0:c1])
        if c1 > 2 * qk_heads * LANES:
            o_ref[:, c0:c1] = res
            continue
        c_ref, sa_ref, sb_ref, qg_ref, kg_ref, bd_ref = rest[4:10]
        gains = (qg_ref[...] * (A_DQK ** -0.5 * LOG2E), kg_ref[...])
        blocks = [res[:, b0 - c0:b0 - c0 + LANES] for b0 in range(c0, c1, LANES)]
        ms_all = _dot(jnp.concatenate([(x * x).astype(BF16) for x in blocks], axis=0), bd_ref[...])
        n_rows = 2 * ROW_TILE
        for i, x in enumerate(blocks):
            b0 = c0 + i * LANES
            y = x * lax.rsqrt(ms_all[i * n_rows:(i + 1) * n_rows] + EPS) * gains[b0 // (qk_heads * LANES)]
            for t in range(2):
                pos = pl.ds(pl.multiple_of(((2 * step + t) % tiles) * ROW_TILE, ROW_TILE), ROW_TILE)
                rows = slice(t * ROW_TILE, (t + 1) * ROW_TILE)
                yt = y[rows]
                o_ref[rows, b0:b0 + LANES] = (yt * c_ref[pos, :] + pltpu.roll(yt, LANES - 16, 1) * sa_ref[pos, :]
                                              + pltpu.roll(yt, 16, 1) * sb_ref[pos, :])


def _mod_row(i, tiles, batch):
    return jnp.where(i % tiles == 0, batch, i // tiles)


def _inproj(xz, mod, gain, w, batch, tiles, qk=None):
    d, ncols = w.shape
    rows = batch * tiles * ROW_TILE
    mod_spec = lambda t: pl.BlockSpec((None, 6, d), lambda i: (_mod_row(2 * i + t, tiles, batch), 0, 0))
    x_specs = [spec for t in range(2) for spec in _row_specs(xz, (lambda t: lambda i: 2 * i + t)(t), tiles)]
    qk_args = []
    if qk:
        tabs, q_gain, k_gain, bd = qk
        qk_args = [*tabs, jnp.tile(q_gain, 2).reshape(1, LANES), jnp.tile(k_gain, 2).reshape(1, LANES), bd]
    return pl.pallas_call(
        functools.partial(_inproj_kernel, n_x=len(_row_sources(xz)), tiles=tiles, qk_heads=A_HEADS if qk else 0),
        out_shape=jax.ShapeDtypeStruct((rows, ncols), F32),
        grid=(rows // (2 * ROW_TILE),),
        in_specs=x_specs + [mod_spec(0), mod_spec(1), _resident((1, d)), _resident((d, ncols))]
        + [_resident(a.shape) for a in qk_args],
        out_specs=pl.BlockSpec((2 * ROW_TILE, ncols), lambda i: (i, 0)),
        compiler_params=_cparams(("parallel",)),
        name="inproj",
    )(*(_row_sources(xz) * 2), mod, mod, gain.reshape(1, d), w, *qk_args)


def _post_kernel(*refs, n_src, tiles):
    groups, rest = _take(refs, [n for n in n_src for _ in range(2)])
    n_parts = len(n_src) - 1
    mod_refs, gain_ref = rest[0:2], rest[2]
    wo_refs = rest[3:3 + n_parts]
    wi_ref, wf_ref, out_ref = rest[3 + n_parts:]
    step = pl.program_id(0)
    is_context = [(2 * step + t) % tiles == 0 for t in range(2)]
    tile = lambda p, t: _row_tile(groups[2 * p + t], is_context[t])
    y = None
    for p, wo_ref in enumerate(wo_refs):
        part = _dot(jnp.concatenate([tile(p, 0), tile(p, 1)], axis=0), wo_ref[...])
        y = part if y is None else y + part
    x1 = [tile(n_parts, t) + mod_refs[t][2:3, :] * y[t * ROW_TILE:(t + 1) * ROW_TILE] for t in range(2)]
    h = jnp.concatenate([_norm_mod(x1[t], gain_ref[...], mod_refs[t][3:4, :], mod_refs[t][4:5, :]).astype(BF16)
                         for t in range(2)], axis=0)
    acc = None
    for c0 in range(0, D_FF, FF_CHUNK):
        gate = _dot(h, wi_ref[:, c0:c0 + FF_CHUNK])
        up = _dot(h, wi_ref[:, D_FF + c0:D_FF + c0 + FF_CHUNK])
        a = (_silu(gate) * up).astype(BF16)
        part = _dot(a, wf_ref[c0:c0 + FF_CHUNK, :])
        acc = part if acc is None else acc + part
    for t in range(2):
        rows = slice(t * ROW_TILE, (t + 1) * ROW_TILE)
        out_ref[rows, :] = x1[t] + mod_refs[t][5:6, :] * acc[rows]


def _post(o_parts, xz, mod, gain, wo_parts, wi, wf, batch, tiles, latent_only):
    d = wi.shape[0]
    tensors = list(o_parts) + [xz]
    if latent_only:
        assert not any(isinstance(rt, tuple) for rt in tensors)
        out_tiles = tiles - 1
        src = lambda i: (i // out_tiles) * tiles + 1 + i % out_tiles
        mrow = lambda i: i // out_tiles
    else:
        out_tiles = tiles
        src = lambda i: i
        mrow = lambda i: _mod_row(i, tiles, batch)
    assert (batch * out_tiles) % 2 == 0
    tile_fn = lambda t: lambda i: src(2 * i + t)
    in_specs = [spec for rt in tensors for t in range(2) for spec in _row_specs(rt, tile_fn(t), tiles)]
    in_specs += [pl.BlockSpec((None, 6, d), (lambda t: lambda i: (mrow(2 * i + t), 0, 0))(t)) for t in range(2)]
    in_specs += [_resident((1, d))]
    in_specs += [_resident(w.shape) for w in wo_parts]
    in_specs += [_resident(wi.shape), _resident(wf.shape)]
    row_args = [a for rt in tensors for _ in range(2) for a in _row_sources(rt)]
    return pl.pallas_call(
        functools.partial(_post_kernel, n_src=tuple(len(_row_sources(rt)) for rt in tensors), tiles=tiles),
        out_shape=jax.ShapeDtypeStruct((batch * out_tiles * ROW_TILE, d), F32),
        grid=(batch * out_tiles // 2,),
        in_specs=in_specs,
        out_specs=pl.BlockSpec((2 * ROW_TILE, d), lambda i: (i, 0)),
        compiler_params=_cparams(("parallel",)),
        name="post",
    )(*row_args, mod, mod, gain.reshape(1, d), *wo_parts, wi, wf)


def _attn_kernel(qc_ref, *refs, lambda_init, seq_all):
    q_refs = refs[:Q_TILES_PER_STEP]
    (k_ref, v_ref, qg_ref, kg_ref, sg_ref, lam_ref, octx_ref, olat_ref, kb_ref, vt_ref) = refs[Q_TILES_PER_STEP:]
    step = pl.program_id(2)

    @pl.when(step == 0)
    def _prep_keys():
        def body(r, carry):
            rows = pl.ds(pl.multiple_of(r * Q_TILE, Q_TILE), Q_TILE)
            kb_ref[rows, :] = k_ref[rows, :].astype(BF16)
            vt_ref[0:LANES, rows] = v_ref[rows, :].T.astype(BF16)
            return carry
        lax.fori_loop(0, seq_all // Q_TILE, body, 0, unroll=3)
        vt_ref[LANES:, :] = jnp.ones((SUM_ROWS, seq_all), BF16)

    lam_v = lam_ref[...]
    e1 = jnp.exp(jnp.sum(lam_v[0:1, :] * lam_v[1:2, :], axis=-1, keepdims=True))
    e2 = jnp.exp(jnp.sum(lam_v[2:3, :] * lam_v[3:4, :], axis=-1, keepdims=True))
    lam = e1 - e2 + lambda_init

    def scores(q_ref, n_keys):
        qn = q_ref[...]
        lane = lax.broadcasted_iota(jnp.int32, qn.shape, 1)
        q2 = jnp.concatenate([jnp.where(lane < A_DQK, qn, 0.0).astype(BF16),
                              jnp.where(lane >= A_DQK, qn, 0.0).astype(BF16)], axis=0)
        return _dot_nt(kb_ref[0:n_keys, :], q2)

    def outputs(s, n_keys, shift):
        if shift is None:
            shift = jnp.max(s, axis=0, keepdims=True)
        p = jnp.exp2(s - shift).astype(BF16)
        pv = _dot(vt_ref[:, 0:n_keys], p)
        pv = pv[0:LANES] / pv[LANES:LANES + 1]
        o = pv[:, :Q_TILE] - lam * pv[:, Q_TILE:]
        var = jnp.mean(o * o, axis=0, keepdims=True)
        o = o * lax.rsqrt(var + EPS) * (sg_ref[...] * (1.0 - lambda_init))
        return o.T.astype(BF16)

    def run(shift):
        s_next = scores(q_refs[0], seq_all)
        for t in range(Q_TILES_PER_STEP):
            s = s_next
            if t + 1 < Q_TILES_PER_STEP:
                s_next = scores(q_refs[t + 1], seq_all)
            olat_ref[t * Q_TILE:(t + 1) * Q_TILE, :] = outputs(s, seq_all, shift)

        @pl.when(step == 0)
        def _context_tile():
            octx_ref[...] = outputs(scores(qc_ref, CTX_LEN), CTX_LEN, shift)

    bound = (A_DQK ** 0.5 * LOG2E) * (jnp.max(jnp.abs(qg_ref[...]), axis=-1, keepdims=True)
                                      * jnp.max(jnp.abs(kg_ref[...]), axis=-1, keepdims=True)) + 1.0
    small = jnp.max(bound) < SAFE_SCORE_LOG2
    pl.when(small)(lambda: run(bound))
    pl.when(jnp.logical_not(small))(lambda: run(None))


def _attn(p3, qk_gain, lam, sub_gain, lambda_init):
    batch, seq_all, _ = p3.shape
    seq = seq_all - CTX_LEN
    n_t = Q_TILES_PER_STEP
    assert seq % (n_t * Q_TILE) == 0
    qg = jnp.tile(qk_gain[0], 2).reshape(1, LANES)
    kg = jnp.tile(qk_gain[1], 2).reshape(1, LANES)
    kern = functools.partial(_attn_kernel, lambda_init=lambda_init, seq_all=seq_all)
    q_spec = lambda t: pl.BlockSpec((None, Q_TILE, LANES), lambda b, h, i: (b, 1 + n_t * i + t, h))
    o_ctx, o_lat = pl.pallas_call(
        kern,
        out_shape=(jax.ShapeDtypeStruct((batch, CTX_LEN, A_HEADS * LANES), BF16),
                   jax.ShapeDtypeStruct((batch, seq, A_HEADS * LANES), BF16)),
        grid=(batch, A_HEADS, seq // (n_t * Q_TILE)),
        in_specs=[pl.BlockSpec((None, Q_TILE, LANES), lambda b, h, i: (b, 0, h))]
        + [q_spec(t) for t in range(n_t)] + [
            pl.BlockSpec((None, seq_all, LANES), lambda b, h, i: (b, 0, A_HEADS + h)),
            pl.BlockSpec((None, seq_all, LANES), lambda b, h, i: (b, 0, 2 * A_HEADS + h)),
            _resident((1, LANES)), _resident((1, LANES)), _resident((LANES, 1)), _resident((4, A_DQK)),
        ],
        out_specs=(pl.BlockSpec((None, CTX_LEN, LANES), lambda b, h, i: (b, 0, h)),
                   pl.BlockSpec((None, n_t * Q_TILE, LANES), lambda b, h, i: (b, i, h))),
        scratch_shapes=[pltpu.VMEM((seq_all, LANES), BF16), pltpu.VMEM((LANES + SUM_ROWS, seq_all), BF16)],
        compiler_params=_cparams(("parallel", "parallel", "arbitrary")),
        name="attn",
    )(*([p3] * (n_t + 3)), qg, kg, sub_gain.reshape(LANES, 1), lam)
    return o_ctx, o_lat


MATMUL_LEVELS = 2
LOCAL_BLOCK = 64
N_STAGED = 5
SAFE_LOG2 = 100.0


def _scan_levels():
    return int(math.log2(SCAN_CHUNK))


def _scan_consts(n_heads):
    c = SCAN_CHUNK
    t = np.arange(c)[:, None]
    u = np.arange(c)[None, :]
    blocks_f = [u <= t]
    blocks_b = [u >= t]
    for lvl in range(1, MATMUL_LEVELS + 1):
        h = 1 << (lvl - 1)
        m = (t // (2 * h)) * (2 * h) + h
        second = (t % (2 * h)) >= h
        blocks_f.append(np.where(second, (u >= m) & (u <= t), (u > t) & (u <= m - 1)))
        blocks_b.append(np.where(second, (u >= m) & (u < t), (u >= t) & (u <= m - 1)))
    a_f = np.tile(np.concatenate(blocks_f, axis=0).astype(np.float32), (1, 2))
    a_b = np.tile(np.concatenate(blocks_b, axis=0).astype(np.float32), (1, 2))
    x = t ^ u
    lv = np.where(x > 0, np.floor(np.log2(np.maximum(x, 1))).astype(np.int32) + 1, 0)
    lv_f = np.where(u <= t, lv, -1).astype(np.int32)
    lv_b = np.where(u >= t, lv, -1).astype(np.int32)
    lv_f = np.tile(lv_f, (n_heads, 1))
    lv_b = np.tile(lv_b, (n_heads, 1))
    return (jnp.asarray(a_f, BF16), jnp.asarray(a_b, BF16), jnp.asarray(lv_f), jnp.asarray(lv_b))


def _scan_sums(g, a):
    g2 = g * LOG2E
    g_hi = g2.astype(BF16)
    g_lo = (g2 - g_hi.astype(F32)).astype(BF16)
    return _dot(a, jnp.concatenate([g_hi, g_lo], axis=0))


def _level_exponent(b, lvl, backward):
    c = SCAN_CHUNK
    h = 1 << (lvl - 1)
    r0 = h if backward else h - 1
    b3 = b.reshape(c // (2 * h), 2 * h, LANES)
    ref = b3[:, r0:r0 + 1, :]
    if h % 8 == 0:
        first, second = b3[:, :h, :], b3[:, h:, :]
        x = jnp.concatenate([first - ref, ref - second] if backward else [ref - first, second - ref], axis=1)
    else:
        x = -jnp.abs(b3 - ref)
    return x.reshape(c, LANES)


def _by_head(xb, n_heads):
    if n_heads == 1:
        return xb
    dk = LANES // n_heads
    lane = lax.broadcasted_iota(jnp.int32, xb.shape, 1)
    masks = [jnp.where((lane >= hh * dk) & (lane < (hh + 1) * dk), 1.0, 0.0).astype(BF16) for hh in range(n_heads)]
    return jnp.concatenate([xb * hm for hm in masks], axis=0)


def _head_block_mask(shape, row_block, col_block):
    r = lax.broadcasted_iota(jnp.int32, shape, 0) // row_block
    cc = lax.broadcasted_iota(jnp.int32, shape, 1) // col_block
    return r == cc


def _level_chunk(qb, kb, b, fine, vb, st, lv_ref, *, n_heads, backward):
    c = SCAN_CHUNK
    nl = _scan_levels()
    edge = b[0:1, :] if backward else b[c - 1:c, :]
    e_in = jnp.exp2(b).astype(BF16)
    e_out = jnp.exp2(edge - b).astype(BF16)

    def level_decay(lvl):
        if lvl <= MATMUL_LEVELS:
            return jnp.exp2(fine[(lvl - 1) * c:lvl * c]).astype(BF16)
        return jnp.exp2(_level_exponent(b, lvl, backward)).astype(BF16)

    o = _dot_nt(qb * e_in, st.astype(BF16))

    lv = lv_ref[...]
    att = jnp.zeros((n_heads * c, c), F32)
    for lvl in range(nl + 1):
        if lvl == 0:
            qh, kh = qb, kb
        else:
            el = level_decay(lvl)
            qh, kh = qb * el, kb * el
        att = jnp.where(lv == lvl, _dot_nt(_by_head(qh, n_heads), kh), att)
    attb = att.astype(BF16)
    intra = [_dot(attb[hh * c:(hh + 1) * c], vb[:, hh * LANES:(hh + 1) * LANES]) for hh in range(n_heads)]
    o = o + (intra[0] if n_heads == 1 else jnp.concatenate(intra, axis=1))

    dec = jnp.exp2(edge)
    upd = _dot_tn(vb, kb * e_out)
    if n_heads > 1:
        upd = jnp.where(_head_block_mask(upd.shape, LANES, LANES // n_heads), upd, 0.0)
    return o, dec * st + upd


def _shortcut_pairs(qp, kp, qt, kd, vb, lv_ref, *, n_heads):
    lv = lv_ref[...]
    rows = lv.shape[0]
    both = _dot_nt(jnp.concatenate([_by_head(qp, n_heads), _by_head(qt, n_heads)], axis=0), kp)
    att = jnp.where(lv == _scan_levels(), both[rows:], jnp.where(lv >= 0, both[:rows], 0.0)).astype(BF16)
    upd = _dot_tn(kd, vb)
    if n_heads > 1:
        upd = jnp.where(_head_block_mask(upd.shape, LANES // n_heads, LANES), upd, 0.0)
    return att, upd


def _shortcut_output(att, upd, qe, vb, dec, s, *, n_heads):
    c = SCAN_CHUNK
    sb = s.astype(BF16)
    outs = []
    for hh in range(n_heads):
        cols = slice(hh * LANES, (hh + 1) * LANES)
        lhs = jnp.concatenate([att[hh * c:(hh + 1) * c], qe], axis=1)
        rhs = jnp.concatenate([vb[:, cols], sb[:, cols]], axis=0)
        outs.append(_dot(lhs, rhs))
    o = outs[0] if n_heads == 1 else jnp.concatenate(outs, axis=1)
    if n_heads > 1:
        dec = jnp.concatenate([dec] * n_heads, axis=1)
    return o, dec * s + upd


def _scan_loop(load_common, load_dir, out_refs, a_refs, lv_refs, v_s, stage_refs, *, n_heads, seq_all):
    c = SCAN_CHUNK
    n = seq_all // c
    n_ctx = CTX_LEN // c

    def chunk_rows(j):
        return pl.ds(pl.multiple_of(j * c, c), c)

    def scan(prepare, finish, state0, unroll):
        def body(i, sts):
            sts = list(sts)
            todo = []
            for u in range(unroll):
                j = i * unroll + u
                jb = jnp.where(j < n_ctx, n_ctx - 1 - j, n - 1 - (j - n_ctx))
                for d, jj in ((0, j), (1, jb)):
                    rows = chunk_rows(jj)
                    todo.append((d, rows, prepare(d, rows)))
            for d, rows, values in todo:
                out_refs[d][rows, :], sts[d] = finish(d, rows, values, sts[d])
            return tuple(sts)
        lax.fori_loop(0, n // unroll, body, (state0, state0))

    def stage(j, worst):
        gated = []
        for u in range(STAGE_UNROLL):
            rows = chunk_rows(j * STAGE_UNROLL + u)
            q, vb = load_common(rows)
            v_s[rows, :] = vb
            gated.append((rows, q, [load_dir[d](rows) for d in (0, 1)]))
        summed = [(rows, q, [(k, _scan_sums(g, a_refs[d][0:c, :])) for d, (k, g) in enumerate(kg)])
                  for rows, q, kg in gated]
        for rows, q, kb in summed:
            for d, (k, b) in enumerate(kb):
                b3 = b.reshape(c // LOCAL_BLOCK, LOCAL_BLOCK, LANES)
                r0 = LOCAL_BLOCK // 2 if d else LOCAL_BLOCK // 2 - 1
                ref = b3[:, r0:r0 + 1, :]
                x = b3 - ref
                edge = b[0:1, :] if d else b[c - 1:c, :]
                qp = q * jnp.exp2(x).reshape(c, LANES)
                kp = k * jnp.exp2(-x).reshape(c, LANES)
                entry = jnp.broadcast_to(jnp.exp2(ref), b3.shape).reshape(c, LANES)
                leave = jnp.broadcast_to(jnp.exp2(edge - ref), b3.shape).reshape(c, LANES)
                cross = jnp.exp2(ref[0] - ref[1] if d else ref[1] - ref[0])
                qp_s, kp_s, qt_s, qe_s, kd_s, dec_s = stage_refs[d]
                qp_s[rows, :] = qp.astype(BF16)
                kp_s[rows, :] = kp.astype(BF16)
                qt_s[rows, :] = (qp * cross).astype(BF16)
                qe_s[rows, :] = (qp * entry).astype(BF16)
                kd_s[rows, :] = (kp * leave).astype(BF16)
                dec_s[rows, :] = jnp.broadcast_to(jnp.exp2(edge), (LANES, LANES)).T
                worst = jnp.maximum(worst, jnp.max(jnp.abs(x).reshape(c // 8, 8, LANES), axis=0))
        return worst
    worst = lax.fori_loop(0, n // STAGE_UNROLL, stage, jnp.zeros((8, LANES), F32))
    safe = jnp.max(worst) < SAFE_LOG2

    def shortcut_pairs(d, rows):
        qp_s, kp_s, qt_s, _, kd_s, _ = stage_refs[d]
        return _shortcut_pairs(qp_s[rows, :], kp_s[rows, :], qt_s[rows, :], kd_s[rows, :],
                               v_s[rows, :], lv_refs[d], n_heads=n_heads)

    def shortcut_output(d, rows, values, s):
        _, _, _, qe_s, _, dec_s = stage_refs[d]
        return _shortcut_output(*values, qe_s[rows, :], v_s[rows, :], dec_s[rows, :], s, n_heads=n_heads)

    def levels(d, rows, values, st):
        q, vb = load_common(rows)
        k, g = load_dir[d](rows)
        sums = _scan_sums(g, a_refs[d][...])
        return _level_chunk(q.astype(BF16), k.astype(BF16), sums[0:c], sums[c:], vb, st, lv_refs[d],
                            n_heads=n_heads, backward=bool(d))

    pl.when(safe)(lambda: scan(shortcut_pairs, shortcut_output, jnp.zeros((LANES, n_heads * LANES), F32), SCAN_UNROLL))
    pl.when(jnp.logical_not(safe))(lambda: scan(lambda d, rows: None, levels, jnp.zeros((n_heads * LANES, LANES), F32), 1))


def _scan_scratch(seq_all, n_heads):
    assert SCAN_CHUNK == LANES == 2 * LOCAL_BLOCK
    width = n_heads * LANES
    out = [pltpu.VMEM((seq_all, width), F32)] * 2
    stage = [pltpu.VMEM((seq_all, LANES), BF16)] * N_STAGED + [pltpu.VMEM((seq_all, LANES), F32)]
    return out + [pltpu.VMEM((seq_all, width), BF16)] + stage * 2


def _log_sigmoid(x):
    return jnp.minimum(x, 0.0) - jnp.log(1.0 + jnp.exp(-jnp.abs(x)))


def _gla_kernel(q_ref, k_ref, v_ref, g_ref, lr_ref, wgf_ref, wgb_ref, bf_ref, bb_ref, gain_ref,
                af_ref, ab_ref, lvf_ref, lvb_ref, o_ref, of_ref, ob_ref, v_s, *stage_refs, seq_all):
    def load_common(rows):
        return q_ref[rows, :] * (B_DK ** -0.5), v_ref[rows, :].astype(BF16)

    def load_dir(wg_ref, b_ref):
        def load(rows):
            pre = _dot(lr_ref[rows, :].astype(BF16), wg_ref[...]) + b_ref[...]
            return k_ref[rows, :], _log_sigmoid(pre) * (1.0 / B_GATE_NORM)
        return load

    _scan_loop(load_common, (load_dir(wgf_ref, bf_ref), load_dir(wgb_ref, bb_ref)), (of_ref, ob_ref),
               (af_ref, ab_ref), (lvf_ref, lvb_ref), v_s, (stage_refs[:N_STAGED + 1], stage_refs[N_STAGED + 1:]), n_heads=2, seq_all=seq_all)

    def post(j, carry):
        rows = pl.ds(pl.multiple_of(j * ROW_TILE, ROW_TILE), ROW_TILE)
        for hh in range(2):
            cols = slice(hh * LANES, (hh + 1) * LANES)
            o = of_ref[rows, cols] + ob_ref[rows, cols]
            var = jnp.mean(o * o, axis=-1, keepdims=True)
            o_ref[rows, cols] = (o * lax.rsqrt(var + EPS) * gain_ref[...] * _silu(g_ref[rows, cols])).astype(BF16)
        return carry
    lax.fori_loop(0, seq_all // ROW_TILE, post, 0, unroll=3)


def _gla(p3, w_gate_up, b_gate_up, gain, consts):
    batch, seq_all, _ = p3.shape
    pairs = B_HEADS // 2
    wg = jnp.zeros((2, pairs, LANES, LANES), F32)
    for d in range(2):
        w = w_gate_up[d].reshape(B_GATE_RANK, pairs, LANES).transpose(1, 0, 2)
        wg = wg.at[d, :, d * B_GATE_RANK:(d + 1) * B_GATE_RANK, :].set(w)
    wg = wg.astype(BF16)
    bias = b_gate_up.reshape(2, pairs, 1, LANES)
    a_f, a_b, lv_f, lv_b = consts
    col = lambda off: (lambda b, p: (b, 0, off + p))
    q_off = 3 * A_HEADS * LANES // LANES
    k_off = q_off + B_HEADS * B_DK // LANES
    v_off = (k_off * LANES + B_HEADS * B_DK) // (2 * LANES)
    g_off = v_off + pairs
    lr_off = (g_off + pairs) * 2
    wspec = pl.BlockSpec((None, LANES, LANES), lambda b, p: (p, 0, 0))
    bspec = pl.BlockSpec((None, 1, LANES), lambda b, p: (p, 0, 0))
    return pl.pallas_call(
        functools.partial(_gla_kernel, seq_all=seq_all),
        out_shape=jax.ShapeDtypeStruct((batch, seq_all, B_HEADS * LANES), BF16),
        grid=(batch, pairs),
        in_specs=[
            pl.BlockSpec((None, seq_all, LANES), col(q_off)),
            pl.BlockSpec((None, seq_all, LANES), col(k_off)),
            pl.BlockSpec((None, seq_all, 2 * LANES), col(v_off)),
            pl.BlockSpec((None, seq_all, 2 * LANES), col(g_off)),
            pl.BlockSpec((None, seq_all, LANES), lambda b, p: (b, 0, lr_off)),
            wspec, wspec, bspec, bspec,
            _resident((1, LANES)),
            _resident(a_f.shape), _resident(a_b.shape), _resident(lv_f.shape), _resident(lv_b.shape),
        ],
        out_specs=pl.BlockSpec((None, seq_all, 2 * LANES), lambda b, p: (b, 0, p)),
        scratch_shapes=_scan_scratch(seq_all, 2),
        compiler_params=_cparams(("parallel", "parallel")),
        name="gla",
    )(p3, p3, p3, p3, p3, wg[0], wg[1], bias[0], bias[1], gain.reshape(1, LANES), a_f, a_b, lv_f, lv_b)


def _hgrn_kernel(q_ref, ff_ref, fb_ref, i_ref, g_ref, lb_ref, gain_ref,
                 af_ref, ab_ref, lvf_ref, lvb_ref, o_ref, of_ref, ob_ref, v_s, *stage_refs, seq_all, layer):
    def lower_bound(d):
        raw = lb_ref[d]
        ex = jnp.exp(raw - jnp.max(raw, axis=0, keepdims=True))
        p = ex / jnp.sum(ex, axis=0, keepdims=True)
        return jnp.sum(p[1:layer + 1], axis=0, keepdims=True)

    def load_common(rows):
        return _silu(q_ref[rows, :]) * (C_DH ** -0.5), i_ref[rows, :].astype(BF16)

    def load_dir(f_ref, lb):
        def load(rows):
            f = lb + (1.0 - lb) * jax.nn.sigmoid(f_ref[rows, :])
            return 1.0 - f, jnp.log(f)
        return load

    _scan_loop(load_common, (load_dir(ff_ref, lower_bound(0)), load_dir(fb_ref, lower_bound(1))), (of_ref, ob_ref),
               (af_ref, ab_ref), (lvf_ref, lvb_ref), v_s, (stage_refs[:N_STAGED + 1], stage_refs[N_STAGED + 1:]), n_heads=1, seq_all=seq_all)

    def post(j, carry):
        rows = pl.ds(pl.multiple_of(j * ROW_TILE, ROW_TILE), ROW_TILE)
        o = of_ref[rows, :] + ob_ref[rows, :]
        var = jnp.mean(o * o, axis=-1, keepdims=True)
        o_ref[rows, :] = (o * lax.rsqrt(var + EPS) * gain_ref[...] * _silu(g_ref[rows, :])).astype(BF16)
        return carry
    lax.fori_loop(0, seq_all // ROW_TILE, post, 0, unroll=3)


def _hgrn(p3, lb_raw, gain, consts, layer):
    batch, seq_all, _ = p3.shape
    a_f, a_b, lv_f, lv_b = consts
    col = lambda off: (lambda b, h: (b, 0, off + h))
    return pl.pallas_call(
        functools.partial(_hgrn_kernel, seq_all=seq_all, layer=layer),
        out_shape=jax.ShapeDtypeStruct((batch, seq_all, C_HEADS * LANES), BF16),
        grid=(batch, C_HEADS),
        in_specs=[pl.BlockSpec((None, seq_all, LANES), col(s * C_HEADS)) for s in range(5)] + [
            pl.BlockSpec((2, DEPTH, LANES), lambda b, h: (0, 0, h)),
            _resident((1, LANES)),
            _resident(a_f.shape), _resident(a_b.shape), _resident(lv_f.shape), _resident(lv_b.shape),
        ],
        out_specs=pl.BlockSpec((None, seq_all, LANES), lambda b, h: (b, 0, h)),
        scratch_shapes=_scan_scratch(seq_all, 1),
        compiler_params=_cparams(("parallel", "parallel")),
        name="hgrn",
    )(p3, p3, p3, p3, p3, lb_raw, gain.reshape(1, LANES), a_f, a_b, lv_f, lv_b)


def _rope_tables(seq):
    pos = jnp.arange(seq)
    row_ids = (pos // GRID_W).astype(F32)
    col_ids = (pos % GRID_W).astype(F32)
    n_axis = A_DQK // 2
    inv = ROPE_BASE ** (-jnp.arange(0, n_axis, 2, dtype=F32) / n_axis)
    ang_r = row_ids[:, None] * inv
    ang_c = col_ids[:, None] * inv
    zeros = jnp.zeros_like(ang_r)
    cos64 = jnp.concatenate([jnp.cos(ang_r)] * 2 + [jnp.cos(ang_c)] * 2, axis=-1)
    sa64 = jnp.concatenate([-jnp.sin(ang_r), zeros, -jnp.sin(ang_c), zeros], axis=-1)
    sb64 = jnp.concatenate([zeros, jnp.sin(ang_r), zeros, jnp.sin(ang_c)], axis=-1)
    ident = lambda v: jnp.full((CTX_LEN, LANES), v, F32)
    full = lambda t, v: jnp.concatenate([ident(v), jnp.tile(t, (1, 2))], axis=0)
    return full(cos64, 1.0), full(sa64, 0.0), full(sb64, 0.0)


def kernel(x, c, ctx, c_ctx, w_ada, b_ada, norm1_gain, norm2_gain, w_in_even, qk_gain_a, lambda_a, subln_gain_a, w_gate_up_b, b_gate_up_b, onorm_gain_b, w_out_even, w_in_odd, lb_raw_c, onorm_gain_c, w_out_odd, w_ffn_in, w_ffn_out):
    batch, seq, d = x.shape
    seq_all = CTX_LEN + seq
    assert d == D_MODEL and ctx.shape[1] == CTX_LEN == ROW_TILE == Q_TILE and seq % ROW_TILE == 0

    tiles = seq_all // ROW_TILE
    flat = lambda a: a.reshape(-1, a.shape[-1])
    xz = (flat(ctx), flat(x))
    mod_rows = -(-(batch + 1) // 8) * 8
    cc = jnp.zeros((mod_rows, d), F32).at[:batch].set(c).at[batch].set(c_ctx)
    mods = _ada(cc, w_ada, b_ada).reshape(DEPTH, mod_rows, 6, d)

    tabs = _rope_tables(seq)
    half = np.arange(LANES) // A_DQK
    bd = jnp.asarray((half[:, None] == half[None, :]) / A_DQK, BF16)
    consts_gla = _scan_consts(2)
    consts_hgrn = _scan_consts(1)

    for l in range(DEPTH):
        j = l // 2
        last = l == DEPTH - 1
        wi = w_ffn_in[l].astype(BF16)
        wf = w_ffn_out[l].astype(BF16)
        if l % 2 == 0:
            w = jnp.pad(w_in_even[j], ((0, 0), (0, EVEN_COLS_PAD - EVEN_COLS))).astype(BF16)
            p3 = _inproj(xz, mods[l], norm1_gain[l], w, batch, tiles,
                         qk=(tabs, qk_gain_a[j][0], qk_gain_a[j][1], bd)).reshape(batch, seq_all, EVEN_COLS_PAD)
            lambda_init = 0.8 - 0.6 * math.exp(-0.3 * l)
            oa_ctx, oa_lat = _attn(p3, qk_gain_a[j], lambda_a[j], subln_gain_a[j], lambda_init)
            ob = _gla(p3, w_gate_up_b[j], b_gate_up_b[j], onorm_gain_b[j], consts_gla)
            a_width = A_HEADS * LANES
            wo = w_out_even[j].astype(BF16)
            o_parts = [(flat(oa_ctx), flat(oa_lat)), flat(ob)]
            wo_parts = [wo[:a_width], wo[a_width:]]
        else:
            p3 = _inproj(xz, mods[l], norm1_gain[l], w_in_odd[j].astype(BF16), batch, tiles).reshape(batch, seq_all, ODD_COLS)
            oc = _hgrn(p3, lb_raw_c, onorm_gain_c[j], consts_hgrn, l)
            o_parts = [flat(oc)]
            wo_parts = [w_out_odd[j].astype(BF16)]
        xz = _post(o_parts, xz, mods[l], norm2_gain[l], wo_parts, wi, wf, batch, tiles, last)
    return xz.reshape(batch, seq, d)
```

```python
import functools
import math

import numpy as np
import jax
import jax.numpy as jnp
from jax import lax
from jax.experimental import pallas as pl
from jax.experimental.pallas import tpu as pltpu

F32 = jnp.float32
BF16 = jnp.bfloat16

D_MODEL = 1024
DEPTH = 4
CTX_LEN = 256
GRID_W = 64
A_HEADS = 4
A_DQK = 64
B_HEADS = 4
B_DK = 64
B_GATE_RANK = 16
B_GATE_NORM = 16.0
C_HEADS = 8
C_DH = 128
ROPE_BASE = 10000.0
EPS = 1e-6
D_FF = 2816
EVEN_COLS = 3104
EVEN_COLS_PAD = 3200
ODD_COLS = 5120

LANES = 128
ROW_TILE = 256
COL_CHUNK = 512
FF_CHUNK = 256
Q_TILE = 256
Q_TILES_PER_STEP = 4
SUM_ROWS = 16
SAFE_SCORE_LOG2 = 60.0
LOG2E = math.log2(math.e)
SCAN_CHUNK = 128
SCAN_UNROLL = 9
STAGE_UNROLL = 6
VMEM_LIMIT = 52 * 1024 * 1024


def _cparams(sem):
    return pltpu.CompilerParams(dimension_semantics=sem, vmem_limit_bytes=VMEM_LIMIT)


def _resident(shape):
    nd = len(shape)
    return pl.BlockSpec(shape, lambda *_: (0,) * nd, pipeline_mode=pl.Buffered(1))


def _silu(x):
    return x * jax.nn.sigmoid(x)


def _dot(a, b):
    return jnp.dot(a, b, preferred_element_type=F32)


def _dot_nt(a, b):
    return lax.dot_general(a, b, (((1,), (1,)), ((), ())), preferred_element_type=F32)


def _dot_tn(a, b):
    return lax.dot_general(a, b, (((0,), (0,)), ((), ())), preferred_element_type=F32)


def _ada_kernel(c_ref, w_ref, b_ref, o_ref):
    s = _silu(c_ref[...]).astype(BF16)
    o_ref[...] = _dot(s, w_ref[...].astype(BF16)) + b_ref[...]


def _ada(cc, w_ada, b_ada):
    depth, d, n = w_ada.shape
    rows = cc.shape[0]
    tn = 1536
    return pl.pallas_call(
        _ada_kernel,
        out_shape=jax.ShapeDtypeStruct((depth, rows, n), F32),
        grid=(depth, n // tn),
        in_specs=[
            pl.BlockSpec((rows, d), lambda l, j: (0, 0)),
            pl.BlockSpec((None, d, tn), lambda l, j: (l, 0, j)),
            pl.BlockSpec((None, 1, tn), lambda l, j: (l, 0, j)),
        ],
        out_specs=pl.BlockSpec((None, rows, tn), lambda l, j: (l, 0, j)),
        compiler_params=_cparams(("parallel", "parallel")),
        name="ada",
    )(cc, w_ada, b_ada.reshape(depth, 1, n))


def _norm_mod(x, gain, shift, scale):
    var = jnp.mean(x * x, axis=-1, keepdims=True)
    return (x * lax.rsqrt(var + EPS) * gain) * (1.0 + scale) + shift


def _row_sources(rt):
    return list(rt) if isinstance(rt, tuple) else [rt]


def _row_specs(rt, tile_fn, tiles):
    if not isinstance(rt, tuple):
        return [pl.BlockSpec((ROW_TILE, rt.shape[1]), lambda i: (tile_fn(i), 0))]
    sample = lambda i: tile_fn(i) // tiles
    latent = lambda i: sample(i) * (tiles - 1) + jnp.maximum(tile_fn(i) % tiles - 1, 0)
    return [pl.BlockSpec((ROW_TILE, rt[0].shape[1]), lambda i: (sample(i), 0)),
            pl.BlockSpec((ROW_TILE, rt[1].shape[1]), lambda i: (latent(i), 0))]


def _row_tile(refs, is_context):
    if len(refs) == 1:
        return refs[0][...]
    return jnp.where(is_context, refs[0][...], refs[1][...])


def _take(refs, counts):
    groups, pos = [], 0
    for n in counts:
        groups.append(refs[pos:pos + n])
        pos += n
    return groups, refs[pos:]


def _inproj_kernel(*refs, n_x, tiles, qk_heads):
    x_groups, rest = _take(refs, [n_x, n_x])
    mod_a_ref, mod_b_ref, gain_ref, w_ref = rest[:4]
    o_ref = rest[-1]
    step = pl.program_id(0)
    halves = []
    for t, m in enumerate((mod_a_ref, mod_b_ref)):
        x = _row_tile(x_groups[t], (2 * step + t) % tiles == 0)
        halves.append(_norm_mod(x, gain_ref[...], m[0:1, :], m[1:2, :]).astype(BF16))
    h = jnp.concatenate(halves, axis=0)
    ncols = w_ref.shape[1]
    for c0 in range(0, ncols, COL_CHUNK):
        c1 = min(c0 + COL_CHUNK, ncols)
        res = _dot(h, w_ref[:, c0:c1])
        if c1 > 2 * qk_heads * LANES:
            o_ref[:, c0:c1] = res
            continue
        c_ref, sa_ref, sb_ref, qg_ref, kg_ref, bd_ref = rest[4:10]
        gains = (qg_ref[...] * (A_DQK ** -0.5 * LOG2E), kg_ref[...])
        blocks = [res[:, b0 - c0:b0 - c0 + LANES] for b0 in range(c0, c1, LANES)]
        ms_all = _dot(jnp.concatenate([(x * x).astype(BF16) for x in blocks], axis=0), bd_ref[...])
        n_rows = 2 * ROW_TILE
        for i, x in enumerate(blocks):
            b0 = c0 + i * LANES
            y = x * lax.rsqrt(ms_all[i * n_rows:(i + 1) * n_rows] + EPS) * gains[b0 // (qk_heads * LANES)]
            for t in range(2):
                pos = pl.ds(pl.multiple_of(((2 * step + t) % tiles) * ROW_TILE, ROW_TILE), ROW_TILE)
                rows = slice(t * ROW_TILE, (t + 1) * ROW_TILE)
                yt = y[rows]
                o_ref[rows, b0:b0 + LANES] = (yt * c_ref[pos, :] + pltpu.roll(yt, LANES - 16, 1) * sa_ref[pos, :]
                                              + pltpu.roll(yt, 16, 1) * sb_ref[pos, :])


def _mod_row(i, tiles, batch):
    return jnp.where(i % tiles == 0, batch, i // tiles)


def _inproj(xz, mod, gain, w, batch, tiles, qk=None):
    d, ncols = w.shape
    rows = batch * tiles * ROW_TILE
    mod_spec = lambda t: pl.BlockSpec((None, 6, d), lambda i: (_mod_row(2 * i + t, tiles, batch), 0, 0))
    x_specs = [spec for t in range(2) for spec in _row_specs(xz, (lambda t: lambda i: 2 * i + t)(t), tiles)]
    qk_args = []
    if qk:
        tabs, q_gain, k_gain, bd = qk
        qk_args = [*tabs, jnp.tile(q_gain, 2).reshape(1, LANES), jnp.tile(k_gain, 2).reshape(1, LANES), bd]
    return pl.pallas_call(
        functools.partial(_inproj_kernel, n_x=len(_row_sources(xz)), tiles=tiles, qk_heads=A_HEADS if qk else 0),
        out_shape=jax.ShapeDtypeStruct((rows, ncols), F32),
        grid=(rows // (2 * ROW_TILE),),
        in_specs=x_specs + [mod_spec(0), mod_spec(1), _resident((1, d)), _resident((d, ncols))]
        + [_resident(a.shape) for a in qk_args],
        out_specs=pl.BlockSpec((2 * ROW_TILE, ncols), lambda i: (i, 0)),
        compiler_params=_cparams(("parallel",)),
        name="inproj",
    )(*(_row_sources(xz) * 2), mod, mod, gain.reshape(1, d), w, *qk_args)


def _post_kernel(*refs, n_src, tiles):
    groups, rest = _take(refs, [n for n in n_src for _ in range(2)])
    n_parts = len(n_src) - 1
    mod_refs, gain_ref = rest[0:2], rest[2]
    wo_refs = rest[3:3 + n_parts]
    wi_ref, wf_ref, out_ref = rest[3 + n_parts:]
    step = pl.program_id(0)
    is_context = [(2 * step + t) % tiles == 0 for t in range(2)]
    tile = lambda p, t: _row_tile(groups[2 * p + t], is_context[t])
    x1 = []
    for t in range(2):
        y = None
        for p, wo_ref in enumerate(wo_refs):
            part = _dot(tile(p, t), wo_ref[...])
            y = part if y is None else y + part
        x1.append(tile(n_parts, t) + mod_refs[t][2:3, :] * y)
    h = jnp.concatenate([_norm_mod(x1[t], gain_ref[...], mod_refs[t][3:4, :], mod_refs[t][4:5, :]).astype(BF16)
                         for t in range(2)], axis=0)
    acc = None
    for c0 in range(0, D_FF, FF_CHUNK):
        gate = _dot(h, wi_ref[:, c0:c0 + FF_CHUNK])
        up = _dot(h, wi_ref[:, D_FF + c0:D_FF + c0 + FF_CHUNK])
        a = (_silu(gate) * up).astype(BF16)
        part = _dot(a, wf_ref[c0:c0 + FF_CHUNK, :])
        acc = part if acc is None else acc + part
    for t in range(2):
        rows = slice(t * ROW_TILE, (t + 1) * ROW_TILE)
        out_ref[rows, :] = x1[t] + mod_refs[t][5:6, :] * acc[rows]


def _post(o_parts, xz, mod, gain, wo_parts, wi, wf, batch, tiles, latent_only):
    d = wi.shape[0]
    tensors = list(o_parts) + [xz]
    if latent_only:
        assert not any(isinstance(rt, tuple) for rt in tensors)
        out_tiles = tiles - 1
        src = lambda i: (i // out_tiles) * tiles + 1 + i % out_tiles
        mrow = lambda i: i // out_tiles
    else:
        out_tiles = tiles
        src = lambda i: i
        mrow = lambda i: _mod_row(i, tiles, batch)
    assert (batch * out_tiles) % 2 == 0
    tile_fn = lambda t: lambda i: src(2 * i + t)
    in_specs = [spec for rt in tensors for t in range(2) for spec in _row_specs(rt, tile_fn(t), tiles)]
    in_specs += [pl.BlockSpec((None, 6, d), (lambda t: lambda i: (mrow(2 * i + t), 0, 0))(t)) for t in range(2)]
    in_specs += [_resident((1, d))]
    in_specs += [_resident(w.shape) for w in wo_parts]
    in_specs += [_resident(wi.shape), _resident(wf.shape)]
    row_args = [a for rt in tensors for _ in range(2) for a in _row_sources(rt)]
    return pl.pallas_call(
        functools.partial(_post_kernel, n_src=tuple(len(_row_sources(rt)) for rt in tensors), tiles=tiles),
        out_shape=jax.ShapeDtypeStruct((batch * out_tiles * ROW_TILE, d), F32),
        grid=(batch * out_tiles // 2,),
        in_specs=in_specs,
        out_specs=pl.BlockSpec((2 * ROW_TILE, d), lambda i: (i, 0)),
        compiler_params=_cparams(("parallel",)),
        name="post",
    )(*row_args, mod, mod, gain.reshape(1, d), *wo_parts, wi, wf)


def _attn_kernel(qc_ref, *refs, lambda_init, seq_all):
    q_refs = refs[:Q_TILES_PER_STEP]
    (k_ref, v_ref, qg_ref, kg_ref, sg_ref, lam_ref, octx_ref, olat_ref, kb_ref, vt_ref) = refs[Q_TILES_PER_STEP:]
    step = pl.program_id(2)

    @pl.when(step == 0)
    def _prep_keys():
        def body(r, carry):
            rows = pl.ds(pl.multiple_of(r * Q_TILE, Q_TILE), Q_TILE)
            kb_ref[rows, :] = k_ref[rows, :].astype(BF16)
            vt_ref[0:LANES, rows] = v_ref[rows, :].T.astype(BF16)
            return carry
        lax.fori_loop(0, seq_all // Q_TILE, body, 0, unroll=3)
        vt_ref[LANES:, :] = jnp.ones((SUM_ROWS, seq_all), BF16)

    lam_v = lam_ref[...]
    e1 = jnp.exp(jnp.sum(lam_v[0:1, :] * lam_v[1:2, :], axis=-1, keepdims=True))
    e2 = jnp.exp(jnp.sum(lam_v[2:3, :] * lam_v[3:4, :], axis=-1, keepdims=True))
    lam = e1 - e2 + lambda_init

    def scores(q_ref, n_keys):
        qn = q_ref[...]
        lane = lax.broadcasted_iota(jnp.int32, qn.shape, 1)
        q2 = jnp.concatenate([jnp.where(lane < A_DQK, qn, 0.0).astype(BF16),
                              jnp.where(lane >= A_DQK, qn, 0.0).astype(BF16)], axis=0)
        return _dot_nt(kb_ref[0:n_keys, :], q2)

    def outputs(s, n_keys, shift):
        if shift is None:
            shift = jnp.max(s, axis=0, keepdims=True)
        p = jnp.exp2(s - shift).astype(BF16)
        pv = _dot(vt_ref[:, 0:n_keys], p)
        pv = pv[0:LANES] / pv[LANES:LANES + 1]
        o = pv[:, :Q_TILE] - lam * pv[:, Q_TILE:]
        var = jnp.mean(o * o, axis=0, keepdims=True)
        o = o * lax.rsqrt(var + EPS) * (sg_ref[...] * (1.0 - lambda_init))
        return o.T.astype(BF16)

    def run(shift):
        s_next = scores(q_refs[0], seq_all)
        for t in range(Q_TILES_PER_STEP):
            s = s_next
            if t + 1 < Q_TILES_PER_STEP:
                s_next = scores(q_refs[t + 1], seq_all)
            olat_ref[t * Q_TILE:(t + 1) * Q_TILE, :] = outputs(s, seq_all, shift)

        @pl.when(step == 0)
        def _context_tile():
            octx_ref[...] = outputs(scores(qc_ref, CTX_LEN), CTX_LEN, shift)

    bound = (A_DQK ** 0.5 * LOG2E) * (jnp.max(jnp.abs(qg_ref[...]), axis=-1, keepdims=True)
                                      * jnp.max(jnp.abs(kg_ref[...]), axis=-1, keepdims=True)) + 1.0
    small = jnp.max(bound) < SAFE_SCORE_LOG2
    pl.when(small)(lambda: run(bound))
    pl.when(jnp.logical_not(small))(lambda: run(None))


def _attn(p3, qk_gain, lam, sub_gain, lambda_init):
    batch, seq_all, _ = p3.shape
    seq = seq_all - CTX_LEN
    n_t = Q_TILES_PER_STEP
    assert seq % (n_t * Q_TILE) == 0
    qg = jnp.tile(qk_gain[0], 2).reshape(1, LANES)
    kg = jnp.tile(qk_gain[1], 2).reshape(1, LANES)
    kern = functools.partial(_attn_kernel, lambda_init=lambda_init, seq_all=seq_all)
    q_spec = lambda t: pl.BlockSpec((None, Q_TILE, LANES), lambda b, h, i: (b, 1 + n_t * i + t, h))
    o_ctx, o_lat = pl.pallas_call(
        kern,
        out_shape=(jax.ShapeDtypeStruct((batch, CTX_LEN, A_HEADS * LANES), BF16),
                   jax.ShapeDtypeStruct((batch, seq, A_HEADS * LANES), BF16)),
        grid=(batch, A_HEADS, seq // (n_t * Q_TILE)),
        in_specs=[pl.BlockSpec((None, Q_TILE, LANES), lambda b, h, i: (b, 0, h))]
        + [q_spec(t) for t in range(n_t)] + [
            pl.BlockSpec((None, seq_all, LANES), lambda b, h, i: (b, 0, A_HEADS + h)),
            pl.BlockSpec((None, seq_all, LANES), lambda b, h, i: (b, 0, 2 * A_HEADS + h)),
            _resident((1, LANES)), _resident((1, LANES)), _resident((LANES, 1)), _resident((4, A_DQK)),
        ],
        out_specs=(pl.BlockSpec((None, CTX_LEN, LANES), lambda b, h, i: (b, 0, h)),
                   pl.BlockSpec((None, n_t * Q_TILE, LANES), lambda b, h, i: (b, i, h))),
        scratch_shapes=[pltpu.VMEM((seq_all, LANES), BF16), pltpu.VMEM((LANES + SUM_ROWS, seq_all), BF16)],
        compiler_params=_cparams(("parallel", "parallel", "arbitrary")),
        name="attn",
    )(*([p3] * (n_t + 3)), qg, kg, sub_gain.reshape(LANES, 1), lam)
    return o_ctx, o_lat


MATMUL_LEVELS = 2
HGRN_LOCAL_BLOCK = 64
GLA_LOCAL_BLOCK = 128
N_STAGED = 5
SAFE_LOG2 = 100.0


def _scan_levels():
    return int(math.log2(SCAN_CHUNK))


def _scan_consts(n_heads):
    c = SCAN_CHUNK
    t = np.arange(c)[:, None]
    u = np.arange(c)[None, :]
    blocks_f = [u <= t]
    blocks_b = [u >= t]
    for lvl in range(1, MATMUL_LEVELS + 1):
        h = 1 << (lvl - 1)
        m = (t // (2 * h)) * (2 * h) + h
        second = (t % (2 * h)) >= h
        blocks_f.append(np.where(second, (u >= m) & (u <= t), (u > t) & (u <= m - 1)))
        blocks_b.append(np.where(second, (u >= m) & (u < t), (u >= t) & (u <= m - 1)))
    a_f = np.tile(np.concatenate(blocks_f, axis=0).astype(np.float32), (1, 2))
    a_b = np.tile(np.concatenate(blocks_b, axis=0).astype(np.float32), (1, 2))
    x = t ^ u
    lv = np.where(x > 0, np.floor(np.log2(np.maximum(x, 1))).astype(np.int32) + 1, 0)
    lv_f = np.where(u <= t, lv, -1).astype(np.int32)
    lv_b = np.where(u >= t, lv, -1).astype(np.int32)
    lv_f = np.tile(lv_f, (n_heads, 1))
    lv_b = np.tile(lv_b, (n_heads, 1))
    return (jnp.asarray(a_f, BF16), jnp.asarray(a_b, BF16), jnp.asarray(lv_f), jnp.asarray(lv_b))


def _scan_sums(g, a):
    g2 = g * LOG2E
    g_hi = g2.astype(BF16)
    g_lo = (g2 - g_hi.astype(F32)).astype(BF16)
    return _dot(a, jnp.concatenate([g_hi, g_lo], axis=0))


def _level_exponent(b, lvl, backward):
    c = SCAN_CHUNK
    h = 1 << (lvl - 1)
    r0 = h if backward else h - 1
    b3 = b.reshape(c // (2 * h), 2 * h, LANES)
    ref = b3[:, r0:r0 + 1, :]
    if h % 8 == 0:
        first, second = b3[:, :h, :], b3[:, h:, :]
        x = jnp.concatenate([first - ref, ref - second] if backward else [ref - first, second - ref], axis=1)
    else:
        x = -jnp.abs(b3 - ref)
    return x.reshape(c, LANES)


def _by_head(xb, n_heads):
    if n_heads == 1:
        return xb
    dk = LANES // n_heads
    lane = lax.broadcasted_iota(jnp.int32, xb.shape, 1)
    masks = [jnp.where((lane >= hh * dk) & (lane < (hh + 1) * dk), 1.0, 0.0).astype(BF16) for hh in range(n_heads)]
    return jnp.concatenate([xb * hm for hm in masks], axis=0)


def _head_block_mask(shape, row_block, col_block):
    r = lax.broadcasted_iota(jnp.int32, shape, 0) // row_block
    cc = lax.broadcasted_iota(jnp.int32, shape, 1) // col_block
    return r == cc


def _level_chunk(qb, kb, b, fine, vb, st, lv_ref, *, n_heads, backward):
    c = SCAN_CHUNK
    nl = _scan_levels()
    edge = b[0:1, :] if backward else b[c - 1:c, :]
    e_in = jnp.exp2(b).astype(BF16)
    e_out = jnp.exp2(edge - b).astype(BF16)

    def level_decay(lvl):
        if lvl <= MATMUL_LEVELS:
            return jnp.exp2(fine[(lvl - 1) * c:lvl * c]).astype(BF16)
        return jnp.exp2(_level_exponent(b, lvl, backward)).astype(BF16)

    o = _dot_nt(qb * e_in, st.astype(BF16))

    lv = lv_ref[...]
    att = jnp.zeros((n_heads * c, c), F32)
    for lvl in range(nl + 1):
        if lvl == 0:
            qh, kh = qb, kb
        else:
            el = level_decay(lvl)
            qh, kh = qb * el, kb * el
        att = jnp.where(lv == lvl, _dot_nt(_by_head(qh, n_heads), kh), att)
    attb = att.astype(BF16)
    intra = [_dot(attb[hh * c:(hh + 1) * c], vb[:, hh * LANES:(hh + 1) * LANES]) for hh in range(n_heads)]
    o = o + (intra[0] if n_heads == 1 else jnp.concatenate(intra, axis=1))

    dec = jnp.exp2(edge)
    upd = _dot_tn(vb, kb * e_out)
    if n_heads > 1:
        upd = jnp.where(_head_block_mask(upd.shape, LANES, LANES // n_heads), upd, 0.0)
    return o, dec * st + upd


def _shortcut_pairs(qp, kp, qt, kd, vb, lv_ref, *, n_heads):
    lv = lv_ref[...]
    rows = lv.shape[0]
    if qt is None:
        att = jnp.where(lv >= 0, _dot_nt(_by_head(qp, n_heads), kp), 0.0).astype(BF16)
    else:
        both = _dot_nt(jnp.concatenate([_by_head(qp, n_heads), _by_head(qt, n_heads)], axis=0), kp)
        att = jnp.where(lv == _scan_levels(), both[rows:], jnp.where(lv >= 0, both[:rows], 0.0)).astype(BF16)
    upd = _dot_tn(kd, vb)
    if n_heads > 1:
        upd = jnp.where(_head_block_mask(upd.shape, LANES // n_heads, LANES), upd, 0.0)
    return att, upd


def _shortcut_output(att, upd, qe, vb, dec, s, *, n_heads):
    c = SCAN_CHUNK
    sb = s.astype(BF16)
    outs = []
    for hh in range(n_heads):
        cols = slice(hh * LANES, (hh + 1) * LANES)
        lhs = jnp.concatenate([att[hh * c:(hh + 1) * c], qe], axis=1)
        rhs = jnp.concatenate([vb[:, cols], sb[:, cols]], axis=0)
        outs.append(_dot(lhs, rhs))
    o = outs[0] if n_heads == 1 else jnp.concatenate(outs, axis=1)
    if n_heads > 1:
        dec = jnp.concatenate([dec] * n_heads, axis=1)
    return o, dec * s + upd


def _scan_loop(load_common, load_dir, out_refs, a_refs, lv_refs, v_s, stage_refs, *, n_heads, seq_all, local_block):
    c = SCAN_CHUNK
    assert local_block in (c, c // 2)
    two_blocks = local_block < c
    n = seq_all // c
    n_ctx = CTX_LEN // c

    def chunk_rows(j):
        return pl.ds(pl.multiple_of(j * c, c), c)

    def scan(prepare, finish, state0, unroll):
        def body(i, sts):
            sts = list(sts)
            todo = []
            for u in range(unroll):
                j = i * unroll + u
                jb = jnp.where(j < n_ctx, n_ctx - 1 - j, n - 1 - (j - n_ctx))
                for d, jj in ((0, j), (1, jb)):
                    rows = chunk_rows(jj)
                    todo.append((d, rows, prepare(d, rows)))
            for d, rows, values in todo:
                out_refs[d][rows, :], sts[d] = finish(d, rows, values, sts[d])
            return tuple(sts)
        lax.fori_loop(0, n // unroll, body, (state0, state0))

    def stage(j, worst):
        gated = []
        for u in range(STAGE_UNROLL):
            rows = chunk_rows(j * STAGE_UNROLL + u)
            q, vb = load_common(rows)
            v_s[rows, :] = vb
            gated.append((rows, q, [load_dir[d](rows) for d in (0, 1)]))
        summed = [(rows, q, [(k, _scan_sums(g, a_refs[d][0:c, :])) for d, (k, g) in enumerate(kg)])
                  for rows, q, kg in gated]
        for rows, q, kb in summed:
            for d, (k, b) in enumerate(kb):
                b3 = b.reshape(c // local_block, local_block, LANES)
                r0 = local_block // 2 if d else local_block // 2 - 1
                ref = b3[:, r0:r0 + 1, :]
                x = b3 - ref
                edge = b[0:1, :] if d else b[c - 1:c, :]
                qp = q * jnp.exp2(x).reshape(c, LANES)
                kp = k * jnp.exp2(-x).reshape(c, LANES)
                entry = jnp.broadcast_to(jnp.exp2(ref), b3.shape).reshape(c, LANES)
                leave = jnp.broadcast_to(jnp.exp2(edge - ref), b3.shape).reshape(c, LANES)
                qp_s, kp_s, qt_s, qe_s, kd_s, dec_s = stage_refs[d]
                qp_s[rows, :] = qp.astype(BF16)
                kp_s[rows, :] = kp.astype(BF16)
                if two_blocks:
                    cross = jnp.exp2(ref[0] - ref[1] if d else ref[1] - ref[0])
                    qt_s[rows, :] = (qp * cross).astype(BF16)
                qe_s[rows, :] = (qp * entry).astype(BF16)
                kd_s[rows, :] = (kp * leave).astype(BF16)
                dec_s[rows, :] = jnp.broadcast_to(jnp.exp2(edge), (LANES, LANES)).T
                ends = jnp.abs(jnp.concatenate([x[:, 0:1, :], x[:, local_block - 1:, :]], axis=1))
                worst = jnp.maximum(worst, jnp.max(ends, axis=(0, 1), keepdims=True).reshape(1, LANES))
        return worst
    worst = lax.fori_loop(0, n // STAGE_UNROLL, stage, jnp.zeros((8, LANES), F32))
    safe = jnp.max(worst) < SAFE_LOG2

    def shortcut_pairs(d, rows):
        qp_s, kp_s, qt_s, _, kd_s, _ = stage_refs[d]
        return _shortcut_pairs(qp_s[rows, :], kp_s[rows, :], qt_s[rows, :] if two_blocks else None, kd_s[rows, :],
                               v_s[rows, :], lv_refs[d], n_heads=n_heads)

    def shortcut_output(d, rows, values, s):
        _, _, _, qe_s, _, dec_s = stage_refs[d]
        return _shortcut_output(*values, qe_s[rows, :], v_s[rows, :], dec_s[rows, :], s, n_heads=n_heads)

    def levels(d, rows, values, st):
        q, vb = load_common(rows)
        k, g = load_dir[d](rows)
        sums = _scan_sums(g, a_refs[d][...])
        return _level_chunk(q.astype(BF16), k.astype(BF16), sums[0:c], sums[c:], vb, st, lv_refs[d],
                            n_heads=n_heads, backward=bool(d))

    pl.when(safe)(lambda: scan(shortcut_pairs, shortcut_output, jnp.zeros((LANES, n_heads * LANES), F32), SCAN_UNROLL))
    pl.when(jnp.logical_not(safe))(lambda: scan(lambda d, rows: None, levels, jnp.zeros((n_heads * LANES, LANES), F32), 1))


def _scan_scratch(seq_all, n_heads):
    assert SCAN_CHUNK == LANES
    width = n_heads * LANES
    out = [pltpu.VMEM((seq_all, width), F32)] * 2
    stage = [pltpu.VMEM((seq_all, LANES), BF16)] * N_STAGED + [pltpu.VMEM((seq_all, LANES), F32)]
    return out + [pltpu.VMEM((seq_all, width), BF16)] + stage * 2


def _log_sigmoid(x):
    return jnp.minimum(x, 0.0) - jnp.log(1.0 + jnp.exp(-jnp.abs(x)))


def _gla_kernel(q_ref, k_ref, v_ref, g_ref, lr_ref, wgf_ref, wgb_ref, bf_ref, bb_ref, gain_ref,
                af_ref, ab_ref, lvf_ref, lvb_ref, o_ref, of_ref, ob_ref, v_s, *stage_refs, seq_all):
    def load_common(rows):
        return q_ref[rows, :] * (B_DK ** -0.5), v_ref[rows, :].astype(BF16)

    def load_dir(wg_ref, b_ref):
        def load(rows):
            pre = _dot(lr_ref[rows, :].astype(BF16), wg_ref[...]) + b_ref[...]
            return k_ref[rows, :], _log_sigmoid(pre) * (1.0 / B_GATE_NORM)
        return load

    _scan_loop(load_common, (load_dir(wgf_ref, bf_ref), load_dir(wgb_ref, bb_ref)), (of_ref, ob_ref),
               (af_ref, ab_ref), (lvf_ref, lvb_ref), v_s, (stage_refs[:N_STAGED + 1], stage_refs[N_STAGED + 1:]),
               n_heads=2, seq_all=seq_all, local_block=GLA_LOCAL_BLOCK)

    def post(j, carry):
        rows = pl.ds(pl.multiple_of(j * ROW_TILE, ROW_TILE), ROW_TILE)
        for hh in range(2):
            cols = slice(hh * LANES, (hh + 1) * LANES)
            o = of_ref[rows, cols] + ob_ref[rows, cols]
            var = jnp.mean(o * o, axis=-1, keepdims=True)
            o_ref[rows, cols] = (o * lax.rsqrt(var + EPS) * gain_ref[...] * _silu(g_ref[rows, cols])).astype(BF16)
        return carry
    lax.fori_loop(0, seq_all // ROW_TILE, post, 0, unroll=3)


def _gla(p3, w_gate_up, b_gate_up, gain, consts):
    batch, seq_all, _ = p3.shape
    pairs = B_HEADS // 2
    wg = jnp.zeros((2, pairs, LANES, LANES), F32)
    for d in range(2):
        w = w_gate_up[d].reshape(B_GATE_RANK, pairs, LANES).transpose(1, 0, 2)
        wg = wg.at[d, :, d * B_GATE_RANK:(d + 1) * B_GATE_RANK, :].set(w)
    wg = wg.astype(BF16)
    bias = b_gate_up.reshape(2, pairs, 1, LANES)
    a_f, a_b, lv_f, lv_b = consts
    col = lambda off: (lambda b, p: (b, 0, off + p))
    q_off = 3 * A_HEADS * LANES // LANES
    k_off = q_off + B_HEADS * B_DK // LANES
    v_off = (k_off * LANES + B_HEADS * B_DK) // (2 * LANES)
    g_off = v_off + pairs
    lr_off = (g_off + pairs) * 2
    wspec = pl.BlockSpec((None, LANES, LANES), lambda b, p: (p, 0, 0))
    bspec = pl.BlockSpec((None, 1, LANES), lambda b, p: (p, 0, 0))
    return pl.pallas_call(
        functools.partial(_gla_kernel, seq_all=seq_all),
        out_shape=jax.ShapeDtypeStruct((batch, seq_all, B_HEADS * LANES), BF16),
        grid=(batch, pairs),
        in_specs=[
            pl.BlockSpec((None, seq_all, LANES), col(q_off)),
            pl.BlockSpec((None, seq_all, LANES), col(k_off)),
            pl.BlockSpec((None, seq_all, 2 * LANES), col(v_off)),
            pl.BlockSpec((None, seq_all, 2 * LANES), col(g_off)),
            pl.BlockSpec((None, seq_all, LANES), lambda b, p: (b, 0, lr_off)),
            wspec, wspec, bspec, bspec,
            _resident((1, LANES)),
            _resident(a_f.shape), _resident(a_b.shape), _resident(lv_f.shape), _resident(lv_b.shape),
        ],
        out_specs=pl.BlockSpec((None, seq_all, 2 * LANES), lambda b, p: (b, 0, p)),
        scratch_shapes=_scan_scratch(seq_all, 2),
        compiler_params=_cparams(("parallel", "parallel")),
        name="gla",
    )(p3, p3, p3, p3, p3, wg[0], wg[1], bias[0], bias[1], gain.reshape(1, LANES), a_f, a_b, lv_f, lv_b)


def _hgrn_kernel(q_ref, ff_ref, fb_ref, i_ref, g_ref, lb_ref, gain_ref,
                 af_ref, ab_ref, lvf_ref, lvb_ref, o_ref, of_ref, ob_ref, v_s, *stage_refs, seq_all, layer):
    def lower_bound(d):
        raw = lb_ref[d]
        ex = jnp.exp(raw - jnp.max(raw, axis=0, keepdims=True))
        p = ex / jnp.sum(ex, axis=0, keepdims=True)
        return jnp.sum(p[1:layer + 1], axis=0, keepdims=True)

    def load_common(rows):
        return _silu(q_ref[rows, :]) * (C_DH ** -0.5), i_ref[rows, :].astype(BF16)

    def load_dir(f_ref, lb):
        def load(rows):
            f = lb + (1.0 - lb) * jax.nn.sigmoid(f_ref[rows, :])
            return 1.0 - f, jnp.log(f)
        return load

    _scan_loop(load_common, (load_dir(ff_ref, lower_bound(0)), load_dir(fb_ref, lower_bound(1))), (of_ref, ob_ref),
               (af_ref, ab_ref), (lvf_ref, lvb_ref), v_s, (stage_refs[:N_STAGED + 1], stage_refs[N_STAGED + 1:]),
               n_heads=1, seq_all=seq_all, local_block=HGRN_LOCAL_BLOCK)

    def post(j, carry):
        rows = pl.ds(pl.multiple_of(j * ROW_TILE, ROW_TILE), ROW_TILE)
        o = of_ref[rows, :] + ob_ref[rows, :]
        var = jnp.mean(o * o, axis=-1, keepdims=True)
        o_ref[rows, :] = (o * lax.rsqrt(var + EPS) * gain_ref[...] * _silu(g_ref[rows, :])).astype(BF16)
        return carry
    lax.fori_loop(0, seq_all // ROW_TILE, post, 0, unroll=3)


def _hgrn(p3, lb_raw, gain, consts, layer):
    batch, seq_all, _ = p3.shape
    a_f, a_b, lv_f, lv_b = consts
    col = lambda off: (lambda b, h: (b, 0, off + h))
    return pl.pallas_call(
        functools.partial(_hgrn_kernel, seq_all=seq_all, layer=layer),
        out_shape=jax.ShapeDtypeStruct((batch, seq_all, C_HEADS * LANES), BF16),
        grid=(batch, C_HEADS),
        in_specs=[pl.BlockSpec((None, seq_all, LANES), col(s * C_HEADS)) for s in range(5)] + [
            pl.BlockSpec((2, DEPTH, LANES), lambda b, h: (0, 0, h)),
            _resident((1, LANES)),
            _resident(a_f.shape), _resident(a_b.shape), _resident(lv_f.shape), _resident(lv_b.shape),
        ],
        out_specs=pl.BlockSpec((None, seq_all, LANES), lambda b, h: (b, 0, h)),
        scratch_shapes=_scan_scratch(seq_all, 1),
        compiler_params=_cparams(("parallel", "parallel")),
        name="hgrn",
    )(p3, p3, p3, p3, p3, lb_raw, gain.reshape(1, LANES), a_f, a_b, lv_f, lv_b)


def _rope_tables(seq):
    pos = jnp.arange(seq)
    row_ids = (pos // GRID_W).astype(F32)
    col_ids = (pos % GRID_W).astype(F32)
    n_axis = A_DQK // 2
    inv = ROPE_BASE ** (-jnp.arange(0, n_axis, 2, dtype=F32) / n_axis)
    ang_r = row_ids[:, None] * inv
    ang_c = col_ids[:, None] * inv
    zeros = jnp.zeros_like(ang_r)
    cos64 = jnp.concatenate([jnp.cos(ang_r)] * 2 + [jnp.cos(ang_c)] * 2, axis=-1)
    sa64 = jnp.concatenate([-jnp.sin(ang_r), zeros, -jnp.sin(ang_c), zeros], axis=-1)
    sb64 = jnp.concatenate([zeros, jnp.sin(ang_r), zeros, jnp.sin(ang_c)], axis=-1)
    ident = lambda v: jnp.full((CTX_LEN, LANES), v, F32)
    full = lambda t, v: jnp.concatenate([ident(v), jnp.tile(t, (1, 2))], axis=0)
    return full(cos64, 1.0), full(sa64, 0.0), full(sb64, 0.0)


def kernel(x, c, ctx, c_ctx, w_ada, b_ada, norm1_gain, norm2_gain, w_in_even, qk_gain_a, lambda_a, subln_gain_a, w_gate_up_b, b_gate_up_b, onorm_gain_b, w_out_even, w_in_odd, lb_raw_c, onorm_gain_c, w_out_odd, w_ffn_in, w_ffn_out):
    batch, seq, d = x.shape
    seq_all = CTX_LEN + seq
    assert d == D_MODEL and ctx.shape[1] == CTX_LEN == ROW_TILE == Q_TILE and seq % ROW_TILE == 0

    tiles = seq_all // ROW_TILE
    flat = lambda a: a.reshape(-1, a.shape[-1])
    xz = (flat(ctx), flat(x))
    mod_rows = -(-(batch + 1) // 8) * 8
    cc = jnp.zeros((mod_rows, d), F32).at[:batch].set(c).at[batch].set(c_ctx)
    mods = _ada(cc, w_ada, b_ada).reshape(DEPTH, mod_rows, 6, d)

    tabs = _rope_tables(seq)
    half = np.arange(LANES) // A_DQK
    bd = jnp.asarray((half[:, None] == half[None, :]) / A_DQK, BF16)
    consts_gla = _scan_consts(2)
    consts_hgrn = _scan_consts(1)

    for l in range(DEPTH):
        j = l // 2
        last = l == DEPTH - 1
        wi = w_ffn_in[l].astype(BF16)
        wf = w_ffn_out[l].astype(BF16)
        if l % 2 == 0:
            w = jnp.pad(w_in_even[j], ((0, 0), (0, EVEN_COLS_PAD - EVEN_COLS))).astype(BF16)
            p3 = _inproj(xz, mods[l], norm1_gain[l], w, batch, tiles,
                         qk=(tabs, qk_gain_a[j][0], qk_gain_a[j][1], bd)).reshape(batch, seq_all, EVEN_COLS_PAD)
            lambda_init = 0.8 - 0.6 * math.exp(-0.3 * l)
            oa_ctx, oa_lat = _attn(p3, qk_gain_a[j], lambda_a[j], subln_gain_a[j], lambda_init)
            ob = _gla(p3, w_gate_up_b[j], b_gate_up_b[j], onorm_gain_b[j], consts_gla)
            a_width = A_HEADS * LANES
            wo = w_out_even[j].astype(BF16)
            o_parts = [(flat(oa_ctx), flat(oa_lat)), flat(ob)]
            wo_parts = [wo[:a_width], wo[a_width:]]
        else:
            p3 = _inproj(xz, mods[l], norm1_gain[l], w_in_odd[j].astype(BF16), batch, tiles).reshape(batch, seq_all, ODD_COLS)
            oc = _hgrn(p3, lb_raw_c, onorm_gain_c[j], consts_hgrn, l)
            o_parts = [flat(oc)]
            wo_parts = [w_out_odd[j].astype(BF16)]
        xz = _post(o_parts, xz, mods[l], norm2_gain[l], wo_parts, wi, wf, batch, tiles, last)
    return xz.reshape(batch, seq, d)
```

```python
import functools
import math

import numpy as np
import jax
import jax.numpy as jnp
from jax import lax
from jax.experimental import pallas as pl
from jax.experimental.pallas import tpu as pltpu

F32 = jnp.float32
BF16 = jnp.bfloat16

D_MODEL = 1024
DEPTH = 4
CTX_LEN = 256
GRID_W = 64
A_HEADS = 4
A_DQK = 64
B_HEADS = 4
B_DK = 64
B_GATE_RANK = 16
B_GATE_NORM = 16.0
C_HEADS = 8
C_DH = 128
ROPE_BASE = 10000.0
EPS = 1e-6
D_FF = 2816
EVEN_COLS = 3104
EVEN_COLS_PAD = 3200
ODD_COLS = 5120

LANES = 128
ROW_TILE = 256
COL_CHUNK = 512
FF_CHUNK = 256
Q_TILE = 256
Q_TILES_PER_STEP = 8
SUM_ROWS = 16
SAFE_SCORE_LOG2 = 60.0
LOG2E = math.log2(math.e)
SCAN_CHUNK = 128
SCAN_UNROLL = 9
STAGE_UNROLL = 6
VMEM_LIMIT = 52 * 1024 * 1024


def _cparams(sem):
    return pltpu.CompilerParams(dimension_semantics=sem, vmem_limit_bytes=VMEM_LIMIT)


def _resident(shape):
    nd = len(shape)
    return pl.BlockSpec(shape, lambda *_: (0,) * nd, pipeline_mode=pl.Buffered(1))


def _silu(x):
    return x * jax.nn.sigmoid(x)


def _dot(a, b):
    return jnp.dot(a, b, preferred_element_type=F32)


def _dot_nt(a, b):
    return lax.dot_general(a, b, (((1,), (1,)), ((), ())), preferred_element_type=F32)


def _dot_tn(a, b):
    return lax.dot_general(a, b, (((0,), (0,)), ((), ())), preferred_element_type=F32)


def _ada_kernel(c_ref, w_ref, b_ref, o_ref):
    s = _silu(c_ref[...]).astype(BF16)
    o_ref[...] = _dot(s, w_ref[...].astype(BF16)) + b_ref[...]


def _ada(cc, w_ada, b_ada):
    depth, d, n = w_ada.shape
    rows = cc.shape[0]
    tn = 1536
    return pl.pallas_call(
        _ada_kernel,
        out_shape=jax.ShapeDtypeStruct((depth, rows, n), F32),
        grid=(depth, n // tn),
        in_specs=[
            pl.BlockSpec((rows, d), lambda l, j: (0, 0)),
            pl.BlockSpec((None, d, tn), lambda l, j: (l, 0, j)),
            pl.BlockSpec((None, 1, tn), lambda l, j: (l, 0, j)),
        ],
        out_specs=pl.BlockSpec((None, rows, tn), lambda l, j: (l, 0, j)),
        compiler_params=_cparams(("parallel", "parallel")),
        name="ada",
    )(cc, w_ada, b_ada.reshape(depth, 1, n))


def _norm_mod(x, gain, shift, scale):
    var = jnp.mean(x * x, axis=-1, keepdims=True)
    return (x * lax.rsqrt(var + EPS) * gain) * (1.0 + scale) + shift


def _row_sources(rt):
    return list(rt) if isinstance(rt, tuple) else [rt]


def _row_specs(rt, tile_fn, tiles):
    if not isinstance(rt, tuple):
        return [pl.BlockSpec((ROW_TILE, rt.shape[1]), lambda i: (tile_fn(i), 0))]
    sample = lambda i: tile_fn(i) // tiles
    latent = lambda i: sample(i) * (tiles - 1) + jnp.maximum(tile_fn(i) % tiles - 1, 0)
    return [pl.BlockSpec((ROW_TILE, rt[0].shape[1]), lambda i: (sample(i), 0)),
            pl.BlockSpec((ROW_TILE, rt[1].shape[1]), lambda i: (latent(i), 0))]


def _row_tile(refs, is_context):
    if len(refs) == 1:
        return refs[0][...]
    return jnp.where(is_context, refs[0][...], refs[1][...])


def _take(refs, counts):
    groups, pos = [], 0
    for n in counts:
        groups.append(refs[pos:pos + n])
        pos += n
    return groups, refs[pos:]


def _inproj_kernel(*refs, n_x, tiles, qk_heads):
    x_groups, rest = _take(refs, [n_x, n_x])
    mod_a_ref, mod_b_ref, gain_ref, w_ref = rest[:4]
    o_ref = rest[-1]
    step = pl.program_id(0)
    halves = []
    for t, m in enumerate((mod_a_ref, mod_b_ref)):
        x = _row_tile(x_groups[t], (2 * step + t) % tiles == 0)
        halves.append(_norm_mod(x, gain_ref[...], m[0:1, :], m[1:2, :]).astype(BF16))
    h = jnp.concatenate(halves, axis=0)
    ncols = w_ref.shape[1]
    for c0 in range(0, ncols, COL_CHUNK):
        c1 = min(c0 + COL_CHUNK, ncols)
        res = _dot(h, w_ref[:, c0:c1])
        if c1 > 2 * qk_heads * LANES:
            o_ref[:, c0:c1] = res
            continue
        c_ref, sa_ref, sb_ref, qg_ref, kg_ref, bd_ref = rest[4:10]
        gains = (qg_ref[...] * (A_DQK ** -0.5 * LOG2E), kg_ref[...])
        blocks = [res[:, b0 - c0:b0 - c0 + LANES] for b0 in range(c0, c1, LANES)]
        ms_all = _dot(jnp.concatenate([(x * x).astype(BF16) for x in blocks], axis=0), bd_ref[...])
        n_rows = 2 * ROW_TILE
        for i, x in enumerate(blocks):
            b0 = c0 + i * LANES
            y = x * lax.rsqrt(ms_all[i * n_rows:(i + 1) * n_rows] + EPS) * gains[b0 // (qk_heads * LANES)]
            for t in range(2):
                pos = pl.ds(pl.multiple_of(((2 * step + t) % tiles) * ROW_TILE, ROW_TILE), ROW_TILE)
                rows = slice(t * ROW_TILE, (t + 1) * ROW_TILE)
                yt = y[rows]
                o_ref[rows, b0:b0 + LANES] = (yt * c_ref[pos, :] + pltpu.roll(yt, LANES - 16, 1) * sa_ref[pos, :]
                                              + pltpu.roll(yt, 16, 1) * sb_ref[pos, :])


def _mod_row(i, tiles, batch):
    return jnp.where(i % tiles == 0, batch, i // tiles)


def _inproj(xz, mod, gain, w, batch, tiles, qk=None):
    d, ncols = w.shape
    rows = batch * tiles * ROW_TILE
    mod_spec = lambda t: pl.BlockSpec((None, 6, d), lambda i: (_mod_row(2 * i + t, tiles, batch), 0, 0))
    x_specs = [spec for t in range(2) for spec in _row_specs(xz, (lambda t: lambda i: 2 * i + t)(t), tiles)]
    qk_args = []
    if qk:
        tabs, q_gain, k_gain, bd = qk
        qk_args = [*tabs, jnp.tile(q_gain, 2).reshape(1, LANES), jnp.tile(k_gain, 2).reshape(1, LANES), bd]
    return pl.pallas_call(
        functools.partial(_inproj_kernel, n_x=len(_row_sources(xz)), tiles=tiles, qk_heads=A_HEADS if qk else 0),
        out_shape=jax.ShapeDtypeStruct((rows, ncols), F32),
        grid=(rows // (2 * ROW_TILE),),
        in_specs=x_specs + [mod_spec(0), mod_spec(1), _resident((1, d)), _resident((d, ncols))]
        + [_resident(a.shape) for a in qk_args],
        out_specs=pl.BlockSpec((2 * ROW_TILE, ncols), lambda i: (i, 0)),
        compiler_params=_cparams(("parallel",)),
        name="inproj",
    )(*(_row_sources(xz) * 2), mod, mod, gain.reshape(1, d), w, *qk_args)


def _post_kernel(*refs, n_src, tiles):
    groups, rest = _take(refs, [n for n in n_src for _ in range(2)])
    n_parts = len(n_src) - 1
    mod_refs, gain_ref = rest[0:2], rest[2]
    wo_refs = rest[3:3 + n_parts]
    wi_ref, wf_ref, out_ref = rest[3 + n_parts:]
    step = pl.program_id(0)
    is_context = [(2 * step + t) % tiles == 0 for t in range(2)]
    tile = lambda p, t: _row_tile(groups[2 * p + t], is_context[t])
    x1 = []
    for t in range(2):
        y = None
        for p, wo_ref in enumerate(wo_refs):
            part = _dot(tile(p, t), wo_ref[...])
            y = part if y is None else y + part
        x1.append(tile(n_parts, t) + mod_refs[t][2:3, :] * y)
    h = jnp.concatenate([_norm_mod(x1[t], gain_ref[...], mod_refs[t][3:4, :], mod_refs[t][4:5, :]).astype(BF16)
                         for t in range(2)], axis=0)
    acc = None
    for c0 in range(0, D_FF, FF_CHUNK):
        gate = _dot(h, wi_ref[:, c0:c0 + FF_CHUNK])
        up = _dot(h, wi_ref[:, D_FF + c0:D_FF + c0 + FF_CHUNK])
        a = (_silu(gate) * up).astype(BF16)
        part = _dot(a, wf_ref[c0:c0 + FF_CHUNK, :])
        acc = part if acc is None else acc + part
    for t in range(2):
        rows = slice(t * ROW_TILE, (t + 1) * ROW_TILE)
        out_ref[rows, :] = x1[t] + mod_refs[t][5:6, :] * acc[rows]


def _post(o_parts, xz, mod, gain, wo_parts, wi, wf, batch, tiles, latent_only):
    d = wi.shape[0]
    tensors = list(o_parts) + [xz]
    if latent_only:
        assert not any(isinstance(rt, tuple) for rt in tensors)
        out_tiles = tiles - 1
        src = lambda i: (i // out_tiles) * tiles + 1 + i % out_tiles
        mrow = lambda i: i // out_tiles
    else:
        out_tiles = tiles
        src = lambda i: i
        mrow = lambda i: _mod_row(i, tiles, batch)
    assert (batch * out_tiles) % 2 == 0
    tile_fn = lambda t: lambda i: src(2 * i + t)
    in_specs = [spec for rt in tensors for t in range(2) for spec in _row_specs(rt, tile_fn(t), tiles)]
    in_specs += [pl.BlockSpec((None, 6, d), (lambda t: lambda i: (mrow(2 * i + t), 0, 0))(t)) for t in range(2)]
    in_specs += [_resident((1, d))]
    in_specs += [_resident(w.shape) for w in wo_parts]
    in_specs += [_resident(wi.shape), _resident(wf.shape)]
    row_args = [a for rt in tensors for _ in range(2) for a in _row_sources(rt)]
    return pl.pallas_call(
        functools.partial(_post_kernel, n_src=tuple(len(_row_sources(rt)) for rt in tensors), tiles=tiles),
        out_shape=jax.ShapeDtypeStruct((batch * out_tiles * ROW_TILE, d), F32),
        grid=(batch * out_tiles // 2,),
        in_specs=in_specs,
        out_specs=pl.BlockSpec((2 * ROW_TILE, d), lambda i: (i, 0)),
        compiler_params=_cparams(("parallel",)),
        name="post",
    )(*row_args, mod, mod, gain.reshape(1, d), *wo_parts, wi, wf)


def _attn_kernel(qc_ref, *refs, lambda_init, seq_all):
    q_refs = refs[:Q_TILES_PER_STEP]
    (k_ref, v_ref, qg_ref, kg_ref, sg_ref, lam_ref, octx_ref, olat_ref, kb_ref, vt_ref) = refs[Q_TILES_PER_STEP:]
    step = pl.program_id(2)

    @pl.when(step == 0)
    def _prep_keys():
        def body(r, carry):
            rows = pl.ds(pl.multiple_of(r * Q_TILE, Q_TILE), Q_TILE)
            kb_ref[rows, :] = k_ref[rows, :].astype(BF16)
            vt_ref[0:LANES, rows] = v_ref[rows, :].T.astype(BF16)
            return carry
        lax.fori_loop(0, seq_all // Q_TILE, body, 0, unroll=3)
        vt_ref[LANES:, :] = jnp.ones((SUM_ROWS, seq_all), BF16)

    lam_v = lam_ref[...]
    e1 = jnp.exp(jnp.sum(lam_v[0:1, :] * lam_v[1:2, :], axis=-1, keepdims=True))
    e2 = jnp.exp(jnp.sum(lam_v[2:3, :] * lam_v[3:4, :], axis=-1, keepdims=True))
    lam = e1 - e2 + lambda_init

    def scores(q_ref, n_keys):
        qn = q_ref[...]
        lane = lax.broadcasted_iota(jnp.int32, qn.shape, 1)
        q2 = jnp.concatenate([jnp.where(lane < A_DQK, qn, 0.0).astype(BF16),
                              jnp.where(lane >= A_DQK, qn, 0.0).astype(BF16)], axis=0)
        return _dot_nt(kb_ref[0:n_keys, :], q2)

    def outputs(s, n_keys, shift):
        if shift is None:
            shift = jnp.max(s, axis=0, keepdims=True)
        p = jnp.exp2(s - shift).astype(BF16)
        pv = _dot(vt_ref[:, 0:n_keys], p)
        pv = pv[0:LANES] / pv[LANES:LANES + 1]
        o = pv[:, :Q_TILE] - lam * pv[:, Q_TILE:]
        var = jnp.mean(o * o, axis=0, keepdims=True)
        o = o * lax.rsqrt(var + EPS) * (sg_ref[...] * (1.0 - lambda_init))
        return o.T.astype(BF16)

    def run(shift):
        s_next = scores(q_refs[0], seq_all)
        for t in range(Q_TILES_PER_STEP):
            s = s_next
            if t + 1 < Q_TILES_PER_STEP:
                s_next = scores(q_refs[t + 1], seq_all)
            olat_ref[t * Q_TILE:(t + 1) * Q_TILE, :] = outputs(s, seq_all, shift)

        @pl.when(step == 0)
        def _context_tile():
            octx_ref[...] = outputs(scores(qc_ref, CTX_LEN), CTX_LEN, shift)

    bound = (A_DQK ** 0.5 * LOG2E) * (jnp.max(jnp.abs(qg_ref[...]), axis=-1, keepdims=True)
                                      * jnp.max(jnp.abs(kg_ref[...]), axis=-1, keepdims=True)) + 1.0
    small = jnp.max(bound) < SAFE_SCORE_LOG2
    pl.when(small)(lambda: run(bound))
    pl.when(jnp.logical_not(small))(lambda: run(None))


def _attn(p3, qk_gain, lam, sub_gain, lambda_init):
    batch, seq_all, _ = p3.shape
    seq = seq_all - CTX_LEN
    n_t = Q_TILES_PER_STEP
    assert seq % (n_t * Q_TILE) == 0
    qg = jnp.tile(qk_gain[0], 2).reshape(1, LANES)
    kg = jnp.tile(qk_gain[1], 2).reshape(1, LANES)
    kern = functools.partial(_attn_kernel, lambda_init=lambda_init, seq_all=seq_all)
    q_spec = lambda t: pl.BlockSpec((None, Q_TILE, LANES), lambda b, h, i: (b, 1 + n_t * i + t, h))
    o_ctx, o_lat = pl.pallas_call(
        kern,
        out_shape=(jax.ShapeDtypeStruct((batch, CTX_LEN, A_HEADS * LANES), BF16),
                   jax.ShapeDtypeStruct((batch, seq, A_HEADS * LANES), BF16)),
        grid=(batch, A_HEADS, seq // (n_t * Q_TILE)),
        in_specs=[pl.BlockSpec((None, Q_TILE, LANES), lambda b, h, i: (b, 0, h))]
        + [q_spec(t) for t in range(n_t)] + [
            pl.BlockSpec((None, seq_all, LANES), lambda b, h, i: (b, 0, A_HEADS + h)),
            pl.BlockSpec((None, seq_all, LANES), lambda b, h, i: (b, 0, 2 * A_HEADS + h)),
            _resident((1, LANES)), _resident((1, LANES)), _resident((LANES, 1)), _resident((4, A_DQK)),
        ],
        out_specs=(pl.BlockSpec((None, CTX_LEN, LANES), lambda b, h, i: (b, 0, h)),
                   pl.BlockSpec((None, n_t * Q_TILE, LANES), lambda b, h, i: (b, i, h))),
        scratch_shapes=[pltpu.VMEM((seq_all, LANES), BF16), pltpu.VMEM((LANES + SUM_ROWS, seq_all), BF16)],
        compiler_params=_cparams(("parallel", "parallel", "arbitrary")),
        name="attn",
    )(*([p3] * (n_t + 3)), qg, kg, sub_gain.reshape(LANES, 1), lam)
    return o_ctx, o_lat


MATMUL_LEVELS = 2
HGRN_LOCAL_BLOCK = 64
GLA_LOCAL_BLOCK = 128
N_STAGED = 5
SAFE_LOG2 = 100.0


def _scan_levels():
    return int(math.log2(SCAN_CHUNK))


def _scan_consts(n_heads):
    c = SCAN_CHUNK
    t = np.arange(c)[:, None]
    u = np.arange(c)[None, :]
    blocks_f = [u <= t]
    blocks_b = [u >= t]
    for lvl in range(1, MATMUL_LEVELS + 1):
        h = 1 << (lvl - 1)
        m = (t // (2 * h)) * (2 * h) + h
        second = (t % (2 * h)) >= h
        blocks_f.append(np.where(second, (u >= m) & (u <= t), (u > t) & (u <= m - 1)))
        blocks_b.append(np.where(second, (u >= m) & (u < t), (u >= t) & (u <= m - 1)))
    a_f = np.tile(np.concatenate(blocks_f, axis=0).astype(np.float32), (1, 2))
    a_b = np.tile(np.concatenate(blocks_b, axis=0).astype(np.float32), (1, 2))
    x = t ^ u
    lv = np.where(x > 0, np.floor(np.log2(np.maximum(x, 1))).astype(np.int32) + 1, 0)
    lv_f = np.where(u <= t, lv, -1).astype(np.int32)
    lv_b = np.where(u >= t, lv, -1).astype(np.int32)
    lv_f = np.tile(lv_f, (n_heads, 1))
    lv_b = np.tile(lv_b, (n_heads, 1))
    return (jnp.asarray(a_f, BF16), jnp.asarray(a_b, BF16), jnp.asarray(lv_f), jnp.asarray(lv_b))


def _scan_sums(g, a):
    g2 = g * LOG2E
    g_hi = g2.astype(BF16)
    g_lo = (g2 - g_hi.astype(F32)).astype(BF16)
    return _dot(a, jnp.concatenate([g_hi, g_lo], axis=0))


def _level_exponent(b, lvl, backward):
    c = SCAN_CHUNK
    h = 1 << (lvl - 1)
    r0 = h if backward else h - 1
    b3 = b.reshape(c // (2 * h), 2 * h, LANES)
    ref = b3[:, r0:r0 + 1, :]
    if h % 8 == 0:
        first, second = b3[:, :h, :], b3[:, h:, :]
        x = jnp.concatenate([first - ref, ref - second] if backward else [ref - first, second - ref], axis=1)
    else:
        x = -jnp.abs(b3 - ref)
    return x.reshape(c, LANES)


def _by_head(xb, n_heads):
    if n_heads == 1:
        return xb
    dk = LANES // n_heads
    lane = lax.broadcasted_iota(jnp.int32, xb.shape, 1)
    masks = [jnp.where((lane >= hh * dk) & (lane < (hh + 1) * dk), 1.0, 0.0).astype(BF16) for hh in range(n_heads)]
    return jnp.concatenate([xb * hm for hm in masks], axis=0)


def _head_block_mask(shape, row_block, col_block):
    r = lax.broadcasted_iota(jnp.int32, shape, 0) // row_block
    cc = lax.broadcasted_iota(jnp.int32, shape, 1) // col_block
    return r == cc


def _level_chunk(qb, kb, b, fine, vb, st, lv_ref, *, n_heads, backward):
    c = SCAN_CHUNK
    nl = _scan_levels()
    edge = b[0:1, :] if backward else b[c - 1:c, :]
    e_in = jnp.exp2(b).astype(BF16)
    e_out = jnp.exp2(edge - b).astype(BF16)

    def level_decay(lvl):
        if lvl <= MATMUL_LEVELS:
            return jnp.exp2(fine[(lvl - 1) * c:lvl * c]).astype(BF16)
        return jnp.exp2(_level_exponent(b, lvl, backward)).astype(BF16)

    o = _dot_nt(qb * e_in, st.astype(BF16))

    lv = lv_ref[...]
    att = jnp.zeros((n_heads * c, c), F32)
    for lvl in range(nl + 1):
        if lvl == 0:
            qh, kh = qb, kb
        else:
            el = level_decay(lvl)
            qh, kh = qb * el, kb * el
        att = jnp.where(lv == lvl, _dot_nt(_by_head(qh, n_heads), kh), att)
    attb = att.astype(BF16)
    intra = [_dot(attb[hh * c:(hh + 1) * c], vb[:, hh * LANES:(hh + 1) * LANES]) for hh in range(n_heads)]
    o = o + (intra[0] if n_heads == 1 else jnp.concatenate(intra, axis=1))

    dec = jnp.exp2(edge)
    upd = _dot_tn(vb, kb * e_out)
    if n_heads > 1:
        upd = jnp.where(_head_block_mask(upd.shape, LANES, LANES // n_heads), upd, 0.0)
    return o, dec * st + upd


def _shortcut_pairs(qp, kp, qt, kd, vb, lv_ref, *, n_heads):
    lv = lv_ref[...]
    rows = lv.shape[0]
    if qt is None:
        att = jnp.where(lv >= 0, _dot_nt(_by_head(qp, n_heads), kp), 0.0).astype(BF16)
    else:
        both = _dot_nt(jnp.concatenate([_by_head(qp, n_heads), _by_head(qt, n_heads)], axis=0), kp)
        att = jnp.where(lv == _scan_levels(), both[rows:], jnp.where(lv >= 0, both[:rows], 0.0)).astype(BF16)
    upd = _dot_tn(kd, vb)
    if n_heads > 1:
        upd = jnp.where(_head_block_mask(upd.shape, LANES // n_heads, LANES), upd, 0.0)
    return att, upd


def _shortcut_output(att, upd, qe, vb, dec, s, *, n_heads):
    c = SCAN_CHUNK
    sb = s.astype(BF16)
    outs = []
    for hh in range(n_heads):
        cols = slice(hh * LANES, (hh + 1) * LANES)
        lhs = jnp.concatenate([att[hh * c:(hh + 1) * c], qe], axis=1)
        rhs = jnp.concatenate([vb[:, cols], sb[:, cols]], axis=0)
        outs.append(_dot(lhs, rhs))
    o = outs[0] if n_heads == 1 else jnp.concatenate(outs, axis=1)
    if n_heads > 1:
        dec = jnp.concatenate([dec] * n_heads, axis=1)
    return o, dec * s + upd


def _scan_loop(load_common, load_dir, out_refs, a_refs, lv_refs, v_s, stage_refs, *, n_heads, seq_all, local_block):
    c = SCAN_CHUNK
    assert local_block in (c, c // 2)
    two_blocks = local_block < c
    n = seq_all // c
    n_ctx = CTX_LEN // c

    def chunk_rows(j):
        return pl.ds(pl.multiple_of(j * c, c), c)

    def scan(prepare, finish, state0, unroll):
        def body(i, sts):
            sts = list(sts)
            todo = []
            for u in range(unroll):
                j = i * unroll + u
                jb = jnp.where(j < n_ctx, n_ctx - 1 - j, n - 1 - (j - n_ctx))
                for d, jj in ((0, j), (1, jb)):
                    rows = chunk_rows(jj)
                    todo.append((d, rows, prepare(d, rows)))
            for d, rows, values in todo:
                out_refs[d][rows, :], sts[d] = finish(d, rows, values, sts[d])
            return tuple(sts)
        lax.fori_loop(0, n // unroll, body, (state0, state0))

    def stage(j, worst):
        gated = []
        for u in range(STAGE_UNROLL):
            rows = chunk_rows(j * STAGE_UNROLL + u)
            q, vb = load_common(rows)
            v_s[rows, :] = vb
            gated.append((rows, q, [load_dir[d](rows) for d in (0, 1)]))
        summed = [(rows, q, [(k, _scan_sums(g, a_refs[d][0:c, :])) for d, (k, g) in enumerate(kg)])
                  for rows, q, kg in gated]
        for rows, q, kb in summed:
            for d, (k, b) in enumerate(kb):
                b3 = b.reshape(c // local_block, local_block, LANES)
                r0 = local_block // 2 if d else local_block // 2 - 1
                ref = b3[:, r0:r0 + 1, :]
                x = b3 - ref
                edge = b[0:1, :] if d else b[c - 1:c, :]
                qp = q * jnp.exp2(x).reshape(c, LANES)
                kp = k * jnp.exp2(-x).reshape(c, LANES)
                entry = jnp.broadcast_to(jnp.exp2(ref), b3.shape).reshape(c, LANES)
                leave = jnp.broadcast_to(jnp.exp2(edge - ref), b3.shape).reshape(c, LANES)
                qp_s, kp_s, qt_s, qe_s, kd_s, dec_s = stage_refs[d]
                qp_s[rows, :] = qp.astype(BF16)
                kp_s[rows, :] = kp.astype(BF16)
                if two_blocks:
                    cross = jnp.exp2(ref[0] - ref[1] if d else ref[1] - ref[0])
                    qt_s[rows, :] = (qp * cross).astype(BF16)
                qe_s[rows, :] = (qp * entry).astype(BF16)
                kd_s[rows, :] = (kp * leave).astype(BF16)
                dec_s[rows, :] = jnp.broadcast_to(jnp.exp2(edge), (LANES, LANES)).T
                ends = jnp.abs(jnp.concatenate([x[:, 0:1, :], x[:, local_block - 1:, :]], axis=1))
                worst = jnp.maximum(worst, jnp.max(ends, axis=(0, 1), keepdims=True).reshape(1, LANES))
        return worst
    worst = lax.fori_loop(0, n // STAGE_UNROLL, stage, jnp.zeros((8, LANES), F32))
    safe = jnp.max(worst) < SAFE_LOG2

    def shortcut_pairs(d, rows):
        qp_s, kp_s, qt_s, _, kd_s, _ = stage_refs[d]
        return _shortcut_pairs(qp_s[rows, :], kp_s[rows, :], qt_s[rows, :] if two_blocks else None, kd_s[rows, :],
                               v_s[rows, :], lv_refs[d], n_heads=n_heads)

    def shortcut_output(d, rows, values, s):
        _, _, _, qe_s, _, dec_s = stage_refs[d]
        return _shortcut_output(*values, qe_s[rows, :], v_s[rows, :], dec_s[rows, :], s, n_heads=n_heads)

    def levels(d, rows, values, st):
        q, vb = load_common(rows)
        k, g = load_dir[d](rows)
        sums = _scan_sums(g, a_refs[d][...])
        return _level_chunk(q.astype(BF16), k.astype(BF16), sums[0:c], sums[c:], vb, st, lv_refs[d],
                            n_heads=n_heads, backward=bool(d))

    pl.when(safe)(lambda: scan(shortcut_pairs, shortcut_output, jnp.zeros((LANES, n_heads * LANES), F32), SCAN_UNROLL))
    pl.when(jnp.logical_not(safe))(lambda: scan(lambda d, rows: None, levels, jnp.zeros((n_heads * LANES, LANES), F32), 1))


def _scan_scratch(seq_all, n_heads):
    assert SCAN_CHUNK == LANES
    width = n_heads * LANES
    out = [pltpu.VMEM((seq_all, width), F32)] * 2
    stage = [pltpu.VMEM((seq_all, LANES), BF16)] * N_STAGED + [pltpu.VMEM((seq_all, LANES), F32)]
    return out + [pltpu.VMEM((seq_all, width), BF16)] + stage * 2


def _log_sigmoid(x):
    return jnp.minimum(x, 0.0) - jnp.log(1.0 + jnp.exp(-jnp.abs(x)))


def _gla_kernel(q_ref, k_ref, v_ref, g_ref, lr_ref, wgf_ref, wgb_ref, bf_ref, bb_ref, gain_ref,
                af_ref, ab_ref, lvf_ref, lvb_ref, o_ref, of_ref, ob_ref, v_s, *stage_refs, seq_all):
    def load_common(rows):
        return q_ref[rows, :] * (B_DK ** -0.5), v_ref[rows, :].astype(BF16)

    def load_dir(wg_ref, b_ref):
        def load(rows):
            pre = _dot(lr_ref[rows, :].astype(BF16), wg_ref[...]) + b_ref[...]
            return k_ref[rows, :], _log_sigmoid(pre) * (1.0 / B_GATE_NORM)
        return load

    _scan_loop(load_common, (load_dir(wgf_ref, bf_ref), load_dir(wgb_ref, bb_ref)), (of_ref, ob_ref),
               (af_ref, ab_ref), (lvf_ref, lvb_ref), v_s, (stage_refs[:N_STAGED + 1], stage_refs[N_STAGED + 1:]),
               n_heads=2, seq_all=seq_all, local_block=GLA_LOCAL_BLOCK)

    def post(j, carry):
        rows = pl.ds(pl.multiple_of(j * ROW_TILE, ROW_TILE), ROW_TILE)
        for hh in range(2):
            cols = slice(hh * LANES, (hh + 1) * LANES)
            o = of_ref[rows, cols] + ob_ref[rows, cols]
            var = jnp.mean(o * o, axis=-1, keepdims=True)
            o_ref[rows, cols] = (o * lax.rsqrt(var + EPS) * gain_ref[...] * _silu(g_ref[rows, cols])).astype(BF16)
        return carry
    lax.fori_loop(0, seq_all // ROW_TILE, post, 0, unroll=3)


def _gla(p3, w_gate_up, b_gate_up, gain, consts):
    batch, seq_all, _ = p3.shape
    pairs = B_HEADS // 2
    wg = jnp.zeros((2, pairs, LANES, LANES), F32)
    for d in range(2):
        w = w_gate_up[d].reshape(B_GATE_RANK, pairs, LANES).transpose(1, 0, 2)
        wg = wg.at[d, :, d * B_GATE_RANK:(d + 1) * B_GATE_RANK, :].set(w)
    wg = wg.astype(BF16)
    bias = b_gate_up.reshape(2, pairs, 1, LANES)
    a_f, a_b, lv_f, lv_b = consts
    col = lambda off: (lambda b, p: (b, 0, off + p))
    q_off = 3 * A_HEADS * LANES // LANES
    k_off = q_off + B_HEADS * B_DK // LANES
    v_off = (k_off * LANES + B_HEADS * B_DK) // (2 * LANES)
    g_off = v_off + pairs
    lr_off = (g_off + pairs) * 2
    wspec = pl.BlockSpec((None, LANES, LANES), lambda b, p: (p, 0, 0))
    bspec = pl.BlockSpec((None, 1, LANES), lambda b, p: (p, 0, 0))
    return pl.pallas_call(
        functools.partial(_gla_kernel, seq_all=seq_all),
        out_shape=jax.ShapeDtypeStruct((batch, seq_all, B_HEADS * LANES), BF16),
        grid=(batch, pairs),
        in_specs=[
            pl.BlockSpec((None, seq_all, LANES), col(q_off)),
            pl.BlockSpec((None, seq_all, LANES), col(k_off)),
            pl.BlockSpec((None, seq_all, 2 * LANES), col(v_off)),
            pl.BlockSpec((None, seq_all, 2 * LANES), col(g_off)),
            pl.BlockSpec((None, seq_all, LANES), lambda b, p: (b, 0, lr_off)),
            wspec, wspec, bspec, bspec,
            _resident((1, LANES)),
            _resident(a_f.shape), _resident(a_b.shape), _resident(lv_f.shape), _resident(lv_b.shape),
        ],
        out_specs=pl.BlockSpec((None, seq_all, 2 * LANES), lambda b, p: (b, 0, p)),
        scratch_shapes=_scan_scratch(seq_all, 2),
        compiler_params=_cparams(("parallel", "parallel")),
        name="gla",
    )(p3, p3, p3, p3, p3, wg[0], wg[1], bias[0], bias[1], gain.reshape(1, LANES), a_f, a_b, lv_f, lv_b)


def _hgrn_kernel(q_ref, ff_ref, fb_ref, i_ref, g_ref, lb_ref, gain_ref,
                 af_ref, ab_ref, lvf_ref, lvb_ref, o_ref, of_ref, ob_ref, v_s, *stage_refs, seq_all, layer):
    def lower_bound(d):
        raw = lb_ref[d]
        ex = jnp.exp(raw - jnp.max(raw, axis=0, keepdims=True))
        p = ex / jnp.sum(ex, axis=0, keepdims=True)
        return jnp.sum(p[1:layer + 1], axis=0, keepdims=True)

    def load_common(rows):
        return _silu(q_ref[rows, :]) * (C_DH ** -0.5), i_ref[rows, :].astype(BF16)

    def load_dir(f_ref, lb):
        def load(rows):
            f = lb + (1.0 - lb) * jax.nn.sigmoid(f_ref[rows, :])
            return 1.0 - f, jnp.log(f)
        return load

    _scan_loop(load_common, (load_dir(ff_ref, lower_bound(0)), load_dir(fb_ref, lower_bound(1))), (of_ref, ob_ref),
               (af_ref, ab_ref), (lvf_ref, lvb_ref), v_s, (stage_refs[:N_STAGED + 1], stage_refs[N_STAGED + 1:]),
               n_heads=1, seq_all=seq_all, local_block=HGRN_LOCAL_BLOCK)

    def post(j, carry):
        rows = pl.ds(pl.multiple_of(j * ROW_TILE, ROW_TILE), ROW_TILE)
        o = of_ref[rows, :] + ob_ref[rows, :]
        var = jnp.mean(o * o, axis=-1, keepdims=True)
        o_ref[rows, :] = (o * lax.rsqrt(var + EPS) * gain_ref[...] * _silu(g_ref[rows, :])).astype(BF16)
        return carry
    lax.fori_loop(0, seq_all // ROW_TILE, post, 0, unroll=3)


def _hgrn(p3, lb_raw, gain, consts, layer):
    batch, seq_all, _ = p3.shape
    a_f, a_b, lv_f, lv_b = consts
    col = lambda off: (lambda b, h: (b, 0, off + h))
    return pl.pallas_call(
        functools.partial(_hgrn_kernel, seq_all=seq_all, layer=layer),
        out_shape=jax.ShapeDtypeStruct((batch, seq_all, C_HEADS * LANES), BF16),
        grid=(batch, C_HEADS),
        in_specs=[pl.BlockSpec((None, seq_all, LANES), col(s * C_HEADS)) for s in range(5)] + [
            pl.BlockSpec((2, DEPTH, LANES), lambda b, h: (0, 0, h)),
            _resident((1, LANES)),
            _resident(a_f.shape), _resident(a_b.shape), _resident(lv_f.shape), _resident(lv_b.shape),
        ],
        out_specs=pl.BlockSpec((None, seq_all, LANES), lambda b, h: (b, 0, h)),
        scratch_shapes=_scan_scratch(seq_all, 1),
        compiler_params=_cparams(("parallel", "parallel")),
        name="hgrn",
    )(p3, p3, p3, p3, p3, lb_raw, gain.reshape(1, LANES), a_f, a_b, lv_f, lv_b)


def _rope_tables(seq):
    pos = jnp.arange(seq)
    row_ids = (pos // GRID_W).astype(F32)
    col_ids = (pos % GRID_W).astype(F32)
    n_axis = A_DQK // 2
    inv = ROPE_BASE ** (-jnp.arange(0, n_axis, 2, dtype=F32) / n_axis)
    ang_r = row_ids[:, None] * inv
    ang_c = col_ids[:, None] * inv
    zeros = jnp.zeros_like(ang_r)
    cos64 = jnp.concatenate([jnp.cos(ang_r)] * 2 + [jnp.cos(ang_c)] * 2, axis=-1)
    sa64 = jnp.concatenate([-jnp.sin(ang_r), zeros, -jnp.sin(ang_c), zeros], axis=-1)
    sb64 = jnp.concatenate([zeros, jnp.sin(ang_r), zeros, jnp.sin(ang_c)], axis=-1)
    ident = lambda v: jnp.full((CTX_LEN, LANES), v, F32)
    full = lambda t, v: jnp.concatenate([ident(v), jnp.tile(t, (1, 2))], axis=0)
    return full(cos64, 1.0), full(sa64, 0.0), full(sb64, 0.0)


def kernel(x, c, ctx, c_ctx, w_ada, b_ada, norm1_gain, norm2_gain, w_in_even, qk_gain_a, lambda_a, subln_gain_a, w_gate_up_b, b_gate_up_b, onorm_gain_b, w_out_even, w_in_odd, lb_raw_c, onorm_gain_c, w_out_odd, w_ffn_in, w_ffn_out):
    batch, seq, d = x.shape
    seq_all = CTX_LEN + seq
    assert d == D_MODEL and ctx.shape[1] == CTX_LEN == ROW_TILE == Q_TILE and seq % ROW_TILE == 0

    tiles = seq_all // ROW_TILE
    flat = lambda a: a.reshape(-1, a.shape[-1])
    xz = (flat(ctx), flat(x))
    mod_rows = -(-(batch + 1) // 8) * 8
    cc = jnp.zeros((mod_rows, d), F32).at[:batch].set(c).at[batch].set(c_ctx)
    mods = _ada(cc, w_ada, b_ada).reshape(DEPTH, mod_rows, 6, d)

    tabs = _rope_tables(seq)
    half = np.arange(LANES) // A_DQK
    bd = jnp.asarray((half[:, None] == half[None, :]) / A_DQK, BF16)
    consts_gla = _scan_consts(2)
    consts_hgrn = _scan_consts(1)

    for l in range(DEPTH):
        j = l // 2
        last = l == DEPTH - 1
        wi = w_ffn_in[l].astype(BF16)
        wf = w_ffn_out[l].astype(BF16)
        if l % 2 == 0:
            w = jnp.pad(w_in_even[j], ((0, 0), (0, EVEN_COLS_PAD - EVEN_COLS))).astype(BF16)
            p3 = _inproj(xz, mods[l], norm1_gain[l], w, batch, tiles,
                         qk=(tabs, qk_gain_a[j][0], qk_gain_a[j][1], bd)).reshape(batch, seq_all, EVEN_COLS_PAD)
            lambda_init = 0.8 - 0.6 * math.exp(-0.3 * l)
            oa_ctx, oa_lat = _attn(p3, qk_gain_a[j], lambda_a[j], subln_gain_a[j], lambda_init)
            ob = _gla(p3, w_gate_up_b[j], b_gate_up_b[j], onorm_gain_b[j], consts_gla)
            a_width = A_HEADS * LANES
            wo = w_out_even[j].astype(BF16)
            o_parts = [(flat(oa_ctx), flat(oa_lat)), flat(ob)]
            wo_parts = [wo[:a_width], wo[a_width:]]
        else:
            p3 = _inproj(xz, mods[l], norm1_gain[l], w_in_odd[j].astype(BF16), batch, tiles).reshape(batch, seq_all, ODD_COLS)
            oc = _hgrn(p3, lb_raw_c, onorm_gain_c[j], consts_hgrn, l)
            o_parts = [flat(oc)]
            wo_parts = [w_out_odd[j].astype(BF16)]
        xz = _post(o_parts, xz, mods[l], norm2_gain[l], wo_parts, wi, wf, batch, tiles, last)
    return xz.reshape(batch, seq, d)
```

```python
import functools
import math

import numpy as np
import jax
import jax.numpy as jnp
from jax import lax
from jax.experimental import pallas as pl
from jax.experimental.pallas import tpu as pltpu

F32 = jnp.float32
BF16 = jnp.bfloat16

D_MODEL = 1024
DEPTH = 4
CTX_LEN = 256
GRID_W = 64
A_HEADS = 4
A_DQK = 64
B_HEADS = 4
B_DK = 64
B_GATE_RANK = 16
B_GATE_NORM = 16.0
C_HEADS = 8
C_DH = 128
ROPE_BASE = 10000.0
EPS = 1e-6
D_FF = 2816
EVEN_COLS = 3104
EVEN_COLS_PAD = 3200
ODD_COLS = 5120

LANES = 128
ROW_TILE = 256
COL_CHUNK = 512
FF_CHUNK = 256
Q_TILE = 256
Q_TILES_PER_STEP = 8
SUM_ROWS = 16
SAFE_SCORE_LOG2 = 60.0
LOG2E = math.log2(math.e)
SCAN_CHUNK = 128
SCAN_UNROLL = 9
STAGE_UNROLL = 6
VMEM_LIMIT = 52 * 1024 * 1024


def _cparams(sem):
    return pltpu.CompilerParams(dimension_semantics=sem, vmem_limit_bytes=VMEM_LIMIT)


def _resident(shape):
    nd = len(shape)
    return pl.BlockSpec(shape, lambda *_: (0,) * nd, pipeline_mode=pl.Buffered(1))


def _silu(x):
    return x * jax.nn.sigmoid(x)


def _dot(a, b):
    return jnp.dot(a, b, preferred_element_type=F32)


def _dot_nt(a, b):
    return lax.dot_general(a, b, (((1,), (1,)), ((), ())), preferred_element_type=F32)


def _dot_tn(a, b):
    return lax.dot_general(a, b, (((0,), (0,)), ((), ())), preferred_element_type=F32)


def _ada_kernel(c_ref, w_ref, b_ref, o_ref):
    s = _silu(c_ref[...]).astype(BF16)
    o_ref[...] = _dot(s, w_ref[...].astype(BF16)) + b_ref[...]


def _ada(cc, w_ada, b_ada):
    depth, d, n = w_ada.shape
    rows = cc.shape[0]
    tn = 1536
    return pl.pallas_call(
        _ada_kernel,
        out_shape=jax.ShapeDtypeStruct((depth, rows, n), F32),
        grid=(depth, n // tn),
        in_specs=[
            pl.BlockSpec((rows, d), lambda l, j: (0, 0)),
            pl.BlockSpec((None, d, tn), lambda l, j: (l, 0, j)),
            pl.BlockSpec((None, 1, tn), lambda l, j: (l, 0, j)),
        ],
        out_specs=pl.BlockSpec((None, rows, tn), lambda l, j: (l, 0, j)),
        compiler_params=_cparams(("parallel", "parallel")),
        name="ada",
    )(cc, w_ada, b_ada.reshape(depth, 1, n))


def _norm_mod(x, gain, shift, scale):
    var = jnp.mean(x * x, axis=-1, keepdims=True)
    return (x * lax.rsqrt(var + EPS) * gain) * (1.0 + scale) + shift


def _row_sources(rt):
    return list(rt) if isinstance(rt, tuple) else [rt]


def _row_specs(rt, tile_fn, tiles):
    if not isinstance(rt, tuple):
        return [pl.BlockSpec((ROW_TILE, rt.shape[1]), lambda i: (tile_fn(i), 0))]
    sample = lambda i: tile_fn(i) // tiles
    latent = lambda i: sample(i) * (tiles - 1) + jnp.maximum(tile_fn(i) % tiles - 1, 0)
    return [pl.BlockSpec((ROW_TILE, rt[0].shape[1]), lambda i: (sample(i), 0)),
            pl.BlockSpec((ROW_TILE, rt[1].shape[1]), lambda i: (latent(i), 0))]


def _row_tile(refs, is_context):
    if len(refs) == 1:
        return refs[0][...]
    return jnp.where(is_context, refs[0][...], refs[1][...])


def _take(refs, counts):
    groups, pos = [], 0
    for n in counts:
        groups.append(refs[pos:pos + n])
        pos += n
    return groups, refs[pos:]


def _inproj_kernel(*refs, n_x, tiles, qk_heads):
    x_groups, rest = _take(refs, [n_x, n_x])
    mod_a_ref, mod_b_ref, gain_ref, w_ref = rest[:4]
    o_ref = rest[-1]
    step = pl.program_id(0)
    halves = []
    for t, m in enumerate((mod_a_ref, mod_b_ref)):
        x = _row_tile(x_groups[t], (2 * step + t) % tiles == 0)
        halves.append(_norm_mod(x, gain_ref[...], m[0:1, :], m[1:2, :]).astype(BF16))
    h = jnp.concatenate(halves, axis=0)
    ncols = w_ref.shape[1]
    for c0 in range(0, ncols, COL_CHUNK):
        c1 = min(c0 + COL_CHUNK, ncols)
        res = _dot(h, w_ref[:, c0:c1])
        if c1 > 2 * qk_heads * LANES:
            o_ref[:, c0:c1] = res
            continue
        c_ref, sa_ref, sb_ref, qg_ref, kg_ref, bd_ref = rest[4:10]
        gains = (qg_ref[...] * (A_DQK ** -0.5 * LOG2E), kg_ref[...])
        blocks = [res[:, b0 - c0:b0 - c0 + LANES] for b0 in range(c0, c1, LANES)]
        ms_all = _dot(jnp.concatenate([(x * x).astype(BF16) for x in blocks], axis=0), bd_ref[...])
        n_rows = 2 * ROW_TILE
        for i, x in enumerate(blocks):
            b0 = c0 + i * LANES
            y = x * lax.rsqrt(ms_all[i * n_rows:(i + 1) * n_rows] + EPS) * gains[b0 // (qk_heads * LANES)]
            for t in range(2):
                pos = pl.ds(pl.multiple_of(((2 * step + t) % tiles) * ROW_TILE, ROW_TILE), ROW_TILE)
                rows = slice(t * ROW_TILE, (t + 1) * ROW_TILE)
                yt = y[rows]
                o_ref[rows, b0:b0 + LANES] = (yt * c_ref[pos, :] + pltpu.roll(yt, LANES - 16, 1) * sa_ref[pos, :]
                                              + pltpu.roll(yt, 16, 1) * sb_ref[pos, :])


def _mod_row(i, tiles, batch):
    return jnp.where(i % tiles == 0, batch, i // tiles)


def _inproj(xz, mod, gain, w, batch, tiles, qk=None):
    d, ncols = w.shape
    rows = batch * tiles * ROW_TILE
    mod_spec = lambda t: pl.BlockSpec((None, 6, d), lambda i: (_mod_row(2 * i + t, tiles, batch), 0, 0))
    x_specs = [spec for t in range(2) for spec in _row_specs(xz, (lambda t: lambda i: 2 * i + t)(t), tiles)]
    qk_args = []
    if qk:
        tabs, q_gain, k_gain, bd = qk
        qk_args = [*tabs, jnp.tile(q_gain, 2).reshape(1, LANES), jnp.tile(k_gain, 2).reshape(1, LANES), bd]
    return pl.pallas_call(
        functools.partial(_inproj_kernel, n_x=len(_row_sources(xz)), tiles=tiles, qk_heads=A_HEADS if qk else 0),
        out_shape=jax.ShapeDtypeStruct((rows, ncols), F32),
        grid=(rows // (2 * ROW_TILE),),
        in_specs=x_specs + [mod_spec(0), mod_spec(1), _resident((1, d)), _resident((d, ncols))]
        + [_resident(a.shape) for a in qk_args],
        out_specs=pl.BlockSpec((2 * ROW_TILE, ncols), lambda i: (i, 0)),
        compiler_params=_cparams(("parallel",)),
        name="inproj",
    )(*(_row_sources(xz) * 2), mod, mod, gain.reshape(1, d), w, *qk_args)


def _post_kernel(*refs, n_src, tiles):
    groups, rest = _take(refs, [n for n in n_src for _ in range(2)])
    n_parts = len(n_src) - 1
    mod_refs, gain_ref = rest[0:2], rest[2]
    wo_refs = rest[3:3 + n_parts]
    wi_ref, wf_ref, out_ref = rest[3 + n_parts:]
    step = pl.program_id(0)
    is_context = [(2 * step + t) % tiles == 0 for t in range(2)]
    tile = lambda p, t: _row_tile(groups[2 * p + t], is_context[t])
    x1 = []
    for t in range(2):
        y = None
        for p, wo_ref in enumerate(wo_refs):
            part = _dot(tile(p, t), wo_ref[...])
            y = part if y is None else y + part
        x1.append(tile(n_parts, t) + mod_refs[t][2:3, :] * y)
    h = jnp.concatenate([_norm_mod(x1[t], gain_ref[...], mod_refs[t][3:4, :], mod_refs[t][4:5, :]).astype(BF16)
                         for t in range(2)], axis=0)
    acc = None
    for c0 in range(0, D_FF, FF_CHUNK):
        gate = _dot(h, wi_ref[:, c0:c0 + FF_CHUNK])
        up = _dot(h, wi_ref[:, D_FF + c0:D_FF + c0 + FF_CHUNK])
        a = (_silu(gate) * up).astype(BF16)
        part = _dot(a, wf_ref[c0:c0 + FF_CHUNK, :])
        acc = part if acc is None else acc + part
    for t in range(2):
        rows = slice(t * ROW_TILE, (t + 1) * ROW_TILE)
        out_ref[rows, :] = x1[t] + mod_refs[t][5:6, :] * acc[rows]


def _post(o_parts, xz, mod, gain, wo_parts, wi, wf, batch, tiles, latent_only):
    d = wi.shape[0]
    tensors = list(o_parts) + [xz]
    if latent_only:
        assert not any(isinstance(rt, tuple) for rt in tensors)
        out_tiles = tiles - 1
        src = lambda i: (i // out_tiles) * tiles + 1 + i % out_tiles
        mrow = lambda i: i // out_tiles
    else:
        out_tiles = tiles
        src = lambda i: i
        mrow = lambda i: _mod_row(i, tiles, batch)
    assert (batch * out_tiles) % 2 == 0
    tile_fn = lambda t: lambda i: src(2 * i + t)
    in_specs = [spec for rt in tensors for t in range(2) for spec in _row_specs(rt, tile_fn(t), tiles)]
    in_specs += [pl.BlockSpec((None, 6, d), (lambda t: lambda i: (mrow(2 * i + t), 0, 0))(t)) for t in range(2)]
    in_specs += [_resident((1, d))]
    in_specs += [_resident(w.shape) for w in wo_parts]
    in_specs += [_resident(wi.shape), _resident(wf.shape)]
    row_args = [a for rt in tensors for _ in range(2) for a in _row_sources(rt)]
    return pl.pallas_call(
        functools.partial(_post_kernel, n_src=tuple(len(_row_sources(rt)) for rt in tensors), tiles=tiles),
        out_shape=jax.ShapeDtypeStruct((batch * out_tiles * ROW_TILE, d), F32),
        grid=(batch * out_tiles // 2,),
        in_specs=in_specs,
        out_specs=pl.BlockSpec((2 * ROW_TILE, d), lambda i: (i, 0)),
        compiler_params=_cparams(("parallel",)),
        name="post",
    )(*row_args, mod, mod, gain.reshape(1, d), *wo_parts, wi, wf)


def _attn_kernel(qc_ref, *refs, lambda_init, seq_all):
    q_refs = refs[:Q_TILES_PER_STEP]
    (k_ref, v_ref, qg_ref, kg_ref, sg_ref, lam_ref, octx_ref, olat_ref, kb_ref, vt_ref) = refs[Q_TILES_PER_STEP:]
    step = pl.program_id(2)

    @pl.when(step == 0)
    def _prep_keys():
        def body(r, carry):
            rows = pl.ds(pl.multiple_of(r * Q_TILE, Q_TILE), Q_TILE)
            kb_ref[rows, :] = k_ref[rows, :].astype(BF16)
            vt_ref[0:LANES, rows] = v_ref[rows, :].T.astype(BF16)
            return carry
        lax.fori_loop(0, seq_all // Q_TILE, body, 0, unroll=3)
        vt_ref[LANES:, :] = jnp.ones((SUM_ROWS, seq_all), BF16)

    lam_v = lam_ref[...]
    e1 = jnp.exp(jnp.sum(lam_v[0:1, :] * lam_v[1:2, :], axis=-1, keepdims=True))
    e2 = jnp.exp(jnp.sum(lam_v[2:3, :] * lam_v[3:4, :], axis=-1, keepdims=True))
    lam = e1 - e2 + lambda_init

    def scores(q_ref, n_keys):
        qn = q_ref[...]
        lane = lax.broadcasted_iota(jnp.int32, qn.shape, 1)
        q2 = jnp.concatenate([jnp.where(lane < A_DQK, qn, 0.0).astype(BF16),
                              jnp.where(lane >= A_DQK, qn, 0.0).astype(BF16)], axis=0)
        return _dot_nt(kb_ref[0:n_keys, :], q2)

    def outputs(s, n_keys, shift):
        if shift is None:
            shift = jnp.max(s, axis=0, keepdims=True)
        p = jnp.exp2(s - shift).astype(BF16)
        pv = _dot(vt_ref[:, 0:n_keys], p)
        pv = pv[0:LANES] / pv[LANES:LANES + 1]
        o = pv[:, :Q_TILE] - lam * pv[:, Q_TILE:]
        var = jnp.mean(o * o, axis=0, keepdims=True)
        o = o * lax.rsqrt(var + EPS) * (sg_ref[...] * (1.0 - lambda_init))
        return o.T.astype(BF16)

    def run(shift):
        s_next = scores(q_refs[0], seq_all)
        for t in range(Q_TILES_PER_STEP):
            s = s_next
            if t + 1 < Q_TILES_PER_STEP:
                s_next = scores(q_refs[t + 1], seq_all)
            olat_ref[t * Q_TILE:(t + 1) * Q_TILE, :] = outputs(s, seq_all, shift)

        @pl.when(step == 0)
        def _context_tile():
            octx_ref[...] = outputs(scores(qc_ref, CTX_LEN), CTX_LEN, shift)

    bound = (A_DQK ** 0.5 * LOG2E) * (jnp.max(jnp.abs(qg_ref[...]), axis=-1, keepdims=True)
                                      * jnp.max(jnp.abs(kg_ref[...]), axis=-1, keepdims=True)) + 1.0
    small = jnp.max(bound) < SAFE_SCORE_LOG2
    pl.when(small)(lambda: run(bound))
    pl.when(jnp.logical_not(small))(lambda: run(None))


def _attn(p3, qk_gain, lam, sub_gain, lambda_init):
    batch, seq_all, _ = p3.shape
    seq = seq_all - CTX_LEN
    n_t = Q_TILES_PER_STEP
    assert seq % (n_t * Q_TILE) == 0
    qg = jnp.tile(qk_gain[0], 2).reshape(1, LANES)
    kg = jnp.tile(qk_gain[1], 2).reshape(1, LANES)
    kern = functools.partial(_attn_kernel, lambda_init=lambda_init, seq_all=seq_all)
    q_spec = lambda t: pl.BlockSpec((None, Q_TILE, LANES), lambda b, h, i: (b, 1 + n_t * i + t, h))
    o_ctx, o_lat = pl.pallas_call(
        kern,
        out_shape=(jax.ShapeDtypeStruct((batch, CTX_LEN, A_HEADS * LANES), BF16),
                   jax.ShapeDtypeStruct((batch, seq, A_HEADS * LANES), BF16)),
        grid=(batch, A_HEADS, seq // (n_t * Q_TILE)),
        in_specs=[pl.BlockSpec((None, Q_TILE, LANES), lambda b, h, i: (b, 0, h))]
        + [q_spec(t) for t in range(n_t)] + [
            pl.BlockSpec((None, seq_all, LANES), lambda b, h, i: (b, 0, A_HEADS + h)),
            pl.BlockSpec((None, seq_all, LANES), lambda b, h, i: (b, 0, 2 * A_HEADS + h)),
            _resident((1, LANES)), _resident((1, LANES)), _resident((LANES, 1)), _resident((4, A_DQK)),
        ],
        out_specs=(pl.BlockSpec((None, CTX_LEN, LANES), lambda b, h, i: (b, 0, h)),
                   pl.BlockSpec((None, n_t * Q_TILE, LANES), lambda b, h, i: (b, i, h))),
        scratch_shapes=[pltpu.VMEM((seq_all, LANES), BF16), pltpu.VMEM((LANES + SUM_ROWS, seq_all), BF16)],
        compiler_params=_cparams(("parallel", "parallel", "arbitrary")),
        name="attn",
    )(*([p3] * (n_t + 3)), qg, kg, sub_gain.reshape(LANES, 1), lam)
    return o_ctx, o_lat


MATMUL_LEVELS = 2
HGRN_LOCAL_BLOCK = 64
GLA_LOCAL_BLOCK = 128
N_STAGED = 5
SAFE_LOG2 = 100.0


def _scan_levels():
    return int(math.log2(SCAN_CHUNK))


def _scan_consts(n_heads):
    c = SCAN_CHUNK
    t = np.arange(c)[:, None]
    u = np.arange(c)[None, :]
    blocks_f = [u <= t]
    blocks_b = [u >= t]
    for lvl in range(1, MATMUL_LEVELS + 1):
        h = 1 << (lvl - 1)
        m = (t // (2 * h)) * (2 * h) + h
        second = (t % (2 * h)) >= h
        blocks_f.append(np.where(second, (u >= m) & (u <= t), (u > t) & (u <= m - 1)))
        blocks_b.append(np.where(second, (u >= m) & (u < t), (u >= t) & (u <= m - 1)))
    a_f = np.tile(np.concatenate(blocks_f, axis=0).astype(np.float32), (1, 2))
    a_b = np.tile(np.concatenate(blocks_b, axis=0).astype(np.float32), (1, 2))
    x = t ^ u
    lv = np.where(x > 0, np.floor(np.log2(np.maximum(x, 1))).astype(np.int32) + 1, 0)
    lv_f = np.where(u <= t, lv, -1).astype(np.int32)
    lv_b = np.where(u >= t, lv, -1).astype(np.int32)
    lv_f = np.tile(lv_f, (n_heads, 1))
    lv_b = np.tile(lv_b, (n_heads, 1))
    return (jnp.asarray(a_f, BF16), jnp.asarray(a_b, BF16), jnp.asarray(lv_f), jnp.asarray(lv_b))


def _scan_sums(g, a):
    g2 = g * LOG2E
    g_hi = g2.astype(BF16)
    g_lo = (g2 - g_hi.astype(F32)).astype(BF16)
    return _dot(a, jnp.concatenate([g_hi, g_lo], axis=0))


def _level_exponent(b, lvl, backward):
    c = SCAN_CHUNK
    h = 1 << (lvl - 1)
    r0 = h if backward else h - 1
    b3 = b.reshape(c // (2 * h), 2 * h, LANES)
    ref = b3[:, r0:r0 + 1, :]
    if h % 8 == 0:
        first, second = b3[:, :h, :], b3[:, h:, :]
        x = jnp.concatenate([first - ref, ref - second] if backward else [ref - first, second - ref], axis=1)
    else:
        x = -jnp.abs(b3 - ref)
    return x.reshape(c, LANES)


def _by_head(xb, n_heads):
    if n_heads == 1:
        return xb
    dk = LANES // n_heads
    lane = lax.broadcasted_iota(jnp.int32, xb.shape, 1)
    masks = [jnp.where((lane >= hh * dk) & (lane < (hh + 1) * dk), 1.0, 0.0).astype(BF16) for hh in range(n_heads)]
    return jnp.concatenate([xb * hm for hm in masks], axis=0)


def _head_block_mask(shape, row_block, col_block):
    r = lax.broadcasted_iota(jnp.int32, shape, 0) // row_block
    cc = lax.broadcasted_iota(jnp.int32, shape, 1) // col_block
    return r == cc


def _level_chunk(qb, kb, b, fine, vb, st, lv_ref, *, n_heads, backward):
    c = SCAN_CHUNK
    nl = _scan_levels()
    edge = b[0:1, :] if backward else b[c - 1:c, :]
    e_in = jnp.exp2(b).astype(BF16)
    e_out = jnp.exp2(edge - b).astype(BF16)

    def level_decay(lvl):
        if lvl <= MATMUL_LEVELS:
            return jnp.exp2(fine[(lvl - 1) * c:lvl * c]).astype(BF16)
        return jnp.exp2(_level_exponent(b, lvl, backward)).astype(BF16)

    o = _dot_nt(qb * e_in, st.astype(BF16))

    lv = lv_ref[...]
    att = jnp.zeros((n_heads * c, c), F32)
    for lvl in range(nl + 1):
        if lvl == 0:
            qh, kh = qb, kb
        else:
            el = level_decay(lvl)
            qh, kh = qb * el, kb * el
        att = jnp.where(lv == lvl, _dot_nt(_by_head(qh, n_heads), kh), att)
    attb = att.astype(BF16)
    intra = [_dot(attb[hh * c:(hh + 1) * c], vb[:, hh * LANES:(hh + 1) * LANES]) for hh in range(n_heads)]
    o = o + (intra[0] if n_heads == 1 else jnp.concatenate(intra, axis=1))

    dec = jnp.exp2(edge)
    upd = _dot_tn(vb, kb * e_out)
    if n_heads > 1:
        upd = jnp.where(_head_block_mask(upd.shape, LANES, LANES // n_heads), upd, 0.0)
    return o, dec * st + upd


def _shortcut_pairs(qp, kp, qt, kd, vb, lv_ref, *, n_heads):
    lv = lv_ref[...]
    rows = lv.shape[0]
    if qt is None:
        att = jnp.where(lv >= 0, _dot_nt(_by_head(qp, n_heads), kp), 0.0).astype(BF16)
    else:
        both = _dot_nt(jnp.concatenate([_by_head(qp, n_heads), _by_head(qt, n_heads)], axis=0), kp)
        att = jnp.where(lv == _scan_levels(), both[rows:], jnp.where(lv >= 0, both[:rows], 0.0)).astype(BF16)
    upd = _dot_tn(kd, vb)
    if n_heads > 1:
        upd = jnp.where(_head_block_mask(upd.shape, LANES // n_heads, LANES), upd, 0.0)
    return att, upd


def _shortcut_output(att, upd, qe, vb, dec, s, *, n_heads):
    c = SCAN_CHUNK
    sb = s.astype(BF16)
    outs = []
    for hh in range(n_heads):
        cols = slice(hh * LANES, (hh + 1) * LANES)
        lhs = jnp.concatenate([att[hh * c:(hh + 1) * c], qe], axis=1)
        rhs = jnp.concatenate([vb[:, cols], sb[:, cols]], axis=0)
        outs.append(_dot(lhs, rhs))
    o = outs[0] if n_heads == 1 else jnp.concatenate(outs, axis=1)
    if n_heads > 1:
        dec = jnp.concatenate([dec] * n_heads, axis=1)
    return o, dec * s + upd


def _scan_loop(load_common, load_dir, out_refs, a_refs, lv_refs, v_s, stage_refs, *, n_heads, seq_all, local_block):
    c = SCAN_CHUNK
    assert local_block in (c, c // 2)
    two_blocks = local_block < c
    n = seq_all // c
    n_ctx = CTX_LEN // c

    def chunk_rows(j):
        return pl.ds(pl.multiple_of(j * c, c), c)

    def scan(prepare, finish, state0, unroll):
        def body(i, sts):
            sts = list(sts)
            todo = []
            for u in range(unroll):
                j = i * unroll + u
                jb = jnp.where(j < n_ctx, n_ctx - 1 - j, n - 1 - (j - n_ctx))
                for d, jj in ((0, j), (1, jb)):
                    rows = chunk_rows(jj)
                    todo.append((d, rows, prepare(d, rows)))
            for d, rows, values in todo:
                out_refs[d][rows, :], sts[d] = finish(d, rows, values, sts[d])
            return tuple(sts)
        lax.fori_loop(0, n // unroll, body, (state0, state0))

    def stage(j, worst):
        gated = []
        for u in range(STAGE_UNROLL):
            rows = chunk_rows(j * STAGE_UNROLL + u)
            q, vb = load_common(rows)
            v_s[rows, :] = vb
            gated.append((rows, q, [load_dir[d](rows) for d in (0, 1)]))
        summed = [(rows, q, [(k, _scan_sums(g, a_refs[d][0:c, :])) for d, (k, g) in enumerate(kg)])
                  for rows, q, kg in gated]
        for rows, q, kb in summed:
            for d, (k, b) in enumerate(kb):
                b3 = b.reshape(c // local_block, local_block, LANES)
                r0 = local_block // 2 if d else local_block // 2 - 1
                ref = b3[:, r0:r0 + 1, :]
                x = b3 - ref
                edge = b[0:1, :] if d else b[c - 1:c, :]
                qp = q * jnp.exp2(x).reshape(c, LANES)
                kp = k * jnp.exp2(-x).reshape(c, LANES)
                entry = jnp.broadcast_to(jnp.exp2(ref), b3.shape).reshape(c, LANES).astype(BF16)
                leave = jnp.broadcast_to(jnp.exp2(edge - ref), b3.shape).reshape(c, LANES).astype(BF16)
                qp_s, kp_s, qt_s, qe_s, kd_s, dec_s = stage_refs[d]
                qp = qp.astype(BF16)
                kp = kp.astype(BF16)
                qp_s[rows, :] = qp
                kp_s[rows, :] = kp
                if two_blocks:
                    cross = jnp.exp2(ref[0] - ref[1] if d else ref[1] - ref[0])
                    qt_s[rows, :] = qp * jnp.broadcast_to(cross, (c, LANES)).astype(BF16)
                qe_s[rows, :] = qp * entry
                kd_s[rows, :] = kp * leave
                dec_s[rows, :] = jnp.broadcast_to(jnp.exp2(edge), (LANES, LANES)).T
                ends = jnp.abs(jnp.concatenate([x[:, 0:1, :], x[:, local_block - 1:, :]], axis=1))
                worst = jnp.maximum(worst, jnp.max(ends, axis=(0, 1), keepdims=True).reshape(1, LANES))
        return worst
    worst = lax.fori_loop(0, n // STAGE_UNROLL, stage, jnp.zeros((8, LANES), F32))
    safe = jnp.max(worst) < SAFE_LOG2

    def shortcut_pairs(d, rows):
        qp_s, kp_s, qt_s, _, kd_s, _ = stage_refs[d]
        return _shortcut_pairs(qp_s[rows, :], kp_s[rows, :], qt_s[rows, :] if two_blocks else None, kd_s[rows, :],
                               v_s[rows, :], lv_refs[d], n_heads=n_heads)

    def shortcut_output(d, rows, values, s):
        _, _, _, qe_s, _, dec_s = stage_refs[d]
        return _shortcut_output(*values, qe_s[rows, :], v_s[rows, :], dec_s[rows, :], s, n_heads=n_heads)

    def levels(d, rows, values, st):
        q, vb = load_common(rows)
        k, g = load_dir[d](rows)
        sums = _scan_sums(g, a_refs[d][...])
        return _level_chunk(q.astype(BF16), k.astype(BF16), sums[0:c], sums[c:], vb, st, lv_refs[d],
                            n_heads=n_heads, backward=bool(d))

    pl.when(safe)(lambda: scan(shortcut_pairs, shortcut_output, jnp.zeros((LANES, n_heads * LANES), F32), SCAN_UNROLL))
    pl.when(jnp.logical_not(safe))(lambda: scan(lambda d, rows: None, levels, jnp.zeros((n_heads * LANES, LANES), F32), 1))


def _scan_scratch(seq_all, n_heads):
    assert SCAN_CHUNK == LANES
    width = n_heads * LANES
    out = [pltpu.VMEM((seq_all, width), F32)] * 2
    stage = [pltpu.VMEM((seq_all, LANES), BF16)] * N_STAGED + [pltpu.VMEM((seq_all, LANES), F32)]
    return out + [pltpu.VMEM((seq_all, width), BF16)] + stage * 2


def _log_sigmoid(x):
    return jnp.minimum(x, 0.0) - jnp.log(1.0 + jnp.exp(-jnp.abs(x)))


def _gla_kernel(q_ref, k_ref, v_ref, g_ref, lr_ref, wgf_ref, wgb_ref, bf_ref, bb_ref, gain_ref,
                af_ref, ab_ref, lvf_ref, lvb_ref, o_ref, of_ref, ob_ref, v_s, *stage_refs, seq_all):
    def load_common(rows):
        return q_ref[rows, :] * (B_DK ** -0.5), v_ref[rows, :].astype(BF16)

    def load_dir(wg_ref, b_ref):
        def load(rows):
            pre = _dot(lr_ref[rows, :].astype(BF16), wg_ref[...]) + b_ref[...]
            return k_ref[rows, :], _log_sigmoid(pre) * (1.0 / B_GATE_NORM)
        return load

    _scan_loop(load_common, (load_dir(wgf_ref, bf_ref), load_dir(wgb_ref, bb_ref)), (of_ref, ob_ref),
               (af_ref, ab_ref), (lvf_ref, lvb_ref), v_s, (stage_refs[:N_STAGED + 1], stage_refs[N_STAGED + 1:]),
               n_heads=2, seq_all=seq_all, local_block=GLA_LOCAL_BLOCK)

    def post(j, carry):
        rows = pl.ds(pl.multiple_of(j * ROW_TILE, ROW_TILE), ROW_TILE)
        for hh in range(2):
            cols = slice(hh * LANES, (hh + 1) * LANES)
            o = of_ref[rows, cols] + ob_ref[rows, cols]
            var = jnp.mean(o * o, axis=-1, keepdims=True)
            o_ref[rows, cols] = (o * lax.rsqrt(var + EPS) * gain_ref[...] * _silu(g_ref[rows, cols])).astype(BF16)
        return carry
    lax.fori_loop(0, seq_all // ROW_TILE, post, 0, unroll=3)


def _gla(p3, w_gate_up, b_gate_up, gain, consts):
    batch, seq_all, _ = p3.shape
    pairs = B_HEADS // 2
    wg = jnp.zeros((2, pairs, LANES, LANES), F32)
    for d in range(2):
        w = w_gate_up[d].reshape(B_GATE_RANK, pairs, LANES).transpose(1, 0, 2)
        wg = wg.at[d, :, d * B_GATE_RANK:(d + 1) * B_GATE_RANK, :].set(w)
    wg = wg.astype(BF16)
    bias = b_gate_up.reshape(2, pairs, 1, LANES)
    a_f, a_b, lv_f, lv_b = consts
    col = lambda off: (lambda b, p: (b, 0, off + p))
    q_off = 3 * A_HEADS * LANES // LANES
    k_off = q_off + B_HEADS * B_DK // LANES
    v_off = (k_off * LANES + B_HEADS * B_DK) // (2 * LANES)
    g_off = v_off + pairs
    lr_off = (g_off + pairs) * 2
    wspec = pl.BlockSpec((None, LANES, LANES), lambda b, p: (p, 0, 0))
    bspec = pl.BlockSpec((None, 1, LANES), lambda b, p: (p, 0, 0))
    return pl.pallas_call(
        functools.partial(_gla_kernel, seq_all=seq_all),
        out_shape=jax.ShapeDtypeStruct((batch, seq_all, B_HEADS * LANES), BF16),
        grid=(batch, pairs),
        in_specs=[
            pl.BlockSpec((None, seq_all, LANES), col(q_off)),
            pl.BlockSpec((None, seq_all, LANES), col(k_off)),
            pl.BlockSpec((None, seq_all, 2 * LANES), col(v_off)),
            pl.BlockSpec((None, seq_all, 2 * LANES), col(g_off)),
            pl.BlockSpec((None, seq_all, LANES), lambda b, p: (b, 0, lr_off)),
            wspec, wspec, bspec, bspec,
            _resident((1, LANES)),
            _resident(a_f.shape), _resident(a_b.shape), _resident(lv_f.shape), _resident(lv_b.shape),
        ],
        out_specs=pl.BlockSpec((None, seq_all, 2 * LANES), lambda b, p: (b, 0, p)),
        scratch_shapes=_scan_scratch(seq_all, 2),
        compiler_params=_cparams(("parallel", "parallel")),
        name="gla",
    )(p3, p3, p3, p3, p3, wg[0], wg[1], bias[0], bias[1], gain.reshape(1, LANES), a_f, a_b, lv_f, lv_b)


def _hgrn_kernel(q_ref, ff_ref, fb_ref, i_ref, g_ref, lb_ref, gain_ref,
                 af_ref, ab_ref, lvf_ref, lvb_ref, o_ref, of_ref, ob_ref, v_s, *stage_refs, seq_all, layer):
    def lower_bound(d):
        raw = lb_ref[d]
        ex = jnp.exp(raw - jnp.max(raw, axis=0, keepdims=True))
        p = ex / jnp.sum(ex, axis=0, keepdims=True)
        return jnp.sum(p[1:layer + 1], axis=0, keepdims=True)

    def load_common(rows):
        return _silu(q_ref[rows, :]) * (C_DH ** -0.5), i_ref[rows, :].astype(BF16)

    def load_dir(f_ref, lb):
        def load(rows):
            f = lb + (1.0 - lb) * jax.nn.sigmoid(f_ref[rows, :])
            return 1.0 - f, jnp.log(f)
        return load

    _scan_loop(load_common, (load_dir(ff_ref, lower_bound(0)), load_dir(fb_ref, lower_bound(1))), (of_ref, ob_ref),
               (af_ref, ab_ref), (lvf_ref, lvb_ref), v_s, (stage_refs[:N_STAGED + 1], stage_refs[N_STAGED + 1:]),
               n_heads=1, seq_all=seq_all, local_block=HGRN_LOCAL_BLOCK)

    def post(j, carry):
        rows = pl.ds(pl.multiple_of(j * ROW_TILE, ROW_TILE), ROW_TILE)
        o = of_ref[rows, :] + ob_ref[rows, :]
        var = jnp.mean(o * o, axis=-1, keepdims=True)
        o_ref[rows, :] = (o * lax.rsqrt(var + EPS) * gain_ref[...] * _silu(g_ref[rows, :])).astype(BF16)
        return carry
    lax.fori_loop(0, seq_all // ROW_TILE, post, 0, unroll=3)


def _hgrn(p3, lb_raw, gain, consts, layer):
    batch, seq_all, _ = p3.shape
    a_f, a_b, lv_f, lv_b = consts
    col = lambda off: (lambda b, h: (b, 0, off + h))
    return pl.pallas_call(
        functools.partial(_hgrn_kernel, seq_all=seq_all, layer=layer),
        out_shape=jax.ShapeDtypeStruct((batch, seq_all, C_HEADS * LANES), BF16),
        grid=(batch, C_HEADS),
        in_specs=[pl.BlockSpec((None, seq_all, LANES), col(s * C_HEADS)) for s in range(5)] + [
            pl.BlockSpec((2, DEPTH, LANES), lambda b, h: (0, 0, h)),
            _resident((1, LANES)),
            _resident(a_f.shape), _resident(a_b.shape), _resident(lv_f.shape), _resident(lv_b.shape),
        ],
        out_specs=pl.BlockSpec((None, seq_all, LANES), lambda b, h: (b, 0, h)),
        scratch_shapes=_scan_scratch(seq_all, 1),
        compiler_params=_cparams(("parallel", "parallel")),
        name="hgrn",
    )(p3, p3, p3, p3, p3, lb_raw, gain.reshape(1, LANES), a_f, a_b, lv_f, lv_b)


def _rope_tables(seq):
    pos = jnp.arange(seq)
    row_ids = (pos // GRID_W).astype(F32)
    col_ids = (pos % GRID_W).astype(F32)
    n_axis = A_DQK // 2
    inv = ROPE_BASE ** (-jnp.arange(0, n_axis, 2, dtype=F32) / n_axis)
    ang_r = row_ids[:, None] * inv
    ang_c = col_ids[:, None] * inv
    zeros = jnp.zeros_like(ang_r)
    cos64 = jnp.concatenate([jnp.cos(ang_r)] * 2 + [jnp.cos(ang_c)] * 2, axis=-1)
    sa64 = jnp.concatenate([-jnp.sin(ang_r), zeros, -jnp.sin(ang_c), zeros], axis=-1)
    sb64 = jnp.concatenate([zeros, jnp.sin(ang_r), zeros, jnp.sin(ang_c)], axis=-1)
    ident = lambda v: jnp.full((CTX_LEN, LANES), v, F32)
    full = lambda t, v: jnp.concatenate([ident(v), jnp.tile(t, (1, 2))], axis=0)
    return full(cos64, 1.0), full(sa64, 0.0), full(sb64, 0.0)


def kernel(x, c, ctx, c_ctx, w_ada, b_ada, norm1_gain, norm2_gain, w_in_even, qk_gain_a, lambda_a, subln_gain_a, w_gate_up_b, b_gate_up_b, onorm_gain_b, w_out_even, w_in_odd, lb_raw_c, onorm_gain_c, w_out_odd, w_ffn_in, w_ffn_out):
    batch, seq, d = x.shape
    seq_all = CTX_LEN + seq
    assert d == D_MODEL and ctx.shape[1] == CTX_LEN == ROW_TILE == Q_TILE and seq % ROW_TILE == 0

    tiles = seq_all // ROW_TILE
    flat = lambda a: a.reshape(-1, a.shape[-1])
    xz = (flat(ctx), flat(x))
    mod_rows = -(-(batch + 1) // 8) * 8
    cc = jnp.zeros((mod_rows, d), F32).at[:batch].set(c).at[batch].set(c_ctx)
    mods = _ada(cc, w_ada, b_ada).reshape(DEPTH, mod_rows, 6, d)

    tabs = _rope_tables(seq)
    half = np.arange(LANES) // A_DQK
    bd = jnp.asarray((half[:, None] == half[None, :]) / A_DQK, BF16)
    consts_gla = _scan_consts(2)
    consts_hgrn = _scan_consts(1)

    for l in range(DEPTH):
        j = l // 2
        last = l == DEPTH - 1
        wi = w_ffn_in[l].astype(BF16)
        wf = w_ffn_out[l].astype(BF16)
        if l % 2 == 0:
            w = jnp.pad(w_in_even[j], ((0, 0), (0, EVEN_COLS_PAD - EVEN_COLS))).astype(BF16)
            p3 = _inproj(xz, mods[l], norm1_gain[l], w, batch, tiles,
                         qk=(tabs, qk_gain_a[j][0], qk_gain_a[j][1], bd)).reshape(batch, seq_all, EVEN_COLS_PAD)
            lambda_init = 0.8 - 0.6 * math.exp(-0.3 * l)
            oa_ctx, oa_lat = _attn(p3, qk_gain_a[j], lambda_a[j], subln_gain_a[j], lambda_init)
            ob = _gla(p3, w_gate_up_b[j], b_gate_up_b[j], onorm_gain_b[j], consts_gla)
            a_width = A_HEADS * LANES
            wo = w_out_even[j].astype(BF16)
            o_parts = [(flat(oa_ctx), flat(oa_lat)), flat(ob)]
            wo_parts = [wo[:a_width], wo[a_width:]]
        else:
            p3 = _inproj(xz, mods[l], norm1_gain[l], w_in_odd[j].astype(BF16), batch, tiles).reshape(batch, seq_all, ODD_COLS)
            oc = _hgrn(p3, lb_raw_c, onorm_gain_c[j], consts_hgrn, l)
            o_parts = [flat(oc)]
            wo_parts = [w_out_odd[j].astype(BF16)]
        xz = _post(o_parts, xz, mods[l], norm2_gain[l], wo_parts, wi, wf, batch, tiles, last)
    return xz.reshape(batch, seq, d)
```

```python
import functools
import math

import numpy as np
import jax
import jax.numpy as jnp
from jax import lax
from jax.experimental import pallas as pl
from jax.experimental.pallas import tpu as pltpu

F32 = jnp.float32
BF16 = jnp.bfloat16

D_MODEL = 1024
DEPTH = 4
CTX_LEN = 256
GRID_W = 64
A_HEADS = 4
A_DQK = 64
B_HEADS = 4
B_DK = 64
B_GATE_RANK = 16
B_GATE_NORM = 16.0
C_HEADS = 8
C_DH = 128
ROPE_BASE = 10000.0
EPS = 1e-6
D_FF = 2816
EVEN_COLS = 3104
EVEN_COLS_PAD = 3200
ODD_COLS = 5120

LANES = 128
ROW_TILE = 256
COL_CHUNK = 512
FF_CHUNK = 256
Q_TILE = 256
Q_TILES_PER_STEP = 8
SUM_ROWS = 16
SAFE_SCORE_LOG2 = 60.0
LOG2E = math.log2(math.e)
SCAN_CHUNK = 128
SCAN_UNROLL = 9
STAGE_UNROLL = 9
VMEM_LIMIT = 52 * 1024 * 1024


def _cparams(sem):
    return pltpu.CompilerParams(dimension_semantics=sem, vmem_limit_bytes=VMEM_LIMIT)


def _resident(shape):
    nd = len(shape)
    return pl.BlockSpec(shape, lambda *_: (0,) * nd, pipeline_mode=pl.Buffered(1))


def _silu(x):
    return x * jax.nn.sigmoid(x)


def _dot(a, b):
    return jnp.dot(a, b, preferred_element_type=F32)


def _dot_nt(a, b):
    return lax.dot_general(a, b, (((1,), (1,)), ((), ())), preferred_element_type=F32)


def _dot_tn(a, b):
    return lax.dot_general(a, b, (((0,), (0,)), ((), ())), preferred_element_type=F32)


def _ada_kernel(c_ref, w_ref, b_ref, o_ref):
    s = _silu(c_ref[...]).astype(BF16)
    o_ref[...] = _dot(s, w_ref[...].astype(BF16)) + b_ref[...]


def _ada(cc, w_ada, b_ada):
    depth, d, n = w_ada.shape
    rows = cc.shape[0]
    tn = 1536
    return pl.pallas_call(
        _ada_kernel,
        out_shape=jax.ShapeDtypeStruct((depth, rows, n), F32),
        grid=(depth, n // tn),
        in_specs=[
            pl.BlockSpec((rows, d), lambda l, j: (0, 0)),
            pl.BlockSpec((None, d, tn), lambda l, j: (l, 0, j)),
            pl.BlockSpec((None, 1, tn), lambda l, j: (l, 0, j)),
        ],
        out_specs=pl.BlockSpec((None, rows, tn), lambda l, j: (l, 0, j)),
        compiler_params=_cparams(("parallel", "parallel")),
        name="ada",
    )(cc, w_ada, b_ada.reshape(depth, 1, n))


def _norm_mod(x, gain, shift, scale):
    var = jnp.mean(x * x, axis=-1, keepdims=True)
    return (x * lax.rsqrt(var + EPS) * gain) * (1.0 + scale) + shift


def _row_sources(rt):
    return list(rt) if isinstance(rt, tuple) else [rt]


def _row_specs(rt, tile_fn, tiles):
    if not isinstance(rt, tuple):
        return [pl.BlockSpec((ROW_TILE, rt.shape[1]), lambda i: (tile_fn(i), 0))]
    sample = lambda i: tile_fn(i) // tiles
    latent = lambda i: sample(i) * (tiles - 1) + jnp.maximum(tile_fn(i) % tiles - 1, 0)
    return [pl.BlockSpec((ROW_TILE, rt[0].shape[1]), lambda i: (sample(i), 0)),
            pl.BlockSpec((ROW_TILE, rt[1].shape[1]), lambda i: (latent(i), 0))]


def _row_tile(refs, is_context):
    if len(refs) == 1:
        return refs[0][...]
    return jnp.where(is_context, refs[0][...], refs[1][...])


def _take(refs, counts):
    groups, pos = [], 0
    for n in counts:
        groups.append(refs[pos:pos + n])
        pos += n
    return groups, refs[pos:]


def _inproj_kernel(*refs, n_x, tiles, qk_heads):
    x_groups, rest = _take(refs, [n_x, n_x])
    mod_a_ref, mod_b_ref, gain_ref, w_ref = rest[:4]
    o_ref = rest[-1]
    step = pl.program_id(0)
    halves = []
    for t, m in enumerate((mod_a_ref, mod_b_ref)):
        x = _row_tile(x_groups[t], (2 * step + t) % tiles == 0)
        halves.append(_norm_mod(x, gain_ref[...], m[0:1, :], m[1:2, :]).astype(BF16))
    h = jnp.concatenate(halves, axis=0)
    ncols = w_ref.shape[1]
    for c0 in range(0, ncols, COL_CHUNK):
        c1 = min(c0 + COL_CHUNK, ncols)
        res = _dot(h, w_ref[:, c0:c1])
        if c1 > 2 * qk_heads * LANES:
            o_ref[:, c0:c1] = res
            continue
        c_ref, sa_ref, sb_ref, qg_ref, kg_ref, bd_ref = rest[4:10]
        gains = (qg_ref[...] * (A_DQK ** -0.5 * LOG2E), kg_ref[...])
        blocks = [res[:, b0 - c0:b0 - c0 + LANES] for b0 in range(c0, c1, LANES)]
        ms_all = _dot(jnp.concatenate([(x * x).astype(BF16) for x in blocks], axis=0), bd_ref[...])
        n_rows = 2 * ROW_TILE
        for i, x in enumerate(blocks):
            b0 = c0 + i * LANES
            y = x * lax.rsqrt(ms_all[i * n_rows:(i + 1) * n_rows] + EPS) * gains[b0 // (qk_heads * LANES)]
            for t in range(2):
                pos = pl.ds(pl.multiple_of(((2 * step + t) % tiles) * ROW_TILE, ROW_TILE), ROW_TILE)
                rows = slice(t * ROW_TILE, (t + 1) * ROW_TILE)
                yt = y[rows]
                o_ref[rows, b0:b0 + LANES] = (yt * c_ref[pos, :] + pltpu.roll(yt, LANES - 16, 1) * sa_ref[pos, :]
                                              + pltpu.roll(yt, 16, 1) * sb_ref[pos, :])


def _mod_row(i, tiles, batch):
    return jnp.where(i % tiles == 0, batch, i // tiles)


def _inproj(xz, mod, gain, w, batch, tiles, qk=None):
    d, ncols = w.shape
    rows = batch * tiles * ROW_TILE
    mod_spec = lambda t: pl.BlockSpec((None, 6, d), lambda i: (_mod_row(2 * i + t, tiles, batch), 0, 0))
    x_specs = [spec for t in range(2) for spec in _row_specs(xz, (lambda t: lambda i: 2 * i + t)(t), tiles)]
    qk_args = []
    if qk:
        tabs, q_gain, k_gain, bd = qk
        qk_args = [*tabs, jnp.tile(q_gain, 2).reshape(1, LANES), jnp.tile(k_gain, 2).reshape(1, LANES), bd]
    return pl.pallas_call(
        functools.partial(_inproj_kernel, n_x=len(_row_sources(xz)), tiles=tiles, qk_heads=A_HEADS if qk else 0),
        out_shape=jax.ShapeDtypeStruct((rows, ncols), F32),
        grid=(rows // (2 * ROW_TILE),),
        in_specs=x_specs + [mod_spec(0), mod_spec(1), _resident((1, d)), _resident((d, ncols))]
        + [_resident(a.shape) for a in qk_args],
        out_specs=pl.BlockSpec((2 * ROW_TILE, ncols), lambda i: (i, 0)),
        compiler_params=_cparams(("parallel",)),
        name="inproj",
    )(*(_row_sources(xz) * 2), mod, mod, gain.reshape(1, d), w, *qk_args)


def _post_kernel(*refs, n_src, tiles):
    groups, rest = _take(refs, [n for n in n_src for _ in range(2)])
    n_parts = len(n_src) - 1
    mod_refs, gain_ref = rest[0:2], rest[2]
    wo_refs = rest[3:3 + n_parts]
    wi_ref, wf_ref, out_ref = rest[3 + n_parts:]
    step = pl.program_id(0)
    is_context = [(2 * step + t) % tiles == 0 for t in range(2)]
    tile = lambda p, t: _row_tile(groups[2 * p + t], is_context[t])
    x1 = []
    for t in range(2):
        y = None
        for p, wo_ref in enumerate(wo_refs):
            part = _dot(tile(p, t), wo_ref[...])
            y = part if y is None else y + part
        x1.append(tile(n_parts, t) + mod_refs[t][2:3, :] * y)
    h = jnp.concatenate([_norm_mod(x1[t], gain_ref[...], mod_refs[t][3:4, :], mod_refs[t][4:5, :]).astype(BF16)
                         for t in range(2)], axis=0)
    acc = None
    for c0 in range(0, D_FF, FF_CHUNK):
        gate = _dot(h, wi_ref[:, c0:c0 + FF_CHUNK])
        up = _dot(h, wi_ref[:, D_FF + c0:D_FF + c0 + FF_CHUNK])
        a = (_silu(gate) * up).astype(BF16)
        part = _dot(a, wf_ref[c0:c0 + FF_CHUNK, :])
        acc = part if acc is None else acc + part
    for t in range(2):
        rows = slice(t * ROW_TILE, (t + 1) * ROW_TILE)
        out_ref[rows, :] = x1[t] + mod_refs[t][5:6, :] * acc[rows]


def _post(o_parts, xz, mod, gain, wo_parts, wi, wf, batch, tiles, latent_only):
    d = wi.shape[0]
    tensors = list(o_parts) + [xz]
    if latent_only:
        assert not any(isinstance(rt, tuple) for rt in tensors)
        out_tiles = tiles - 1
        src = lambda i: (i // out_tiles) * tiles + 1 + i % out_tiles
        mrow = lambda i: i // out_tiles
    else:
        out_tiles = tiles
        src = lambda i: i
        mrow = lambda i: _mod_row(i, tiles, batch)
    assert (batch * out_tiles) % 2 == 0
    tile_fn = lambda t: lambda i: src(2 * i + t)
    in_specs = [spec for rt in tensors for t in range(2) for spec in _row_specs(rt, tile_fn(t), tiles)]
    in_specs += [pl.BlockSpec((None, 6, d), (lambda t: lambda i: (mrow(2 * i + t), 0, 0))(t)) for t in range(2)]
    in_specs += [_resident((1, d))]
    in_specs += [_resident(w.shape) for w in wo_parts]
    in_specs += [_resident(wi.shape), _resident(wf.shape)]
    row_args = [a for rt in tensors for _ in range(2) for a in _row_sources(rt)]
    return pl.pallas_call(
        functools.partial(_post_kernel, n_src=tuple(len(_row_sources(rt)) for rt in tensors), tiles=tiles),
        out_shape=jax.ShapeDtypeStruct((batch * out_tiles * ROW_TILE, d), F32),
        grid=(batch * out_tiles // 2,),
        in_specs=in_specs,
        out_specs=pl.BlockSpec((2 * ROW_TILE, d), lambda i: (i, 0)),
        compiler_params=_cparams(("parallel",)),
        name="post",
    )(*row_args, mod, mod, gain.reshape(1, d), *wo_parts, wi, wf)


def _attn_kernel(qc_ref, *refs, lambda_init, seq_all):
    q_refs = refs[:Q_TILES_PER_STEP]
    (k_ref, v_ref, qg_ref, kg_ref, sg_ref, lam_ref, octx_ref, olat_ref, kb_ref, vt_ref) = refs[Q_TILES_PER_STEP:]
    step = pl.program_id(2)

    @pl.when(step == 0)
    def _prep_keys():
        def body(r, carry):
            rows = pl.ds(pl.multiple_of(r * Q_TILE, Q_TILE), Q_TILE)
            kb_ref[rows, :] = k_ref[rows, :].astype(BF16)
            vt_ref[0:LANES, rows] = v_ref[rows, :].T.astype(BF16)
            return carry
        lax.fori_loop(0, seq_all // Q_TILE, body, 0, unroll=3)
        vt_ref[LANES:, :] = jnp.ones((SUM_ROWS, seq_all), BF16)

    lam_v = lam_ref[...]
    e1 = jnp.exp(jnp.sum(lam_v[0:1, :] * lam_v[1:2, :], axis=-1, keepdims=True))
    e2 = jnp.exp(jnp.sum(lam_v[2:3, :] * lam_v[3:4, :], axis=-1, keepdims=True))
    lam = e1 - e2 + lambda_init

    def scores(q_ref, n_keys):
        qn = q_ref[...]
        lane = lax.broadcasted_iota(jnp.int32, qn.shape, 1)
        q2 = jnp.concatenate([jnp.where(lane < A_DQK, qn, 0.0).astype(BF16),
                              jnp.where(lane >= A_DQK, qn, 0.0).astype(BF16)], axis=0)
        return _dot_nt(kb_ref[0:n_keys, :], q2)

    def outputs(s, n_keys, shift):
        if shift is None:
            shift = jnp.max(s, axis=0, keepdims=True)
        p = jnp.exp2(s - shift).astype(BF16)
        pv = _dot(vt_ref[:, 0:n_keys], p)
        pv = pv[0:LANES] / pv[LANES:LANES + 1]
        o = pv[:, :Q_TILE] - lam * pv[:, Q_TILE:]
        var = jnp.mean(o * o, axis=0, keepdims=True)
        o = o * lax.rsqrt(var + EPS) * (sg_ref[...] * (1.0 - lambda_init))
        return o.T.astype(BF16)

    def run(shift):
        s_next = scores(q_refs[0], seq_all)
        for t in range(Q_TILES_PER_STEP):
            s = s_next
            if t + 1 < Q_TILES_PER_STEP:
                s_next = scores(q_refs[t + 1], seq_all)
            olat_ref[t * Q_TILE:(t + 1) * Q_TILE, :] = outputs(s, seq_all, shift)

        @pl.when(step == 0)
        def _context_tile():
            octx_ref[...] = outputs(scores(qc_ref, CTX_LEN), CTX_LEN, shift)

    bound = (A_DQK ** 0.5 * LOG2E) * (jnp.max(jnp.abs(qg_ref[...]), axis=-1, keepdims=True)
                                      * jnp.max(jnp.abs(kg_ref[...]), axis=-1, keepdims=True)) + 1.0
    small = jnp.max(bound) < SAFE_SCORE_LOG2
    pl.when(small)(lambda: run(bound))
    pl.when(jnp.logical_not(small))(lambda: run(None))


def _attn(p3, qk_gain, lam, sub_gain, lambda_init):
    batch, seq_all, _ = p3.shape
    seq = seq_all - CTX_LEN
    n_t = Q_TILES_PER_STEP
    assert seq % (n_t * Q_TILE) == 0
    qg = jnp.tile(qk_gain[0], 2).reshape(1, LANES)
    kg = jnp.tile(qk_gain[1], 2).reshape(1, LANES)
    kern = functools.partial(_attn_kernel, lambda_init=lambda_init, seq_all=seq_all)
    q_spec = lambda t: pl.BlockSpec((None, Q_TILE, LANES), lambda b, h, i: (b, 1 + n_t * i + t, h))
    o_ctx, o_lat = pl.pallas_call(
        kern,
        out_shape=(jax.ShapeDtypeStruct((batch, CTX_LEN, A_HEADS * LANES), BF16),
                   jax.ShapeDtypeStruct((batch, seq, A_HEADS * LANES), BF16)),
        grid=(batch, A_HEADS, seq // (n_t * Q_TILE)),
        in_specs=[pl.BlockSpec((None, Q_TILE, LANES), lambda b, h, i: (b, 0, h))]
        + [q_spec(t) for t in range(n_t)] + [
            pl.BlockSpec((None, seq_all, LANES), lambda b, h, i: (b, 0, A_HEADS + h)),
            pl.BlockSpec((None, seq_all, LANES), lambda b, h, i: (b, 0, 2 * A_HEADS + h)),
            _resident((1, LANES)), _resident((1, LANES)), _resident((LANES, 1)), _resident((4, A_DQK)),
        ],
        out_specs=(pl.BlockSpec((None, CTX_LEN, LANES), lambda b, h, i: (b, 0, h)),
                   pl.BlockSpec((None, n_t * Q_TILE, LANES), lambda b, h, i: (b, i, h))),
        scratch_shapes=[pltpu.VMEM((seq_all, LANES), BF16), pltpu.VMEM((LANES + SUM_ROWS, seq_all), BF16)],
        compiler_params=_cparams(("parallel", "parallel", "arbitrary")),
        name="attn",
    )(*([p3] * (n_t + 3)), qg, kg, sub_gain.reshape(LANES, 1), lam)
    return o_ctx, o_lat


MATMUL_LEVELS = 2
HGRN_LOCAL_BLOCK = 64
GLA_LOCAL_BLOCK = 128
N_STAGED = 5
SAFE_LOG2 = 100.0


def _scan_levels():
    return int(math.log2(SCAN_CHUNK))


def _scan_consts(n_heads):
    c = SCAN_CHUNK
    t = np.arange(c)[:, None]
    u = np.arange(c)[None, :]
    blocks_f = [u <= t]
    blocks_b = [u >= t]
    for lvl in range(1, MATMUL_LEVELS + 1):
        h = 1 << (lvl - 1)
        m = (t // (2 * h)) * (2 * h) + h
        second = (t % (2 * h)) >= h
        blocks_f.append(np.where(second, (u >= m) & (u <= t), (u > t) & (u <= m - 1)))
        blocks_b.append(np.where(second, (u >= m) & (u < t), (u >= t) & (u <= m - 1)))
    a_f = np.tile(np.concatenate(blocks_f, axis=0).astype(np.float32), (1, 2))
    a_b = np.tile(np.concatenate(blocks_b, axis=0).astype(np.float32), (1, 2))
    x = t ^ u
    lv = np.where(x > 0, np.floor(np.log2(np.maximum(x, 1))).astype(np.int32) + 1, 0)
    lv_f = np.where(u <= t, lv, -1).astype(np.int32)
    lv_b = np.where(u >= t, lv, -1).astype(np.int32)
    lv_f = np.tile(lv_f, (n_heads, 1))
    lv_b = np.tile(lv_b, (n_heads, 1))
    return (jnp.asarray(a_f, BF16), jnp.asarray(a_b, BF16), jnp.asarray(lv_f), jnp.asarray(lv_b))


def _scan_sums(g, a):
    g2 = g * LOG2E
    g_hi = g2.astype(BF16)
    g_lo = (g2 - g_hi.astype(F32)).astype(BF16)
    return _dot(a, jnp.concatenate([g_hi, g_lo], axis=0))


def _level_exponent(b, lvl, backward):
    c = SCAN_CHUNK
    h = 1 << (lvl - 1)
    r0 = h if backward else h - 1
    b3 = b.reshape(c // (2 * h), 2 * h, LANES)
    ref = b3[:, r0:r0 + 1, :]
    if h % 8 == 0:
        first, second = b3[:, :h, :], b3[:, h:, :]
        x = jnp.concatenate([first - ref, ref - second] if backward else [ref - first, second - ref], axis=1)
    else:
        x = -jnp.abs(b3 - ref)
    return x.reshape(c, LANES)


def _by_head(xb, n_heads):
    if n_heads == 1:
        return xb
    dk = LANES // n_heads
    lane = lax.broadcasted_iota(jnp.int32, xb.shape, 1)
    masks = [jnp.where((lane >= hh * dk) & (lane < (hh + 1) * dk), 1.0, 0.0).astype(BF16) for hh in range(n_heads)]
    return jnp.concatenate([xb * hm for hm in masks], axis=0)


def _head_block_mask(shape, row_block, col_block):
    r = lax.broadcasted_iota(jnp.int32, shape, 0) // row_block
    cc = lax.broadcasted_iota(jnp.int32, shape, 1) // col_block
    return r == cc


def _level_chunk(qb, kb, b, fine, vb, st, lv_ref, *, n_heads, backward):
    c = SCAN_CHUNK
    nl = _scan_levels()
    edge = b[0:1, :] if backward else b[c - 1:c, :]
    e_in = jnp.exp2(b).astype(BF16)
    e_out = jnp.exp2(edge - b).astype(BF16)

    def level_decay(lvl):
        if lvl <= MATMUL_LEVELS:
            return jnp.exp2(fine[(lvl - 1) * c:lvl * c]).astype(BF16)
        return jnp.exp2(_level_exponent(b, lvl, backward)).astype(BF16)

    o = _dot_nt(qb * e_in, st.astype(BF16))

    lv = lv_ref[...]
    att = jnp.zeros((n_heads * c, c), F32)
    for lvl in range(nl + 1):
        if lvl == 0:
            qh, kh = qb, kb
        else:
            el = level_decay(lvl)
            qh, kh = qb * el, kb * el
        att = jnp.where(lv == lvl, _dot_nt(_by_head(qh, n_heads), kh), att)
    attb = att.astype(BF16)
    intra = [_dot(attb[hh * c:(hh + 1) * c], vb[:, hh * LANES:(hh + 1) * LANES]) for hh in range(n_heads)]
    o = o + (intra[0] if n_heads == 1 else jnp.concatenate(intra, axis=1))

    dec = jnp.exp2(edge)
    upd = _dot_tn(vb, kb * e_out)
    if n_heads > 1:
        upd = jnp.where(_head_block_mask(upd.shape, LANES, LANES // n_heads), upd, 0.0)
    return o, dec * st + upd


def _shortcut_pairs(qp, kp, qt, kd, vb, lv_ref, *, n_heads):
    lv = lv_ref[...]
    rows = lv.shape[0]
    if qt is None:
        att = jnp.where(lv >= 0, _dot_nt(_by_head(qp, n_heads), kp), 0.0).astype(BF16)
    else:
        both = _dot_nt(jnp.concatenate([_by_head(qp, n_heads), _by_head(qt, n_heads)], axis=0), kp)
        att = jnp.where(lv == _scan_levels(), both[rows:], jnp.where(lv >= 0, both[:rows], 0.0)).astype(BF16)
    upd = _dot_tn(kd, vb)
    if n_heads > 1:
        upd = jnp.where(_head_block_mask(upd.shape, LANES // n_heads, LANES), upd, 0.0)
    return att, upd


def _shortcut_output(att, upd, qe, vb, dec, s, *, n_heads):
    c = SCAN_CHUNK
    sb = s.astype(BF16)
    outs = []
    for hh in range(n_heads):
        cols = slice(hh * LANES, (hh + 1) * LANES)
        lhs = jnp.concatenate([att[hh * c:(hh + 1) * c], qe], axis=1)
        rhs = jnp.concatenate([vb[:, cols], sb[:, cols]], axis=0)
        outs.append(_dot(lhs, rhs))
    o = outs[0] if n_heads == 1 else jnp.concatenate(outs, axis=1)
    if n_heads > 1:
        dec = jnp.concatenate([dec] * n_heads, axis=1)
    return o, dec * s + upd


def _scan_loop(load_common, load_dir, out_refs, a_refs, lv_refs, v_s, stage_refs, *, n_heads, seq_all, local_block):
    c = SCAN_CHUNK
    assert local_block in (c, c // 2)
    two_blocks = local_block < c
    n = seq_all // c
    n_ctx = CTX_LEN // c

    def chunk_rows(j):
        return pl.ds(pl.multiple_of(j * c, c), c)

    def scan(prepare, finish, state0, unroll):
        def body(i, sts):
            sts = list(sts)
            todo = []
            for u in range(unroll):
                j = i * unroll + u
                jb = jnp.where(j < n_ctx, n_ctx - 1 - j, n - 1 - (j - n_ctx))
                for d, jj in ((0, j), (1, jb)):
                    rows = chunk_rows(jj)
                    todo.append((d, rows, prepare(d, rows)))
            for d, rows, values in todo:
                out_refs[d][rows, :], sts[d] = finish(d, rows, values, sts[d])
            return tuple(sts)
        lax.fori_loop(0, n // unroll, body, (state0, state0))

    def stage(j, worst):
        gated = []
        for u in range(STAGE_UNROLL):
            rows = chunk_rows(j * STAGE_UNROLL + u)
            q, vb = load_common(rows)
            v_s[rows, :] = vb
            gated.append((rows, q, [load_dir[d](rows) for d in (0, 1)]))
        summed = [(rows, q, [(k, _scan_sums(g, a_refs[d][0:c, :])) for d, (k, g) in enumerate(kg)])
                  for rows, q, kg in gated]
        for rows, q, kb in summed:
            for d, (k, b) in enumerate(kb):
                b3 = b.reshape(c // local_block, local_block, LANES)
                r0 = local_block // 2 if d else local_block // 2 - 1
                ref = b3[:, r0:r0 + 1, :]
                x = b3 - ref
                edge = b[0:1, :] if d else b[c - 1:c, :]
                qp = q * jnp.exp2(x).reshape(c, LANES)
                kp = k * jnp.exp2(-x).reshape(c, LANES)
                entry = jnp.broadcast_to(jnp.exp2(ref), b3.shape).reshape(c, LANES)
                leave = jnp.broadcast_to(jnp.exp2(edge - ref), b3.shape).reshape(c, LANES)
                qp_s, kp_s, qt_s, qe_s, kd_s, dec_s = stage_refs[d]
                qp_s[rows, :] = qp.astype(BF16)
                kp_s[rows, :] = kp.astype(BF16)
                if two_blocks:
                    cross = jnp.exp2(ref[0] - ref[1] if d else ref[1] - ref[0])
                    qt_s[rows, :] = (qp * cross).astype(BF16)
                qe_s[rows, :] = (qp * entry).astype(BF16)
                kd_s[rows, :] = (kp * leave).astype(BF16)
                dec_s[rows, :] = jnp.broadcast_to(jnp.exp2(edge), (LANES, LANES)).T
                ends = jnp.abs(jnp.concatenate([x[:, 0:1, :], x[:, local_block - 1:, :]], axis=1))
                worst = jnp.maximum(worst, jnp.max(ends, axis=(0, 1), keepdims=True).reshape(1, LANES))
        return worst
    worst = lax.fori_loop(0, n // STAGE_UNROLL, stage, jnp.zeros((8, LANES), F32))
    safe = jnp.max(worst) < SAFE_LOG2

    def shortcut_pairs(d, rows):
        qp_s, kp_s, qt_s, _, kd_s, _ = stage_refs[d]
        return _shortcut_pairs(qp_s[rows, :], kp_s[rows, :], qt_s[rows, :] if two_blocks else None, kd_s[rows, :],
                               v_s[rows, :], lv_refs[d], n_heads=n_heads)

    def shortcut_output(d, rows, values, s):
        _, _, _, qe_s, _, dec_s = stage_refs[d]
        return _shortcut_output(*values, qe_s[rows, :], v_s[rows, :], dec_s[rows, :], s, n_heads=n_heads)

    def levels(d, rows, values, st):
        q, vb = load_common(rows)
        k, g = load_dir[d](rows)
        sums = _scan_sums(g, a_refs[d][...])
        return _level_chunk(q.astype(BF16), k.astype(BF16), sums[0:c], sums[c:], vb, st, lv_refs[d],
                            n_heads=n_heads, backward=bool(d))

    pl.when(safe)(lambda: scan(shortcut_pairs, shortcut_output, jnp.zeros((LANES, n_heads * LANES), F32), SCAN_UNROLL))
    pl.when(jnp.logical_not(safe))(lambda: scan(lambda d, rows: None, levels, jnp.zeros((n_heads * LANES, LANES), F32), 1))


def _scan_scratch(seq_all, n_heads):
    assert SCAN_CHUNK == LANES
    width = n_heads * LANES
    out = [pltpu.VMEM((seq_all, width), F32)] * 2
    stage = [pltpu.VMEM((seq_all, LANES), BF16)] * N_STAGED + [pltpu.VMEM((seq_all, LANES), F32)]
    return out + [pltpu.VMEM((seq_all, width), BF16)] + stage * 2


def _log_sigmoid(x):
    return jnp.minimum(x, 0.0) - jnp.log(1.0 + jnp.exp(-jnp.abs(x)))


def _gla_kernel(q_ref, k_ref, v_ref, g_ref, lr_ref, wgf_ref, wgb_ref, bf_ref, bb_ref, gain_ref,
                af_ref, ab_ref, lvf_ref, lvb_ref, o_ref, of_ref, ob_ref, v_s, *stage_refs, seq_all):
    def load_common(rows):
        return q_ref[rows, :] * (B_DK ** -0.5), v_ref[rows, :].astype(BF16)

    def load_dir(wg_ref, b_ref):
        def load(rows):
            pre = _dot(lr_ref[rows, :].astype(BF16), wg_ref[...]) + b_ref[...]
            return k_ref[rows, :], _log_sigmoid(pre) * (1.0 / B_GATE_NORM)
        return load

    _scan_loop(load_common, (load_dir(wgf_ref, bf_ref), load_dir(wgb_ref, bb_ref)), (of_ref, ob_ref),
               (af_ref, ab_ref), (lvf_ref, lvb_ref), v_s, (stage_refs[:N_STAGED + 1], stage_refs[N_STAGED + 1:]),
               n_heads=2, seq_all=seq_all, local_block=GLA_LOCAL_BLOCK)

    def post(j, carry):
        rows = pl.ds(pl.multiple_of(j * ROW_TILE, ROW_TILE), ROW_TILE)
        for hh in range(2):
            cols = slice(hh * LANES, (hh + 1) * LANES)
            o = of_ref[rows, cols] + ob_ref[rows, cols]
            var = jnp.mean(o * o, axis=-1, keepdims=True)
            o_ref[rows, cols] = (o * lax.rsqrt(var + EPS) * gain_ref[...] * _silu(g_ref[rows, cols])).astype(BF16)
        return carry
    lax.fori_loop(0, seq_all // ROW_TILE, post, 0, unroll=3)


def _gla(p3, w_gate_up, b_gate_up, gain, consts):
    batch, seq_all, _ = p3.shape
    pairs = B_HEADS // 2
    wg = jnp.zeros((2, pairs, LANES, LANES), F32)
    for d in range(2):
        w = w_gate_up[d].reshape(B_GATE_RANK, pairs, LANES).transpose(1, 0, 2)
        wg = wg.at[d, :, d * B_GATE_RANK:(d + 1) * B_GATE_RANK, :].set(w)
    wg = wg.astype(BF16)
    bias = b_gate_up.reshape(2, pairs, 1, LANES)
    a_f, a_b, lv_f, lv_b = consts
    col = lambda off: (lambda b, p: (b, 0, off + p))
    q_off = 3 * A_HEADS * LANES // LANES
    k_off = q_off + B_HEADS * B_DK // LANES
    v_off = (k_off * LANES + B_HEADS * B_DK) // (2 * LANES)
    g_off = v_off + pairs
    lr_off = (g_off + pairs) * 2
    wspec = pl.BlockSpec((None, LANES, LANES), lambda b, p: (p, 0, 0))
    bspec = pl.BlockSpec((None, 1, LANES), lambda b, p: (p, 0, 0))
    return pl.pallas_call(
        functools.partial(_gla_kernel, seq_all=seq_all),
        out_shape=jax.ShapeDtypeStruct((batch, seq_all, B_HEADS * LANES), BF16),
        grid=(batch, pairs),
        in_specs=[
            pl.BlockSpec((None, seq_all, LANES), col(q_off)),
            pl.BlockSpec((None, seq_all, LANES), col(k_off)),
            pl.BlockSpec((None, seq_all, 2 * LANES), col(v_off)),
            pl.BlockSpec((None, seq_all, 2 * LANES), col(g_off)),
            pl.BlockSpec((None, seq_all, LANES), lambda b, p: (b, 0, lr_off)),
            wspec, wspec, bspec, bspec,
            _resident((1, LANES)),
            _resident(a_f.shape), _resident(a_b.shape), _resident(lv_f.shape), _resident(lv_b.shape),
        ],
        out_specs=pl.BlockSpec((None, seq_all, 2 * LANES), lambda b, p: (b, 0, p)),
        scratch_shapes=_scan_scratch(seq_all, 2),
        compiler_params=_cparams(("parallel", "parallel")),
        name="gla",
    )(p3, p3, p3, p3, p3, wg[0], wg[1], bias[0], bias[1], gain.reshape(1, LANES), a_f, a_b, lv_f, lv_b)


def _hgrn_kernel(q_ref, ff_ref, fb_ref, i_ref, g_ref, lb_ref, gain_ref,
                 af_ref, ab_ref, lvf_ref, lvb_ref, o_ref, of_ref, ob_ref, v_s, *stage_refs, seq_all, layer):
    def lower_bound(d):
        raw = lb_ref[d]
        ex = jnp.exp(raw - jnp.max(raw, axis=0, keepdims=True))
        p = ex / jnp.sum(ex, axis=0, keepdims=True)
        return jnp.sum(p[1:layer + 1], axis=0, keepdims=True)

    def load_common(rows):
        return _silu(q_ref[rows, :]) * (C_DH ** -0.5), i_ref[rows, :].astype(BF16)

    def load_dir(f_ref, lb):
        def load(rows):
            f = lb + (1.0 - lb) * jax.nn.sigmoid(f_ref[rows, :])
            return 1.0 - f, jnp.log(f)
        return load

    _scan_loop(load_common, (load_dir(ff_ref, lower_bound(0)), load_dir(fb_ref, lower_bound(1))), (of_ref, ob_ref),
               (af_ref, ab_ref), (lvf_ref, lvb_ref), v_s, (stage_refs[:N_STAGED + 1], stage_refs[N_STAGED + 1:]),
               n_heads=1, seq_all=seq_all, local_block=HGRN_LOCAL_BLOCK)

    def post(j, carry):
        rows = pl.ds(pl.multiple_of(j * ROW_TILE, ROW_TILE), ROW_TILE)
        o = of_ref[rows, :] + ob_ref[rows, :]
        var = jnp.mean(o * o, axis=-1, keepdims=True)
        o_ref[rows, :] = (o * lax.rsqrt(var + EPS) * gain_ref[...] * _silu(g_ref[rows, :])).astype(BF16)
        return carry
    lax.fori_loop(0, seq_all // ROW_TILE, post, 0, unroll=3)


def _hgrn(p3, lb_raw, gain, consts, layer):
    batch, seq_all, _ = p3.shape
    a_f, a_b, lv_f, lv_b = consts
    col = lambda off: (lambda b, h: (b, 0, off + h))
    return pl.pallas_call(
        functools.partial(_hgrn_kernel, seq_all=seq_all, layer=layer),
        out_shape=jax.ShapeDtypeStruct((batch, seq_all, C_HEADS * LANES), BF16),
        grid=(batch, C_HEADS),
        in_specs=[pl.BlockSpec((None, seq_all, LANES), col(s * C_HEADS)) for s in range(5)] + [
            pl.BlockSpec((2, DEPTH, LANES), lambda b, h: (0, 0, h)),
            _resident((1, LANES)),
            _resident(a_f.shape), _resident(a_b.shape), _resident(lv_f.shape), _resident(lv_b.shape),
        ],
        out_specs=pl.BlockSpec((None, seq_all, LANES), lambda b, h: (b, 0, h)),
        scratch_shapes=_scan_scratch(seq_all, 1),
        compiler_params=_cparams(("parallel", "parallel")),
        name="hgrn",
    )(p3, p3, p3, p3, p3, lb_raw, gain.reshape(1, LANES), a_f, a_b, lv_f, lv_b)


def _rope_tables(seq):
    pos = jnp.arange(seq)
    row_ids = (pos // GRID_W).astype(F32)
    col_ids = (pos % GRID_W).astype(F32)
    n_axis = A_DQK // 2
    inv = ROPE_BASE ** (-jnp.arange(0, n_axis, 2, dtype=F32) / n_axis)
    ang_r = row_ids[:, None] * inv
    ang_c = col_ids[:, None] * inv
    zeros = jnp.zeros_like(ang_r)
    cos64 = jnp.concatenate([jnp.cos(ang_r)] * 2 + [jnp.cos(ang_c)] * 2, axis=-1)
    sa64 = jnp.concatenate([-jnp.sin(ang_r), zeros, -jnp.sin(ang_c), zeros], axis=-1)
    sb64 = jnp.concatenate([zeros, jnp.sin(ang_r), zeros, jnp.sin(ang_c)], axis=-1)
    ident = lambda v: jnp.full((CTX_LEN, LANES), v, F32)
    full = lambda t, v: jnp.concatenate([ident(v), jnp.tile(t, (1, 2))], axis=0)
    return full(cos64, 1.0), full(sa64, 0.0), full(sb64, 0.0)


def kernel(x, c, ctx, c_ctx, w_ada, b_ada, norm1_gain, norm2_gain, w_in_even, qk_gain_a, lambda_a, subln_gain_a, w_gate_up_b, b_gate_up_b, onorm_gain_b, w_out_even, w_in_odd, lb_raw_c, onorm_gain_c, w_out_odd, w_ffn_in, w_ffn_out):
    batch, seq, d = x.shape
    seq_all = CTX_LEN + seq
    assert d == D_MODEL and ctx.shape[1] == CTX_LEN == ROW_TILE == Q_TILE and seq % ROW_TILE == 0

    tiles = seq_all // ROW_TILE
    flat = lambda a: a.reshape(-1, a.shape[-1])
    xz = (flat(ctx), flat(x))
    mod_rows = -(-(batch + 1) // 8) * 8
    cc = jnp.zeros((mod_rows, d), F32).at[:batch].set(c).at[batch].set(c_ctx)
    mods = _ada(cc, w_ada, b_ada).reshape(DEPTH, mod_rows, 6, d)

    tabs = _rope_tables(seq)
    half = np.arange(LANES) // A_DQK
    bd = jnp.asarray((half[:, None] == half[None, :]) / A_DQK, BF16)
    consts_gla = _scan_consts(2)
    consts_hgrn = _scan_consts(1)

    for l in range(DEPTH):
        j = l // 2
        last = l == DEPTH - 1
        wi = w_ffn_in[l].astype(BF16)
        wf = w_ffn_out[l].astype(BF16)
        if l % 2 == 0:
            w = jnp.pad(w_in_even[j], ((0, 0), (0, EVEN_COLS_PAD - EVEN_COLS))).astype(BF16)
            p3 = _inproj(xz, mods[l], norm1_gain[l], w, batch, tiles,
                         qk=(tabs, qk_gain_a[j][0], qk_gain_a[j][1], bd)).reshape(batch, seq_all, EVEN_COLS_PAD)
            lambda_init = 0.8 - 0.6 * math.exp(-0.3 * l)
            oa_ctx, oa_lat = _attn(p3, qk_gain_a[j], lambda_a[j], subln_gain_a[j], lambda_init)
            ob = _gla(p3, w_gate_up_b[j], b_gate_up_b[j], onorm_gain_b[j], consts_gla)
            a_width = A_HEADS * LANES
            wo = w_out_even[j].astype(BF16)
            o_parts = [(flat(oa_ctx), flat(oa_lat)), flat(ob)]
            wo_parts = [wo[:a_width], wo[a_width:]]
        else:
            p3 = _inproj(xz, mods[l], norm1_gain[l], w_in_odd[j].astype(BF16), batch, tiles).reshape(batch, seq_all, ODD_COLS)
            oc = _hgrn(p3, lb_raw_c, onorm_gain_c[j], consts_hgrn, l)
            o_parts = [flat(oc)]
            wo_parts = [w_out_odd[j].astype(BF16)]
        xz = _post(o_parts, xz, mods[l], norm2_gain[l], wo_parts, wi, wf, batch, tiles, last)
    return xz.reshape(batch, seq, d)
```
